```python
import math
import jax, jax.numpy as jnp
from jax import lax
import numpy as np

D_MODEL = 2048
BATCH = 2
SEQ = 4096
DEPTH = 1
DEC_BATCH = 128
DEC_SEQ = 4
PAST_LEN = 2048
PAGE_SIZE = 128

HEAD_DIM = 64
D_ATT = D_MODEL // 2
D_SSM = D_MODEL - D_ATT
D_MIX = D_ATT + D_SSM
H_ATT = D_ATT // HEAD_DIM
SSM_HEAD_DIM = 64
H_SSM = D_SSM // SSM_HEAD_DIM
SSM_GROUPS = 2
D_STATE = 128
CONV_W = 4
SSD_CHUNK = 128
CONV_DIM = D_SSM + 2 * SSM_GROUPS * D_STATE
D_FF = ((8 * D_MODEL // 3 + 255) // 256) * 256
DILATED_PATTERNS = ((128, 1), (512, 4), (2048, 16))
WINDOW_MAX = 2048
ATT_BLOCK = 128
RPB_BUCKETS = 32
RPB_MAX_DIST = 2048
NORM_EPS = 1e-6
N_MOD = 9
PROJ_SIZES = (D_ATT, D_ATT, D_ATT, D_SSM, CONV_DIM, H_SSM)
D_IN = sum(PROJ_SIZES)

kernel_name = "hymba_ssd_dilated_macaron_step"


def rmsnorm(x, g):
    xf = x.astype(jnp.float32)
    y = xf * lax.rsqrt(jnp.mean(xf * xf, axis=-1, keepdims=True) + NORM_EPS)
    return (y * g.astype(jnp.float32)).astype(x.dtype)


def modulate(h, shift, scale):
    return h * (1 + scale[:, None, :]) + shift[:, None, :]


def swiglu(h, w_gate, w_up, w_down):
    return (jax.nn.silu(h @ w_gate) * (h @ w_up)) @ w_down


def rel_bucket(dist):
    max_exact = RPB_BUCKETS // 2
    df = jnp.maximum(dist, 1).astype(jnp.float32)
    large = max_exact + (jnp.log(df / max_exact) / math.log(RPB_MAX_DIST / max_exact)
                         * (RPB_BUCKETS - max_exact)).astype(jnp.int32)
    large = jnp.minimum(large, RPB_BUCKETS - 1)
    return jnp.where(dist < max_exact, dist, large)


def combine_dilations(outs, lses):
    w = jax.nn.softmax(jnp.stack(lses, 0), axis=0)
    return jnp.sum(w[..., None] * jnp.stack(outs, 0), axis=0)


def dilated_banded_prompt(q, k, v, rpb, window, dil):
    b, s_len, h, dh = q.shape
    span = window // dil
    blk = ATT_BLOCK
    unit = dil * blk
    s_pad = -(-s_len // unit) * unit
    n_sub = s_pad // dil
    nb = n_sub // blk

    def to_blocks(t):
        t = jnp.pad(t, ((0, 0), (0, s_pad - s_len), (0, 0), (0, 0)))
        t = t.reshape(b, n_sub, dil, h, dh).transpose(0, 2, 1, 3, 4)
        return t.reshape(b, dil, nb, blk, h, dh)

    def with_prev(t):
        prev = jnp.pad(t[:, :, :-1], ((0, 0), (0, 0), (1, 0), (0, 0), (0, 0), (0, 0)))
        return jnp.concatenate([prev, t], axis=3)

    qb = to_blocks(q)
    kk = with_prev(to_blocks(k))
    vv = with_prev(to_blocks(v))
    qi = jnp.arange(blk)[:, None]
    ki = jnp.arange(2 * blk)[None, :]
    rel = qi + blk - ki
    in_band = (rel >= 0) & (rel <= span)
    bias = rpb[rel_bucket(jnp.clip(rel, 0, span) * dil)].transpose(2, 0, 1).astype(jnp.float32)
    has_prev = (jnp.arange(nb)[:, None] > 0) | (jnp.arange(2 * blk)[None, :] >= blk)
    mask = in_band[None] & has_prev[:, None, :]
    s = jnp.einsum('brnqhd,brnkhd->brnhqk', qb, kk).astype(jnp.float32) * (HEAD_DIM ** -0.5) + bias
    s = jnp.where(mask[None, None, :, None], s, -jnp.inf)
    m = jnp.max(s, axis=-1, keepdims=True)
    p = jnp.exp(s - m)
    l = jnp.sum(p, axis=-1)
    o = jnp.einsum('brnhqk,brnkhd->brnqhd', p, vv.astype(jnp.float32)) / jnp.swapaxes(l, -1, -2)[..., None]
    lse = m[..., 0] + jnp.log(l)
    o = o.reshape(b, dil, n_sub, h, dh).transpose(0, 2, 1, 3, 4).reshape(b, s_pad, h, dh)[:, :s_len]
    lse = jnp.swapaxes(lse, -1, -2).reshape(b, dil, n_sub, h).transpose(0, 2, 1, 3).reshape(b, s_pad, h)[:, :s_len]
    return o, lse


def dilated_attn_prompt(q, k, v, rpb):
    outs, lses = [], []
    for window, dil in DILATED_PATTERNS:
        o, lse = dilated_banded_prompt(q, k, v, rpb, window, dil)
        outs.append(o)
        lses.append(lse)
    return combine_dilations(outs, lses)


def dilated_attn_sample(q, k_new, v_new, cache_k, cache_v, rpb):
    kc = jnp.concatenate([cache_k.astype(k_new.dtype), k_new], axis=1)
    vc = jnp.concatenate([cache_v.astype(v_new.dtype), v_new], axis=1)
    wb = cache_k.shape[1]
    n = q.shape[1]
    outs, lses = [], []
    for window, dil in DILATED_PATTERNS:
        span = window // dil
        j = jnp.arange(span + 1)
        idx = wb + jnp.arange(n)[:, None] - j[None, :] * dil
        valid = idx >= 0
        idx = jnp.maximum(idx, 0)
        kg = kc[:, idx]
        vg = vc[:, idx]
        bias = rpb[rel_bucket(j * dil)].T.astype(jnp.float32)
        s = jnp.einsum('bqhd,bqjhd->bhqj', q, kg).astype(jnp.float32) * (HEAD_DIM ** -0.5) + bias[:, None, :]
        s = jnp.where(valid[None, None], s, -jnp.inf)
        m = jnp.max(s, axis=-1, keepdims=True)
        p = jnp.exp(s - m)
        l = jnp.sum(p, axis=-1)
        o = jnp.einsum('bhqj,bqjhd->bqhd', p, vg.astype(jnp.float32)) / jnp.swapaxes(l, 1, 2)[..., None]
        outs.append(o)
        lses.append(jnp.swapaxes(m[..., 0] + jnp.log(l), 1, 2))
    return combine_dilations(outs, lses)


def ssd_scan(xs, dt, a, bm, cm, h0):
    bt, L, g, hg, pdim = xs.shape
    n = bm.shape[-1]
    qn = SSD_CHUNK if L % SSD_CHUNK == 0 else L
    nc = L // qn
    xdt = (xs.astype(jnp.float32) * dt[..., None]).reshape(bt, nc, qn, g, hg, pdim)
    a_cum = jnp.cumsum((dt * a).reshape(bt, nc, qn, g, hg), axis=2)
    bc = bm.reshape(bt, nc, qn, g, n)
    cc = cm.reshape(bt, nc, qn, g, n)
    seg = a_cum[:, :, :, None] - a_cum[:, :, None, :]
    causal = jnp.tril(jnp.ones((qn, qn), dtype=bool))[:, :, None, None]
    decay = jnp.exp(jnp.where(causal, seg, -jnp.inf))
    y_diag = jnp.einsum('bctgn,bcsgn,bctsgh,bcsghp->bctghp', cc, bc, decay, xdt)
    decay_end = jnp.exp(a_cum[:, :, -1:] - a_cum)
    chunk_states = jnp.einsum('bcsgn,bcsgh,bcsghp->bcghpn', bc, decay_end, xdt)
    chunk_decay = jnp.exp(a_cum[:, :, -1])

    def step(h, inp):
        dec, st = inp
        return dec[..., None, None] * h + st, h

    h_last, h_in = lax.scan(step, h0.astype(jnp.float32),
                            (jnp.swapaxes(chunk_decay, 0, 1), jnp.swapaxes(chunk_states, 0, 1)))
    h_in = jnp.swapaxes(h_in, 0, 1)
    y_off = jnp.einsum('bctgn,bctgh,bcghpn->bctghp', cc, jnp.exp(a_cum), h_in)
    return (y_diag + y_off).reshape(bt, L, g, hg, pdim), h_last


def ssd_branch(z, xbc, dt_raw, conv_buf, h0, p):
    bt, L, _ = xbc.shape
    hg = H_SSM // SSM_GROUPS
    xpad = jnp.concatenate([conv_buf.astype(xbc.dtype), xbc], axis=1)
    conv = p['conv_b'] + xpad[:, 0:L] * p['conv_w'][0]
    for i in range(1, CONV_W):
        conv = conv + xpad[:, i:i + L] * p['conv_w'][i]
    act = jax.nn.silu(conv)
    xs, bm, cm = jnp.split(act, [D_SSM, D_SSM + SSM_GROUPS * D_STATE], axis=-1)
    xs = xs.reshape(bt, L, SSM_GROUPS, hg, SSM_HEAD_DIM)
    bm = bm.reshape(bt, L, SSM_GROUPS, D_STATE)
    cm = cm.reshape(bt, L, SSM_GROUPS, D_STATE)
    dt = jax.nn.softplus(dt_raw.astype(jnp.float32) + p['dt_bias'].astype(jnp.float32)).reshape(bt, L, SSM_GROUPS, hg)
    a = -jnp.exp(p['a_log'].astype(jnp.float32)).reshape(SSM_GROUPS, hg)
    y, h_last = ssd_scan(xs, dt, a, bm, cm, h0.reshape(bt, SSM_GROUPS, hg, SSM_HEAD_DIM, D_STATE))
    y = y + p['d_skip'].astype(jnp.float32).reshape(SSM_GROUPS, hg)[:, :, None] * xs
    gated = y.reshape(bt, L, SSM_GROUPS, hg * SSM_HEAD_DIM) * jax.nn.silu(z.astype(jnp.float32)).reshape(bt, L, SSM_GROUPS, hg * SSM_HEAD_DIM)
    yn = gated * lax.rsqrt(jnp.mean(gated * gated, axis=-1, keepdims=True) + NORM_EPS)
    yn = yn.reshape(bt, L, D_SSM) * p['g_ssm_out'].astype(jnp.float32)
    return yn.astype(z.dtype), h_last.reshape(bt, H_SSM, SSM_HEAD_DIM, D_STATE), xpad[:, L:]


def decoder_layer(x, c, p, attend, conv_buf, h0):
    b, L, _ = x.shape
    mod = jax.nn.silu(c) @ p['w_ada'] + p['b_ada']
    sh1, sc1, g1, sh2, sc2, g2, sh3, sc3, g3 = jnp.split(mod, N_MOD, axis=-1)
    h = modulate(rmsnorm(x, p['g_ffn1']), sh1, sc1)
    x = x + 0.5 * g1[:, None, :] * swiglu(h, p['w_gate1'], p['w_up1'], p['w_down1'])
    h = modulate(rmsnorm(x, p['g_mix']), sh2, sc2)
    proj = h @ p['w_in']
    q, k, v, z, xbc, dt_raw = jnp.split(proj, list(np.cumsum(PROJ_SIZES)[:-1]), axis=-1)
    q = q.reshape(b, L, H_ATT, HEAD_DIM)
    k = k.reshape(b, L, H_ATT, HEAD_DIM)
    v = v.reshape(b, L, H_ATT, HEAD_DIM)
    att_o, k_rows, v_rows = attend(q, k, v)
    att_o = rmsnorm(att_o.reshape(b, L, D_ATT).astype(x.dtype), p['g_attn_out'])
    ssd_o, h_last, conv_new = ssd_branch(z, xbc, dt_raw, conv_buf, h0, p)
    mix = jnp.concatenate([att_o, ssd_o], axis=-1) @ p['w_out']
    x = x + g2[:, None, :] * mix
    h = modulate(rmsnorm(x, p['g_ffn2']), sh3, sc3)
    x = x + 0.5 * g3[:, None, :] * swiglu(h, p['w_gate2'], p['w_up2'], p['w_down2'])
    return x, k_rows, v_rows, h_last, conv_new


def setup_inputs(seed: int = 0) -> dict:
    key = jax.random.key(seed)
    ks = jax.random.split(key, 32)
    f32 = jnp.float32

    def nrm(k, shape, scale):
        return jax.random.normal(k, shape, f32) * scale

    wbuf = min(WINDOW_MAX, PAST_LEN)
    dt0 = jnp.exp(jax.random.uniform(ks[18], (DEPTH, H_SSM), f32, math.log(1e-3), math.log(1e-1)))
    return {
        'x_prompt': nrm(ks[0], (BATCH, SEQ, D_MODEL), 1.0),
        'x_sample': nrm(ks[1], (DEC_BATCH, DEC_SEQ, D_MODEL), 1.0),
        'cache_k': nrm(ks[2], (DEPTH, DEC_BATCH, wbuf, H_ATT, HEAD_DIM), 1.0),
        'cache_v': nrm(ks[3], (DEPTH, DEC_BATCH, wbuf, H_ATT, HEAD_DIM), 1.0),
        'state_ssm': nrm(ks[4], (DEPTH, DEC_BATCH, H_SSM, SSM_HEAD_DIM, D_STATE), 0.1),
        'state_conv': nrm(ks[5], (DEPTH, DEC_BATCH, CONV_W - 1, CONV_DIM), 1.0),
        'c_prompt': nrm(ks[6], (BATCH, D_MODEL), 1.0),
        'c_sample': nrm(ks[7], (DEC_BATCH, D_MODEL), 1.0),
        'rpb_table': nrm(ks[8], (RPB_BUCKETS, H_ATT), 0.5),
        'w_ada': nrm(ks[9], (DEPTH, D_MODEL, N_MOD * D_MODEL), 0.5 * D_MODEL ** -0.5),
        'b_ada': nrm(ks[10], (DEPTH, N_MOD * D_MODEL), 0.02),
        'g_ffn1': 1.0 + nrm(ks[11], (DEPTH, D_MODEL), 0.05),
        'w_gate1': nrm(ks[12], (DEPTH, D_MODEL, D_FF), D_MODEL ** -0.5),
        'w_up1': nrm(ks[13], (DEPTH, D_MODEL, D_FF), D_MODEL ** -0.5),
        'w_down1': nrm(ks[14], (DEPTH, D_FF, D_MODEL), D_FF ** -0.5),
        'g_mix': 1.0 + nrm(ks[15], (DEPTH, D_MODEL), 0.05),
        'w_in': nrm(ks[16], (DEPTH, D_MODEL, D_IN), D_MODEL ** -0.5),
        'conv_w': nrm(ks[17], (DEPTH, CONV_W, CONV_DIM), CONV_W ** -0.5),
        'conv_b': nrm(ks[19], (DEPTH, CONV_DIM), 0.02),
        'dt_bias': dt0 + jnp.log(-jnp.expm1(-dt0)),
        'a_log': jnp.log(jax.random.uniform(ks[20], (DEPTH, H_SSM), f32, 1.0, 16.0)),
        'd_skip': 1.0 + nrm(ks[21], (DEPTH, H_SSM), 0.1),
        'g_ssm_out': 1.0 + nrm(ks[22], (DEPTH, D_SSM), 0.05),
        'g_attn_out': 1.0 + nrm(ks[23], (DEPTH, D_ATT), 0.05),
        'w_out': nrm(ks[24], (DEPTH, D_MIX, D_MODEL), D_MIX ** -0.5),
        'g_ffn2': 1.0 + nrm(ks[25], (DEPTH, D_MODEL), 0.05),
        'w_gate2': nrm(ks[26], (DEPTH, D_MODEL, D_FF), D_MODEL ** -0.5),
        'w_up2': nrm(ks[27], (DEPTH, D_MODEL, D_FF), D_MODEL ** -0.5),
        'w_down2': nrm(ks[28], (DEPTH, D_FF, D_MODEL), D_FF ** -0.5),
        'g_final': 1.0 + nrm(ks[29], (D_MODEL,), 0.05),
    }


def reference(x_prompt, x_sample, cache_k, cache_v, state_ssm, state_conv, c_prompt, c_sample,
              rpb_table, w_ada, b_ada, g_ffn1, w_gate1, w_up1, w_down1, g_mix, w_in, conv_w, conv_b,
              dt_bias, a_log, d_skip, g_ssm_out, g_attn_out, w_out, g_ffn2, w_gate2, w_up2, w_down2, g_final):
    wbp = min(WINDOW_MAX, x_prompt.shape[1])
    yp, ys = x_prompt, x_sample
    kp_l, vp_l, ks_l, vs_l, hp_l, hs_l, cp_l, cs_l = [], [], [], [], [], [], [], []
    for layer in range(DEPTH):
        p = {'w_ada': w_ada[layer], 'b_ada': b_ada[layer], 'g_ffn1': g_ffn1[layer],
             'w_gate1': w_gate1[layer], 'w_up1': w_up1[layer], 'w_down1': w_down1[layer],
             'g_mix': g_mix[layer], 'w_in': w_in[layer], 'conv_w': conv_w[layer], 'conv_b': conv_b[layer],
             'dt_bias': dt_bias[layer], 'a_log': a_log[layer], 'd_skip': d_skip[layer],
             'g_ssm_out': g_ssm_out[layer], 'g_attn_out': g_attn_out[layer], 'w_out': w_out[layer],
             'g_ffn2': g_ffn2[layer], 'w_gate2': w_gate2[layer], 'w_up2': w_up2[layer], 'w_down2': w_down2[layer]}

        def attend_prompt(q, k, v):
            return dilated_attn_prompt(q, k, v, rpb_table), k[:, -wbp:], v[:, -wbp:]

        def attend_sample(q, k, v, ck=cache_k[layer], cv=cache_v[layer]):
            return dilated_attn_sample(q, k, v, ck, cv, rpb_table), k, v

        conv0 = jnp.zeros((yp.shape[0], CONV_W - 1, CONV_DIM), yp.dtype)
        h0 = jnp.zeros((yp.shape[0], H_SSM, SSM_HEAD_DIM, D_STATE), jnp.float32)
        yp, kp, vp, hp, cp = decoder_layer(yp, c_prompt, p, attend_prompt, conv0, h0)
        ys, kn, vn, hs, cs = decoder_layer(ys, c_sample, p, attend_sample, state_conv[layer], state_ssm[layer])
        kp_l.append(kp); vp_l.append(vp); ks_l.append(kn); vs_l.append(vn)
        hp_l.append(hp); hs_l.append(hs); cp_l.append(cp); cs_l.append(cs)
    y_prompt = rmsnorm(yp, g_final)
    y_sample = rmsnorm(ys, g_final)
    return (y_prompt, y_sample, jnp.stack(kp_l), jnp.stack(vp_l), jnp.stack(ks_l), jnp.stack(vs_l),
            jnp.stack(hp_l), jnp.stack(hs_l), jnp.stack(cp_l), jnp.stack(cs_l))
```

```python
import functools
import math

import jax
import jax.numpy as jnp
import numpy as np
from jax import lax
from jax.experimental import pallas as pl
from jax.experimental.pallas import tpu as pltpu

D_MODEL = 2048
HEAD_DIM = 64
D_ATT = 1024
D_SSM = 1024
H_ATT = 16
SSM_HEAD_DIM = 64
H_SSM = 16
SSM_GROUPS = 2
D_STATE = 128
CONV_W = 4
SSD_CHUNK = 128
CONV_DIM = D_SSM + 2 * SSM_GROUPS * D_STATE
D_FF = 5632
DILATED_PATTERNS = ((128, 1), (512, 4), (2048, 16))
WINDOW_MAX = 2048
ATT_BLOCK = 128
RPB_BUCKETS = 32
RPB_MAX_DIST = 2048
NORM_EPS = 1e-6
N_MOD = 9
PROJ_SIZES = (D_ATT, D_ATT, D_ATT, D_SSM, CONV_DIM, H_SSM)
D_IN = sum(PROJ_SIZES)
D_IN_PAD = 6144

VMEM_LIMIT_BYTES = 56 * 1024 * 1024

BF16 = jnp.bfloat16
F32 = jnp.float32


def _silu(v):
    return v * jax.nn.sigmoid(v)


def _norm_mod(x, g, shift, scale):
    y = x * lax.rsqrt(jnp.mean(x * x, axis=-1, keepdims=True) + NORM_EPS)
    return (y * g) * (1.0 + scale) + shift


def _mod_kernel(c_ref, w_ref, b_ref, o_ref, a_ref):
    @pl.when(pl.program_id(0) == 0)
    def _():
        a_ref[...] = _silu(c_ref[...]).astype(BF16)

    o_ref[...] = jnp.dot(a_ref[...], w_ref[...].astype(BF16),
                         preferred_element_type=F32) + b_ref[...]


def _mod_call(c_all, w_ada, b_ada):
    rows = c_all.shape[0]
    n = w_ada.shape[1]
    tn = 1024
    return pl.pallas_call(
        _mod_kernel,
        out_shape=jax.ShapeDtypeStruct((rows, n), F32),
        grid=(n // tn,),
        in_specs=[
            pl.BlockSpec((rows, D_MODEL), lambda j: (0, 0)),
            pl.BlockSpec((D_MODEL, tn), lambda j: (0, j)),
            pl.BlockSpec((1, tn), lambda j: (0, j)),
        ],
        out_specs=pl.BlockSpec((rows, tn), lambda j: (0, j)),
        scratch_shapes=[pltpu.VMEM((rows, D_MODEL), BF16)],
        compiler_params=pltpu.CompilerParams(
            dimension_semantics=("arbitrary",), vmem_limit_bytes=VMEM_LIMIT_BYTES),
        name="adaln_mod",
    )(c_all, w_ada, b_ada.reshape(1, n))


def _ffn_kernel(x_ref, g_ref, sh_ref, sc_ref, gt_ref, wg_ref, wu_ref, wd_ref, gf_ref,
                o_ref, h_ref, *, final_norm):
    j = pl.program_id(1)

    @pl.when(j == 0)
    def _():
        h_ref[...] = _norm_mod(x_ref[...], g_ref[...], sh_ref[...], sc_ref[...]).astype(BF16)
        o_ref[...] = jnp.zeros_like(o_ref)

    h = h_ref[...]
    a = jnp.dot(h, wg_ref[...], preferred_element_type=F32)
    b = jnp.dot(h, wu_ref[...], preferred_element_type=F32)
    s = (_silu(a) * b).astype(BF16)
    o_ref[...] += jnp.dot(s, wd_ref[...], preferred_element_type=F32)

    @pl.when(j == pl.num_programs(1) - 1)
    def _():
        y = x_ref[...] + 0.5 * gt_ref[...] * o_ref[...]
        if final_norm:
            y = y * lax.rsqrt(jnp.mean(y * y, axis=-1, keepdims=True) + NORM_EPS) * gf_ref[...]
        o_ref[...] = y


def _mod_spec(mod, tm, rows_per_mod):
    if mod.ndim == 3:
        tiles_per_row = rows_per_mod // tm
        return pl.BlockSpec((None, 1, D_MODEL), lambda i, j: (i // tiles_per_row, 0, 0))
    return pl.BlockSpec((tm, D_MODEL), lambda i, j: (i, 0))


def _ffn_call(x, g, sh, sc, gt, wg, wu, wd, g_final, *, tm, rows_per_mod, final_norm):
    m = x.shape[0]
    tf = 512
    row = lambda i, j: (i, 0)
    const = lambda i, j: (0, 0)
    return pl.pallas_call(
        functools.partial(_ffn_kernel, final_norm=final_norm),
        out_shape=jax.ShapeDtypeStruct((m, D_MODEL), F32),
        grid=(m // tm, D_FF // tf),
        in_specs=[
            pl.BlockSpec((tm, D_MODEL), row, pipeline_mode=pl.Buffered(1)),
            pl.BlockSpec((1, D_MODEL), const),
            _mod_spec(sh, tm, rows_per_mod),
            _mod_spec(sc, tm, rows_per_mod),
            _mod_spec(gt, tm, rows_per_mod),
            pl.BlockSpec((D_MODEL, tf), lambda i, j: (0, j)),
            pl.BlockSpec((D_MODEL, tf), lambda i, j: (0, j)),
            pl.BlockSpec((tf, D_MODEL), lambda i, j: (j, 0)),
            pl.BlockSpec((1, D_MODEL), const),
        ],
        out_specs=pl.BlockSpec((tm, D_MODEL), row),
        scratch_shapes=[pltpu.VMEM((tm, D_MODEL), BF16)],
        compiler_params=pltpu.CompilerParams(
            dimension_semantics=("parallel", "arbitrary"), vmem_limit_bytes=VMEM_LIMIT_BYTES),
        name="ffn_final" if final_norm else "ffn",
    )(x, g.reshape(1, D_MODEL), sh, sc, gt, wg, wu, wd, g_final.reshape(1, D_MODEL))


def _inproj_kernel(x_ref, g_ref, sh_ref, sc_ref, w_ref, o_ref, h_ref):
    @pl.when(pl.program_id(1) == 0)
    def _():
        h_ref[...] = _norm_mod(x_ref[...], g_ref[...], sh_ref[...], sc_ref[...]).astype(BF16)

    o_ref[...] = jnp.dot(h_ref[...], w_ref[...], preferred_element_type=F32)


def _inproj_call(x, g, sh, sc, w_pad, *, tm, rows_per_mod):
    m = x.shape[0]
    tn = 1024
    return pl.pallas_call(
        _inproj_kernel,
        out_shape=jax.ShapeDtypeStruct((m, D_IN_PAD), F32),
        grid=(m // tm, D_IN_PAD // tn),
        in_specs=[
            pl.BlockSpec((tm, D_MODEL), lambda i, j: (i, 0)),
            pl.BlockSpec((1, D_MODEL), lambda i, j: (0, 0)),
            _mod_spec(sh, tm, rows_per_mod),
            _mod_spec(sc, tm, rows_per_mod),
            pl.BlockSpec((D_MODEL, tn), lambda i, j: (0, j)),
        ],
        out_specs=pl.BlockSpec((tm, tn), lambda i, j: (i, j)),
        scratch_shapes=[pltpu.VMEM((tm, D_MODEL), BF16)],
        compiler_params=pltpu.CompilerParams(
            dimension_semantics=("parallel", "arbitrary"), vmem_limit_bytes=VMEM_LIMIT_BYTES),
        name="in_proj",
    )(x, g.reshape(1, D_MODEL), sh, sc, w_pad)


def _outproj_kernel(x_ref, att_ref, ssd_ref, gt_ref, wa_ref, ws_ref, o_ref):
    mix = jnp.dot(att_ref[...].astype(BF16), wa_ref[...], preferred_element_type=F32)
    mix += jnp.dot(ssd_ref[...].astype(BF16), ws_ref[...], preferred_element_type=F32)
    o_ref[...] = x_ref[...] + gt_ref[...] * mix


def _outproj_call(x, att, ssd, gt, w_att, w_ssd, *, tm, rows_per_mod):
    m = x.shape[0]
    if gt.ndim == 3:
        tiles_per_row = rows_per_mod // tm
        gt_spec = pl.BlockSpec((None, 1, D_MODEL), lambda i: (i // tiles_per_row, 0, 0))
    else:
        gt_spec = pl.BlockSpec((tm, D_MODEL), lambda i: (i, 0))
    return pl.pallas_call(
        _outproj_kernel,
        out_shape=jax.ShapeDtypeStruct((m, D_MODEL), F32),
        grid=(m // tm,),
        in_specs=[
            pl.BlockSpec((tm, D_MODEL), lambda i: (i, 0)),
            pl.BlockSpec((tm, D_ATT), lambda i: (i, 0)),
            pl.BlockSpec((tm, D_SSM), lambda i: (i, 0)),
            gt_spec,
            pl.BlockSpec((D_ATT, D_MODEL), lambda i: (0, 0)),
            pl.BlockSpec((D_SSM, D_MODEL), lambda i: (0, 0)),
        ],
        out_specs=pl.BlockSpec((tm, D_MODEL), lambda i: (i, 0)),
        compiler_params=pltpu.CompilerParams(
            dimension_semantics=("parallel",), vmem_limit_bytes=VMEM_LIMIT_BYTES),
        name="out_proj",
    )(x, att, ssd, gt, w_att, w_ssd)


def _rmsnorm(x, g):
    xf = x.astype(jnp.float32)
    y = xf * lax.rsqrt(jnp.mean(xf * xf, axis=-1, keepdims=True) + NORM_EPS)
    return (y * g.astype(jnp.float32)).astype(x.dtype)


def _rel_bucket(dist):
    max_exact = RPB_BUCKETS // 2
    df = jnp.maximum(dist, 1).astype(jnp.float32)
    large = max_exact + (jnp.log(df / max_exact) / math.log(RPB_MAX_DIST / max_exact)
                         * (RPB_BUCKETS - max_exact)).astype(jnp.int32)
    large = jnp.minimum(large, RPB_BUCKETS - 1)
    return jnp.where(dist < max_exact, dist, large)


def _combine_dilations(outs, lses):
    w = jax.nn.softmax(jnp.stack(lses, 0), axis=0)
    return jnp.sum(w[..., None] * jnp.stack(outs, 0), axis=0)


def _dilated_banded_prompt(q, k, v, rpb, window, dil):
    b, s_len, h, dh = q.shape
    span = window // dil
    blk = ATT_BLOCK
    unit = dil * blk
    s_pad = -(-s_len // unit) * unit
    n_sub = s_pad // dil
    nb = n_sub // blk

    def to_blocks(t):
        t = jnp.pad(t, ((0, 0), (0, s_pad - s_len), (0, 0), (0, 0)))
        t = t.reshape(b, n_sub, dil, h, dh).transpose(0, 2, 1, 3, 4)
        return t.reshape(b, dil, nb, blk, h, dh)

    def with_prev(t):
        prev = jnp.pad(t[:, :, :-1], ((0, 0), (0, 0), (1, 0), (0, 0), (0, 0), (0, 0)))
        return jnp.concatenate([prev, t], axis=3)

    qb = to_blocks(q)
    kk = with_prev(to_blocks(k))
    vv = with_prev(to_blocks(v))
    qi = jnp.arange(blk)[:, None]
    ki = jnp.arange(2 * blk)[None, :]
    rel = qi + blk - ki
    in_band = (rel >= 0) & (rel <= span)
    bias = rpb[_rel_bucket(jnp.clip(rel, 0, span) * dil)].transpose(2, 0, 1).astype(jnp.float32)
    has_prev = (jnp.arange(nb)[:, None] > 0) | (jnp.arange(2 * blk)[None, :] >= blk)
    mask = in_band[None] & has_prev[:, None, :]
    s = jnp.einsum('brnqhd,brnkhd->brnhqk', qb, kk).astype(jnp.float32) * (HEAD_DIM ** -0.5) + bias
    s = jnp.where(mask[None, None, :, None], s, -jnp.inf)
    m = jnp.max(s, axis=-1, keepdims=True)
    p = jnp.exp(s - m)
    l = jnp.sum(p, axis=-1)
    o = jnp.einsum('brnhqk,brnkhd->brnqhd', p, vv.astype(jnp.float32)) / jnp.swapaxes(l, -1, -2)[..., None]
    lse = m[..., 0] + jnp.log(l)
    o = o.reshape(b, dil, n_sub, h, dh).transpose(0, 2, 1, 3, 4).reshape(b, s_pad, h, dh)[:, :s_len]
    lse = jnp.swapaxes(lse, -1, -2).reshape(b, dil, n_sub, h).transpose(0, 2, 1, 3).reshape(b, s_pad, h)[:, :s_len]
    return o, lse


def _dilated_attn_prompt(q, k, v, rpb):
    outs, lses = [], []
    for window, dil in DILATED_PATTERNS:
        o, lse = _dilated_banded_prompt(q, k, v, rpb, window, dil)
        outs.append(o)
        lses.append(lse)
    return _combine_dilations(outs, lses)


def _dilated_attn_sample(q, k_new, v_new, cache_k, cache_v, rpb):
    kc = jnp.concatenate([cache_k.astype(k_new.dtype), k_new], axis=1)
    vc = jnp.concatenate([cache_v.astype(v_new.dtype), v_new], axis=1)
    wb = cache_k.shape[1]
    n = q.shape[1]
    outs, lses = [], []
    for window, dil in DILATED_PATTERNS:
        span = window // dil
        j = jnp.arange(span + 1)
        idx = wb + jnp.arange(n)[:, None] - j[None, :] * dil
        valid = idx >= 0
        idx = jnp.maximum(idx, 0)
        kg = kc[:, idx]
        vg = vc[:, idx]
        bias = rpb[_rel_bucket(j * dil)].T.astype(jnp.float32)
        s = jnp.einsum('bqhd,bqjhd->bhqj', q, kg).astype(jnp.float32) * (HEAD_DIM ** -0.5) + bias[:, None, :]
        s = jnp.where(valid[None, None], s, -jnp.inf)
        m = jnp.max(s, axis=-1, keepdims=True)
        p = jnp.exp(s - m)
        l = jnp.sum(p, axis=-1)
        o = jnp.einsum('bhqj,bqjhd->bqhd', p, vg.astype(jnp.float32)) / jnp.swapaxes(l, 1, 2)[..., None]
        outs.append(o)
        lses.append(jnp.swapaxes(m[..., 0] + jnp.log(l), 1, 2))
    return _combine_dilations(outs, lses)


def _ssd_scan(xs, dt, a, bm, cm, h0):
    bt, L, g, hg, pdim = xs.shape
    n = bm.shape[-1]
    qn = SSD_CHUNK if L % SSD_CHUNK == 0 else L
    nc = L // qn
    xdt = (xs.astype(jnp.float32) * dt[..., None]).reshape(bt, nc, qn, g, hg, pdim)
    a_cum = jnp.cumsum((dt * a).reshape(bt, nc, qn, g, hg), axis=2)
    bc = bm.reshape(bt, nc, qn, g, n)
    cc = cm.reshape(bt, nc, qn, g, n)
    seg = a_cum[:, :, :, None] - a_cum[:, :, None, :]
    causal = jnp.tril(jnp.ones((qn, qn), dtype=bool))[:, :, None, None]
    decay = jnp.exp(jnp.where(causal, seg, -jnp.inf))
    y_diag = jnp.einsum('bctgn,bcsgn,bctsgh,bcsghp->bctghp', cc, bc, decay, xdt)
    decay_end = jnp.exp(a_cum[:, :, -1:] - a_cum)
    chunk_states = jnp.einsum('bcsgn,bcsgh,bcsghp->bcghpn', bc, decay_end, xdt)
    chunk_decay = jnp.exp(a_cum[:, :, -1])

    def step(h, inp):
        dec, st = inp
        return dec[..., None, None] * h + st, h

    h_last, h_in = lax.scan(step, h0.astype(jnp.float32),
                            (jnp.swapaxes(chunk_decay, 0, 1), jnp.swapaxes(chunk_states, 0, 1)))
    h_in = jnp.swapaxes(h_in, 0, 1)
    y_off = jnp.einsum('bctgn,bctgh,bcghpn->bctghp', cc, jnp.exp(a_cum), h_in)
    return (y_diag + y_off).reshape(bt, L, g, hg, pdim), h_last


def _ssd_branch(z, xbc, dt_raw, conv_buf, h0, p):
    bt, L, _ = xbc.shape
    hg = H_SSM // SSM_GROUPS
    xpad = jnp.concatenate([conv_buf.astype(xbc.dtype), xbc], axis=1)
    conv = p['conv_b'] + xpad[:, 0:L] * p['conv_w'][0]
    for i in range(1, CONV_W):
        conv = conv + xpad[:, i:i + L] * p['conv_w'][i]
    act = jax.nn.silu(conv)
    xs, bm, cm = jnp.split(act, [D_SSM, D_SSM + SSM_GROUPS * D_STATE], axis=-1)
    xs = xs.reshape(bt, L, SSM_GROUPS, hg, SSM_HEAD_DIM)
    bm = bm.reshape(bt, L, SSM_GROUPS, D_STATE)
    cm = cm.reshape(bt, L, SSM_GROUPS, D_STATE)
    dt = jax.nn.softplus(dt_raw.astype(jnp.float32) + p['dt_bias'].astype(jnp.float32)).reshape(bt, L, SSM_GROUPS, hg)
    a = -jnp.exp(p['a_log'].astype(jnp.float32)).reshape(SSM_GROUPS, hg)
    y, h_last = _ssd_scan(xs, dt, a, bm, cm, h0.reshape(bt, SSM_GROUPS, hg, SSM_HEAD_DIM, D_STATE))
    y = y + p['d_skip'].astype(jnp.float32).reshape(SSM_GROUPS, hg)[:, :, None] * xs
    gated = y.reshape(bt, L, SSM_GROUPS, hg * SSM_HEAD_DIM) * jax.nn.silu(z.astype(jnp.float32)).reshape(bt, L, SSM_GROUPS, hg * SSM_HEAD_DIM)
    yn = gated * lax.rsqrt(jnp.mean(gated * gated, axis=-1, keepdims=True) + NORM_EPS)
    yn = yn.reshape(bt, L, D_SSM) * p['g_ssm_out'].astype(jnp.float32)
    return yn.astype(z.dtype), h_last.reshape(bt, H_SSM, SSM_HEAD_DIM, D_STATE), xpad[:, L:]


def _layer(x, mods, p, attend, conv_buf, h0, *, tm, per_token, g_final):
    b, L, _ = x.shape
    m = b * L
    sh1, sc1, g1, sh2, sc2, g2, sh3, sc3, g3 = mods
    x2 = x.reshape(m, D_MODEL)
    kw = dict(tm=tm, rows_per_mod=L)
    x2 = _ffn_call(x2, p['g_ffn1'], sh1, sc1, g1, p['w_gate1'], p['w_up1'], p['w_down1'],
                   g_final, final_norm=False, **kw)
    kw_small = dict(tm=min(tm, 512), rows_per_mod=L)
    proj = _inproj_call(x2, p['g_mix'], sh2, sc2, p['w_in_pad'], **kw_small).reshape(b, L, D_IN_PAD)
    q = proj[..., 0:D_ATT].reshape(b, L, H_ATT, HEAD_DIM)
    k = proj[..., D_ATT:2 * D_ATT].reshape(b, L, H_ATT, HEAD_DIM)
    v = proj[..., 2 * D_ATT:3 * D_ATT].reshape(b, L, H_ATT, HEAD_DIM)
    z = proj[..., 3 * D_ATT:3 * D_ATT + D_SSM]
    o = 3 * D_ATT + D_SSM
    xbc = proj[..., o:o + CONV_DIM]
    dt_raw = proj[..., o + CONV_DIM:o + CONV_DIM + H_SSM]
    att_o, k_rows, v_rows = attend(q, k, v)
    att_o = _rmsnorm(att_o.reshape(b, L, D_ATT), p['g_attn_out'])
    ssd_o, h_last, conv_new = _ssd_branch(z, xbc, dt_raw, conv_buf, h0, p)
    x2 = _outproj_call(x2, att_o.reshape(m, D_ATT), ssd_o.reshape(m, D_SSM), g2,
                       p['w_out_att'], p['w_out_ssm'], **kw_small)
    x2 = _ffn_call(x2, p['g_ffn2'], sh3, sc3, g3, p['w_gate2'], p['w_up2'], p['w_down2'],
                   g_final, final_norm=True, **kw)
    return x2.reshape(b, L, D_MODEL), k_rows, v_rows, h_last, conv_new


def kernel(x_prompt, x_sample, cache_k, cache_v, state_ssm, state_conv, c_prompt, c_sample,
           rpb_table, w_ada, b_ada, g_ffn1, w_gate1, w_up1, w_down1, g_mix, w_in, conv_w, conv_b,
           dt_bias, a_log, d_skip, g_ssm_out, g_attn_out, w_out, g_ffn2, w_gate2, w_up2, w_down2, g_final):
    bp, sp, _ = x_prompt.shape
    bs, ls, _ = x_sample.shape
    wbp = min(WINDOW_MAX, sp)
    layer = 0
    w_in_pad = jnp.pad(w_in[layer].astype(BF16), ((0, 0), (0, D_IN_PAD - D_IN)))
    p = {'g_ffn1': g_ffn1[layer], 'w_gate1': w_gate1[layer].astype(BF16), 'w_up1': w_up1[layer].astype(BF16),
         'w_down1': w_down1[layer].astype(BF16), 'g_mix': g_mix[layer], 'w_in_pad': w_in_pad,
         'conv_w': conv_w[layer], 'conv_b': conv_b[layer], 'dt_bias': dt_bias[layer], 'a_log': a_log[layer],
         'd_skip': d_skip[layer], 'g_ssm_out': g_ssm_out[layer], 'g_attn_out': g_attn_out[layer],
         'w_out_att': w_out[layer, :D_ATT].astype(BF16), 'w_out_ssm': w_out[layer, D_ATT:].astype(BF16),
         'g_ffn2': g_ffn2[layer], 'w_gate2': w_gate2[layer].astype(BF16), 'w_up2': w_up2[layer].astype(BF16),
         'w_down2': w_down2[layer].astype(BF16)}

    n_c = bp + bs
    c_rows = -(-n_c // 16) * 16
    c_all = jnp.pad(jnp.concatenate([c_prompt, c_sample], axis=0), ((0, c_rows - n_c), (0, 0)))
    mod = _mod_call(c_all, w_ada[layer], b_ada[layer])
    mods_p = [mod[:bp, i * D_MODEL:(i + 1) * D_MODEL].reshape(bp, 1, D_MODEL) for i in range(N_MOD)]
    mods_s = [jnp.repeat(mod[bp:n_c, i * D_MODEL:(i + 1) * D_MODEL], ls, axis=0) for i in range(N_MOD)]

    def attend_prompt(q, k, v):
        return _dilated_attn_prompt(q, k, v, rpb_table), k[:, -wbp:], v[:, -wbp:]

    def attend_sample(q, k, v):
        return _dilated_attn_sample(q, k, v, cache_k[layer], cache_v[layer], rpb_table), k, v

    conv0 = jnp.zeros((bp, CONV_W - 1, CONV_DIM), F32)
    h0 = jnp.zeros((bp, H_SSM, SSM_HEAD_DIM, D_STATE), F32)
    yp, kp, vp, hp, cp = _layer(x_prompt, mods_p, p, attend_prompt, conv0, h0,
                                tm=1024, per_token=False, g_final=g_final)
    ys, kn, vn, hs, cs = _layer(x_sample, mods_s, p, attend_sample, state_conv[layer], state_ssm[layer],
                                tm=bs * ls, per_token=True, g_final=g_final)
    return (yp, ys, kp[None], vp[None], kn[None], vn[None], hp[None], hs[None], cp[None], cs[None])
```

```python
import functools
import math

import jax
import jax.numpy as jnp
import numpy as np
from jax import lax
from jax.experimental import pallas as pl
from jax.experimental.pallas import tpu as pltpu

D_MODEL = 2048
HEAD_DIM = 64
D_ATT = 1024
D_SSM = 1024
H_ATT = 16
SSM_HEAD_DIM = 64
H_SSM = 16
SSM_GROUPS = 2
D_STATE = 128
CONV_W = 4
SSD_CHUNK = 128
CONV_DIM = D_SSM + 2 * SSM_GROUPS * D_STATE
D_FF = 5632
DILATED_PATTERNS = ((128, 1), (512, 4), (2048, 16))
WINDOW_MAX = 2048
ATT_BLOCK = 128
RPB_BUCKETS = 32
RPB_MAX_DIST = 2048
NORM_EPS = 1e-6
N_MOD = 9
PROJ_SIZES = (D_ATT, D_ATT, D_ATT, D_SSM, CONV_DIM, H_SSM)
D_IN = sum(PROJ_SIZES)
D_IN_PAD = 6144

VMEM_LIMIT_BYTES = 56 * 1024 * 1024

BF16 = jnp.bfloat16
F32 = jnp.float32


def _silu(v):
    return v * jax.nn.sigmoid(v)


def _norm_mod(x, g, shift, scale):
    y = x * lax.rsqrt(jnp.mean(x * x, axis=-1, keepdims=True) + NORM_EPS)
    return (y * g) * (1.0 + scale) + shift


def _mod_kernel(c_ref, w_ref, b_ref, o_ref, a_ref):
    @pl.when(pl.program_id(0) == 0)
    def _():
        a_ref[...] = _silu(c_ref[...]).astype(BF16)

    o_ref[...] = jnp.dot(a_ref[...], w_ref[...].astype(BF16),
                         preferred_element_type=F32) + b_ref[...]


def _mod_call(c_all, w_ada, b_ada):
    rows = c_all.shape[0]
    n = w_ada.shape[1]
    tn = 1024
    return pl.pallas_call(
        _mod_kernel,
        out_shape=jax.ShapeDtypeStruct((rows, n), F32),
        grid=(n // tn,),
        in_specs=[
            pl.BlockSpec((rows, D_MODEL), lambda j: (0, 0)),
            pl.BlockSpec((D_MODEL, tn), lambda j: (0, j)),
            pl.BlockSpec((1, tn), lambda j: (0, j)),
        ],
        out_specs=pl.BlockSpec((rows, tn), lambda j: (0, j)),
        scratch_shapes=[pltpu.VMEM((rows, D_MODEL), BF16)],
        compiler_params=pltpu.CompilerParams(
            dimension_semantics=("arbitrary",), vmem_limit_bytes=VMEM_LIMIT_BYTES),
        name="adaln_mod",
    )(c_all, w_ada, b_ada.reshape(1, n))


def _ffn_kernel(x_ref, g_ref, sh_ref, sc_ref, gt_ref, wg_ref, wu_ref, wd_ref, gf_ref,
                o_ref, h_ref, *, final_norm):
    j = pl.program_id(1)

    @pl.when(j == 0)
    def _():
        h_ref[...] = _norm_mod(x_ref[...], g_ref[...], sh_ref[...], sc_ref[...]).astype(BF16)
        o_ref[...] = jnp.zeros_like(o_ref)

    h = h_ref[...]
    a = jnp.dot(h, wg_ref[...], preferred_element_type=F32)
    b = jnp.dot(h, wu_ref[...], preferred_element_type=F32)
    s = (_silu(a) * b).astype(BF16)
    o_ref[...] += jnp.dot(s, wd_ref[...], preferred_element_type=F32)

    @pl.when(j == pl.num_programs(1) - 1)
    def _():
        y = x_ref[...] + 0.5 * gt_ref[...] * o_ref[...]
        if final_norm:
            y = y * lax.rsqrt(jnp.mean(y * y, axis=-1, keepdims=True) + NORM_EPS) * gf_ref[...]
        o_ref[...] = y


def _mod_spec(mod, tm, rows_per_mod):
    if mod.ndim == 3:
        tiles_per_row = rows_per_mod // tm
        return pl.BlockSpec((None, 1, D_MODEL), lambda i, j: (i // tiles_per_row, 0, 0))
    return pl.BlockSpec((tm, D_MODEL), lambda i, j: (i, 0))


def _ffn_call(x, g, sh, sc, gt, wg, wu, wd, g_final, *, tm, rows_per_mod, final_norm):
    m = x.shape[0]
    tf = 512
    row = lambda i, j: (i, 0)
    const = lambda i, j: (0, 0)
    return pl.pallas_call(
        functools.partial(_ffn_kernel, final_norm=final_norm),
        out_shape=jax.ShapeDtypeStruct((m, D_MODEL), F32),
        grid=(m // tm, D_FF // tf),
        in_specs=[
            pl.BlockSpec((tm, D_MODEL), row, pipeline_mode=pl.Buffered(1)),
            pl.BlockSpec((1, D_MODEL), const),
            _mod_spec(sh, tm, rows_per_mod),
            _mod_spec(sc, tm, rows_per_mod),
            _mod_spec(gt, tm, rows_per_mod),
            pl.BlockSpec((D_MODEL, tf), lambda i, j: (0, j)),
            pl.BlockSpec((D_MODEL, tf), lambda i, j: (0, j)),
            pl.BlockSpec((tf, D_MODEL), lambda i, j: (j, 0)),
            pl.BlockSpec((1, D_MODEL), const),
        ],
        out_specs=pl.BlockSpec((tm, D_MODEL), row),
        scratch_shapes=[pltpu.VMEM((tm, D_MODEL), BF16)],
        compiler_params=pltpu.CompilerParams(
            dimension_semantics=("parallel", "arbitrary"), vmem_limit_bytes=VMEM_LIMIT_BYTES),
        name="ffn_final" if final_norm else "ffn",
    )(x, g.reshape(1, D_MODEL), sh, sc, gt, wg, wu, wd, g_final.reshape(1, D_MODEL))


def _inproj_kernel(x_ref, g_ref, sh_ref, sc_ref, w_ref, o_ref, h_ref):
    @pl.when(pl.program_id(1) == 0)
    def _():
        h_ref[...] = _norm_mod(x_ref[...], g_ref[...], sh_ref[...], sc_ref[...]).astype(BF16)

    o_ref[...] = jnp.dot(h_ref[...], w_ref[...], preferred_element_type=F32)


def _inproj_call(x, g, sh, sc, w_pad, *, tm, rows_per_mod):
    m = x.shape[0]
    tn = 1024
    return pl.pallas_call(
        _inproj_kernel,
        out_shape=jax.ShapeDtypeStruct((m, D_IN_PAD), F32),
        grid=(m // tm, D_IN_PAD // tn),
        in_specs=[
            pl.BlockSpec((tm, D_MODEL), lambda i, j: (i, 0)),
            pl.BlockSpec((1, D_MODEL), lambda i, j: (0, 0)),
            _mod_spec(sh, tm, rows_per_mod),
            _mod_spec(sc, tm, rows_per_mod),
            pl.BlockSpec((D_MODEL, tn), lambda i, j: (0, j)),
        ],
        out_specs=pl.BlockSpec((tm, tn), lambda i, j: (i, j)),
        scratch_shapes=[pltpu.VMEM((tm, D_MODEL), BF16)],
        compiler_params=pltpu.CompilerParams(
            dimension_semantics=("parallel", "arbitrary"), vmem_limit_bytes=VMEM_LIMIT_BYTES),
        name="in_proj",
    )(x, g.reshape(1, D_MODEL), sh, sc, w_pad)


def _outproj_kernel(*refs, n_pat):
    x_ref = refs[0]
    if n_pat:
        o_refs = refs[1:1 + n_pat]
        l_refs = refs[1 + n_pat:1 + 2 * n_pat]
        ga_ref, ssd_ref, gt_ref, wa_ref, ws_ref, out_ref = refs[1 + 2 * n_pat:]
        lses = [r[...] for r in l_refs]
        mx = functools.reduce(jnp.maximum, lses)
        ws = [jnp.exp(v - mx) for v in lses]
        wsum = functools.reduce(lambda a, b: a + b, ws)
        expand = _head_lane_mask(128).astype(BF16)
        att = None
        for w, o_ref in zip(ws, o_refs):
            wn = w / wsum
            w_hi = wn.astype(BF16)
            w_lo = (wn - w_hi.astype(F32)).astype(BF16)
            wx = (jnp.dot(w_hi, expand, preferred_element_type=F32)
                  + jnp.dot(w_lo, expand, preferred_element_type=F32))
            term = wx * o_ref[...]
            att = term if att is None else att + term
        att = att * lax.rsqrt(jnp.mean(att * att, axis=-1, keepdims=True) + NORM_EPS) * ga_ref[...]
    else:
        att_ref, ssd_ref, gt_ref, wa_ref, ws_ref, out_ref = refs[1:]
        att = att_ref[...]
    mix = jnp.dot(att.astype(BF16), wa_ref[...], preferred_element_type=F32)
    mix += jnp.dot(ssd_ref[...].astype(BF16), ws_ref[...], preferred_element_type=F32)
    out_ref[...] = x_ref[...] + gt_ref[...] * mix


def _outproj_call(x, att_parts, ssd, gt, w_att, w_ssd, g_attn, *, tm, rows_per_mod):
    m = x.shape[0]
    n_pat = len(att_parts) // 2
    if gt.ndim == 3:
        tiles_per_row = rows_per_mod // tm
        gt_spec = pl.BlockSpec((None, 1, D_MODEL), lambda i: (i // tiles_per_row, 0, 0))
    else:
        gt_spec = pl.BlockSpec((tm, D_MODEL), lambda i: (i, 0))
    row = lambda width: pl.BlockSpec((tm, width), lambda i: (i, 0))
    if n_pat:
        att_specs = [row(D_ATT)] * n_pat + [row(128)] * n_pat + [pl.BlockSpec((1, D_ATT), lambda i: (0, 0))]
        att_args = list(att_parts) + [g_attn.reshape(1, D_ATT)]
    else:
        att_specs = [row(D_ATT)]
        att_args = list(att_parts)
    weight = lambda k: pl.BlockSpec((k, D_MODEL), lambda i: (0, 0), pipeline_mode=pl.Buffered(1))
    return pl.pallas_call(
        functools.partial(_outproj_kernel, n_pat=n_pat),
        out_shape=jax.ShapeDtypeStruct((m, D_MODEL), F32),
        grid=(m // tm,),
        in_specs=[row(D_MODEL)] + att_specs + [row(D_SSM), gt_spec, weight(D_ATT), weight(D_SSM)],
        out_specs=row(D_MODEL),
        compiler_params=pltpu.CompilerParams(
            dimension_semantics=("parallel",), vmem_limit_bytes=VMEM_LIMIT_BYTES),
        name="out_proj_merge" if n_pat else "out_proj",
    )(x, *att_args, ssd, gt, w_att, w_ssd)


def _rel_bucket(dist):
    max_exact = RPB_BUCKETS // 2
    df = jnp.maximum(dist, 1).astype(jnp.float32)
    large = max_exact + (jnp.log(df / max_exact) / math.log(RPB_MAX_DIST / max_exact)
                         * (RPB_BUCKETS - max_exact)).astype(jnp.int32)
    large = jnp.minimum(large, RPB_BUCKETS - 1)
    return jnp.where(dist < max_exact, dist, large)


def _head_lane_mask(rows=H_ATT):
    r = lax.broadcasted_iota(jnp.int32, (rows, D_ATT), 0)
    c = lax.broadcasted_iota(jnp.int32, (rows, D_ATT), 1)
    return (c // HEAD_DIM == r).astype(F32)


NT_DIMS = (((1,), (1,)), ((), ()))


def _attn_prompt_kernel(q_ref, kp_ref, kc_ref, vp_ref, vc_ref, bias_ref, o_ref, lse_ref):
    nb = pl.program_id(2)
    blk = ATT_BLOCK
    col = lax.broadcasted_iota(jnp.int32, (blk, 2 * blk), 1)
    has_prev = (col >= blk) | (nb > 0)
    lane = lax.broadcasted_iota(jnp.int32, (blk, 2 * HEAD_DIM), 1)
    lo = lane < HEAD_DIM
    lse_lane = lax.broadcasted_iota(jnp.int32, (blk, 128), 1)
    lse_tile = jnp.zeros((blk, 128), F32)
    scale = HEAD_DIM ** -0.5
    for hp in range(H_ATT // 2):
        sl = slice(hp * 2 * HEAD_DIM, (hp + 1) * 2 * HEAD_DIM)
        q2 = q_ref[:, sl] * scale
        k2 = jnp.concatenate([kp_ref[:, sl], kc_ref[:, sl]], axis=0).astype(BF16)
        v2 = jnp.concatenate([vp_ref[:, sl], vc_ref[:, sl]], axis=0).astype(BF16)
        outs = []
        for half in range(2):
            h = 2 * hp + half
            qh = jnp.where(lo if half == 0 else ~lo, q2, 0.0).astype(BF16)
            s = lax.dot_general(qh, k2, NT_DIMS, preferred_element_type=F32) + bias_ref[h]
            s = jnp.where(has_prev, s, -jnp.inf)
            m = jnp.max(s, axis=-1, keepdims=True)
            p = jnp.exp(s - m)
            l = jnp.sum(p, axis=-1, keepdims=True)
            pv = jnp.dot(p.astype(BF16), v2, preferred_element_type=F32)
            outs.append(pv / l)
            lse_tile = jnp.where(lse_lane == h, m + jnp.log(l), lse_tile)
        o_ref[:, sl] = jnp.where(lo, outs[0], outs[1])
    lse_ref[...] = lse_tile


def _attn_prompt_call(proj, bias, dil):
    b, s_len, _ = proj.shape
    blk = ATT_BLOCK
    n_sub = s_len // dil
    nbk = n_sub // blk
    tiles = D_IN_PAD // D_ATT
    pv = proj.reshape(b, n_sub, dil * D_IN_PAD)
    cur = lambda t: (lambda bi, r, nb: (bi, nb, r * tiles + t))
    prev = lambda t: (lambda bi, r, nb: (bi, jnp.maximum(nb - 1, 0), r * tiles + t))
    spec = lambda f: pl.BlockSpec((None, blk, D_ATT), f)
    o, lse = pl.pallas_call(
        _attn_prompt_kernel,
        out_shape=(jax.ShapeDtypeStruct((b, n_sub, dil * D_ATT), F32),
                   jax.ShapeDtypeStruct((b, n_sub, dil * 128), F32)),
        grid=(b, dil, nbk),
        in_specs=[spec(cur(0)), spec(prev(1)), spec(cur(1)), spec(prev(2)), spec(cur(2)),
                  pl.BlockSpec((H_ATT, blk, 2 * blk), lambda bi, r, nb: (0, 0, 0))],
        out_specs=(pl.BlockSpec((None, blk, D_ATT), lambda bi, r, nb: (bi, nb, r)),
                   pl.BlockSpec((None, blk, 128), lambda bi, r, nb: (bi, nb, r))),
        compiler_params=pltpu.CompilerParams(
            dimension_semantics=("parallel", "parallel", "arbitrary"),
            vmem_limit_bytes=VMEM_LIMIT_BYTES),
        name=f"attn_prompt_d{dil}",
    )(pv, pv, pv, pv, pv, bias)
    return o.reshape(b * s_len, D_ATT), lse.reshape(b * s_len, 128)


def _prompt_bias(rpb, window, dil):
    span = window // dil
    blk = ATT_BLOCK
    rel = jnp.arange(blk)[:, None] + blk - jnp.arange(2 * blk)[None, :]
    in_band = (rel >= 0) & (rel <= span)
    bias = rpb[_rel_bucket(jnp.clip(rel, 0, span) * dil)].transpose(2, 0, 1).astype(F32)
    return jnp.where(in_band[None], bias, -jnp.inf)


def _attn_sample_kernel(*refs):
    q_ref, kn_ref, vn_ref = refs[0:3]
    kc_refs = refs[3:12]
    vc_refs = refs[12:21]
    bias_c_ref, bias_n_ref, g_ref, o_ref = refs[21:25]
    n_q = q_ref.shape[0]
    et = _head_lane_mask()
    scale = HEAD_DIM ** -0.5
    pad = jnp.zeros((8 - n_q, D_ATT), F32)
    knb = jnp.concatenate([kn_ref[...], pad], axis=0).astype(BF16)
    vnb = jnp.concatenate([vn_ref[...], pad], axis=0).astype(BF16)
    k1 = kc_refs[0][...].astype(BF16)
    v1 = vc_refs[0][...].astype(BF16)
    rows = []
    for n in range(n_q):
        qbd = ((q_ref[n:n + 1, :] * scale) * et).astype(BF16)
        s_new = lax.dot_general(qbd, knb, NT_DIMS, preferred_element_type=F32)
        outs, lses = [], []
        for p in range(len(DILATED_PATTERNS)):
            if p == 0:
                kb, vb = k1, v1
            else:
                kb = kc_refs[1 + (p - 1) * n_q + n][...].astype(BF16)
                vb = vc_refs[1 + (p - 1) * n_q + n][...].astype(BF16)
            s_c = lax.dot_general(qbd, kb, NT_DIMS, preferred_element_type=F32) + bias_c_ref[p, n]
            s_n = s_new + bias_n_ref[p, n]
            m = jnp.maximum(jnp.max(s_c, axis=-1, keepdims=True), jnp.max(s_n, axis=-1, keepdims=True))
            p_c = jnp.exp(s_c - m)
            p_n = jnp.exp(s_n - m)
            l = jnp.sum(p_c, axis=-1, keepdims=True) + jnp.sum(p_n, axis=-1, keepdims=True)
            acc = jnp.dot(p_c.astype(BF16), vb, preferred_element_type=F32)
            acc += jnp.dot(p_n.astype(BF16), vnb, preferred_element_type=F32)
            outs.append(acc / l)
            lses.append(m + jnp.log(l))
        mx = jnp.maximum(jnp.maximum(lses[0], lses[1]), lses[2])
        ws = [jnp.exp(v - mx) for v in lses]
        wsum = ws[0] + ws[1] + ws[2]
        comb = (ws[0] / wsum) * outs[0] + (ws[1] / wsum) * outs[1] + (ws[2] / wsum) * outs[2]
        rows.append(jnp.sum(comb * et, axis=0, keepdims=True))
    att = jnp.concatenate(rows, axis=0)
    att = att * lax.rsqrt(jnp.mean(att * att, axis=-1, keepdims=True) + NORM_EPS) * g_ref[...]
    o_ref[...] = att


def _attn_sample_call(proj, cache_k, cache_v, bias_c, bias_n, g_attn):
    b, n_q, _ = proj.shape
    wb = cache_k.shape[1]
    blk = ATT_BLOCK
    assert wb == WINDOW_MAX and n_q == 4
    spec = lambda f: pl.BlockSpec((None, blk, D_ATT), f)

    def cache_views(c):
        flat = c.reshape(b, wb, D_ATT)
        views = [flat]
        specs = [spec(lambda bi: (bi, wb // blk - 1, 0))]
        for _, dil in DILATED_PATTERNS[1:]:
            rows = wb // dil
            v = c.reshape(b, rows, dil * D_ATT)
            for n in range(n_q):
                views.append(v)
                specs.append(spec(functools.partial(lambda bi, n, last: (bi, last, n), n=n, last=rows // blk - 1)))
        return views, specs

    kviews, kspecs = cache_views(cache_k)
    vviews, vspecs = cache_views(cache_v)
    qspec = lambda t: pl.BlockSpec((None, n_q, D_ATT), lambda bi: (bi, 0, t))
    n_pat = len(DILATED_PATTERNS)
    out = pl.pallas_call(
        _attn_sample_kernel,
        out_shape=jax.ShapeDtypeStruct((b, n_q, D_ATT), F32),
        grid=(b,),
        in_specs=[qspec(0), qspec(1), qspec(2)] + kspecs + vspecs + [
            pl.BlockSpec((n_pat, n_q, H_ATT, blk), lambda bi: (0, 0, 0, 0)),
            pl.BlockSpec((n_pat, n_q, H_ATT, 8), lambda bi: (0, 0, 0, 0)),
            pl.BlockSpec((1, D_ATT), lambda bi: (0, 0)),
        ],
        out_specs=pl.BlockSpec((None, n_q, D_ATT), lambda bi: (bi, 0, 0)),
        compiler_params=pltpu.CompilerParams(
            dimension_semantics=("parallel",), vmem_limit_bytes=VMEM_LIMIT_BYTES),
        name="attn_sample",
    )(proj, proj, proj, *kviews, *vviews, bias_c, bias_n, g_attn.reshape(1, D_ATT))
    return out.reshape(b * n_q, D_ATT)


def _sample_bias(rpb, n_q, wb):
    blk = ATT_BLOCK
    i = jnp.arange(blk)
    n = jnp.arange(n_q)
    m = jnp.arange(8)
    neg = -jnp.inf
    bc, bn = [], []
    for p, (window, dil) in enumerate(DILATED_PATTERNS):
        span = window // dil
        table = rpb[_rel_bucket(jnp.arange(span + 1) * dil)].T.astype(F32)
        if p == 0:
            j = span + n[:, None] - i[None, :]
        else:
            j = jnp.broadcast_to(span - i[None, :], (n_q, blk))
        ok = (j >= 0) & (j <= span)
        bc.append(jnp.where(ok[:, None, :], table[:, jnp.clip(j, 0, span)].transpose(1, 0, 2), neg))
        d = n[:, None] - m[None, :]
        okn = (m[None, :] < n_q) & (d >= 0) & (d % dil == 0) & (d // dil <= span)
        bn.append(jnp.where(okn[:, None, :], table[:, jnp.clip(d // dil, 0, span)].transpose(1, 0, 2), neg))
    return jnp.stack(bc), jnp.stack(bn)


def _rmsnorm(x, g):
    xf = x.astype(jnp.float32)
    y = xf * lax.rsqrt(jnp.mean(xf * xf, axis=-1, keepdims=True) + NORM_EPS)
    return (y * g.astype(jnp.float32)).astype(x.dtype)


def _rel_bucket(dist):
    max_exact = RPB_BUCKETS // 2
    df = jnp.maximum(dist, 1).astype(jnp.float32)
    large = max_exact + (jnp.log(df / max_exact) / math.log(RPB_MAX_DIST / max_exact)
                         * (RPB_BUCKETS - max_exact)).astype(jnp.int32)
    large = jnp.minimum(large, RPB_BUCKETS - 1)
    return jnp.where(dist < max_exact, dist, large)


def _combine_dilations(outs, lses):
    w = jax.nn.softmax(jnp.stack(lses, 0), axis=0)
    return jnp.sum(w[..., None] * jnp.stack(outs, 0), axis=0)


def _dilated_banded_prompt(q, k, v, rpb, window, dil):
    b, s_len, h, dh = q.shape
    span = window // dil
    blk = ATT_BLOCK
    unit = dil * blk
    s_pad = -(-s_len // unit) * unit
    n_sub = s_pad // dil
    nb = n_sub // blk

    def to_blocks(t):
        t = jnp.pad(t, ((0, 0), (0, s_pad - s_len), (0, 0), (0, 0)))
        t = t.reshape(b, n_sub, dil, h, dh).transpose(0, 2, 1, 3, 4)
        return t.reshape(b, dil, nb, blk, h, dh)

    def with_prev(t):
        prev = jnp.pad(t[:, :, :-1], ((0, 0), (0, 0), (1, 0), (0, 0), (0, 0), (0, 0)))
        return jnp.concatenate([prev, t], axis=3)

    qb = to_blocks(q)
    kk = with_prev(to_blocks(k))
    vv = with_prev(to_blocks(v))
    qi = jnp.arange(blk)[:, None]
    ki = jnp.arange(2 * blk)[None, :]
    rel = qi + blk - ki
    in_band = (rel >= 0) & (rel <= span)
    bias = rpb[_rel_bucket(jnp.clip(rel, 0, span) * dil)].transpose(2, 0, 1).astype(jnp.float32)
    has_prev = (jnp.arange(nb)[:, None] > 0) | (jnp.arange(2 * blk)[None, :] >= blk)
    mask = in_band[None] & has_prev[:, None, :]
    s = jnp.einsum('brnqhd,brnkhd->brnhqk', qb, kk).astype(jnp.float32) * (HEAD_DIM ** -0.5) + bias
    s = jnp.where(mask[None, None, :, None], s, -jnp.inf)
    m = jnp.max(s, axis=-1, keepdims=True)
    p = jnp.exp(s - m)
    l = jnp.sum(p, axis=-1)
    o = jnp.einsum('brnhqk,brnkhd->brnqhd', p, vv.astype(jnp.float32)) / jnp.swapaxes(l, -1, -2)[..., None]
    lse = m[..., 0] + jnp.log(l)
    o = o.reshape(b, dil, n_sub, h, dh).transpose(0, 2, 1, 3, 4).reshape(b, s_pad, h, dh)[:, :s_len]
    lse = jnp.swapaxes(lse, -1, -2).reshape(b, dil, n_sub, h).transpose(0, 2, 1, 3).reshape(b, s_pad, h)[:, :s_len]
    return o, lse


def _dilated_attn_prompt(q, k, v, rpb):
    outs, lses = [], []
    for window, dil in DILATED_PATTERNS:
        o, lse = _dilated_banded_prompt(q, k, v, rpb, window, dil)
        outs.append(o)
        lses.append(lse)
    return _combine_dilations(outs, lses)


def _dilated_attn_sample(q, k_new, v_new, cache_k, cache_v, rpb):
    kc = jnp.concatenate([cache_k.astype(k_new.dtype), k_new], axis=1)
    vc = jnp.concatenate([cache_v.astype(v_new.dtype), v_new], axis=1)
    wb = cache_k.shape[1]
    n = q.shape[1]
    outs, lses = [], []
    for window, dil in DILATED_PATTERNS:
        span = window // dil
        j = jnp.arange(span + 1)
        idx = wb + jnp.arange(n)[:, None] - j[None, :] * dil
        valid = idx >= 0
        idx = jnp.maximum(idx, 0)
        kg = kc[:, idx]
        vg = vc[:, idx]
        bias = rpb[_rel_bucket(j * dil)].T.astype(jnp.float32)
        s = jnp.einsum('bqhd,bqjhd->bhqj', q, kg).astype(jnp.float32) * (HEAD_DIM ** -0.5) + bias[:, None, :]
        s = jnp.where(valid[None, None], s, -jnp.inf)
        m = jnp.max(s, axis=-1, keepdims=True)
        p = jnp.exp(s - m)
        l = jnp.sum(p, axis=-1)
        o = jnp.einsum('bhqj,bqjhd->bqhd', p, vg.astype(jnp.float32)) / jnp.swapaxes(l, 1, 2)[..., None]
        outs.append(o)
        lses.append(jnp.swapaxes(m[..., 0] + jnp.log(l), 1, 2))
    return _combine_dilations(outs, lses)


def _ssd_scan(xs, dt, a, bm, cm, h0):
    bt, L, g, hg, pdim = xs.shape
    n = bm.shape[-1]
    qn = SSD_CHUNK if L % SSD_CHUNK == 0 else L
    nc = L // qn
    xdt = (xs.astype(jnp.float32) * dt[..., None]).reshape(bt, nc, qn, g, hg, pdim)
    a_cum = jnp.cumsum((dt * a).reshape(bt, nc, qn, g, hg), axis=2)
    bc = bm.reshape(bt, nc, qn, g, n)
    cc = cm.reshape(bt, nc, qn, g, n)
    seg = a_cum[:, :, :, None] - a_cum[:, :, None, :]
    causal = jnp.tril(jnp.ones((qn, qn), dtype=bool))[:, :, None, None]
    decay = jnp.exp(jnp.where(causal, seg, -jnp.inf))
    y_diag = jnp.einsum('bctgn,bcsgn,bctsgh,bcsghp->bctghp', cc, bc, decay, xdt)
    decay_end = jnp.exp(a_cum[:, :, -1:] - a_cum)
    chunk_states = jnp.einsum('bcsgn,bcsgh,bcsghp->bcghpn', bc, decay_end, xdt)
    chunk_decay = jnp.exp(a_cum[:, :, -1])

    def step(h, inp):
        dec, st = inp
        return dec[..., None, None] * h + st, h

    h_last, h_in = lax.scan(step, h0.astype(jnp.float32),
                            (jnp.swapaxes(chunk_decay, 0, 1), jnp.swapaxes(chunk_states, 0, 1)))
    h_in = jnp.swapaxes(h_in, 0, 1)
    y_off = jnp.einsum('bctgn,bctgh,bcghpn->bctghp', cc, jnp.exp(a_cum), h_in)
    return (y_diag + y_off).reshape(bt, L, g, hg, pdim), h_last


def _ssd_branch(z, xbc, dt_raw, conv_buf, h0, p):
    bt, L, _ = xbc.shape
    hg = H_SSM // SSM_GROUPS
    xpad = jnp.concatenate([conv_buf.astype(xbc.dtype), xbc], axis=1)
    conv = p['conv_b'] + xpad[:, 0:L] * p['conv_w'][0]
    for i in range(1, CONV_W):
        conv = conv + xpad[:, i:i + L] * p['conv_w'][i]
    act = jax.nn.silu(conv)
    xs, bm, cm = jnp.split(act, [D_SSM, D_SSM + SSM_GROUPS * D_STATE], axis=-1)
    xs = xs.reshape(bt, L, SSM_GROUPS, hg, SSM_HEAD_DIM)
    bm = bm.reshape(bt, L, SSM_GROUPS, D_STATE)
    cm = cm.reshape(bt, L, SSM_GROUPS, D_STATE)
    dt = jax.nn.softplus(dt_raw.astype(jnp.float32) + p['dt_bias'].astype(jnp.float32)).reshape(bt, L, SSM_GROUPS, hg)
    a = -jnp.exp(p['a_log'].astype(jnp.float32)).reshape(SSM_GROUPS, hg)
    y, h_last = _ssd_scan(xs, dt, a, bm, cm, h0.reshape(bt, SSM_GROUPS, hg, SSM_HEAD_DIM, D_STATE))
    y = y + p['d_skip'].astype(jnp.float32).reshape(SSM_GROUPS, hg)[:, :, None] * xs
    gated = y.reshape(bt, L, SSM_GROUPS, hg * SSM_HEAD_DIM) * jax.nn.silu(z.astype(jnp.float32)).reshape(bt, L, SSM_GROUPS, hg * SSM_HEAD_DIM)
    yn = gated * lax.rsqrt(jnp.mean(gated * gated, axis=-1, keepdims=True) + NORM_EPS)
    yn = yn.reshape(bt, L, D_SSM) * p['g_ssm_out'].astype(jnp.float32)
    return yn.astype(z.dtype), h_last.reshape(bt, H_SSM, SSM_HEAD_DIM, D_STATE), xpad[:, L:]


def _layer(x, mods, p, attend, conv_buf, h0, *, tm, g_final):
    b, L, _ = x.shape
    m = b * L
    sh1, sc1, g1, sh2, sc2, g2, sh3, sc3, g3 = mods
    x2 = x.reshape(m, D_MODEL)
    kw = dict(tm=tm, rows_per_mod=L)
    x2 = _ffn_call(x2, p['g_ffn1'], sh1, sc1, g1, p['w_gate1'], p['w_up1'], p['w_down1'],
                   g_final, final_norm=False, **kw)
    proj = _inproj_call(x2, p['g_mix'], sh2, sc2, p['w_in_pad'], tm=min(tm, 512), rows_per_mod=L)
    proj = proj.reshape(b, L, D_IN_PAD)
    k = proj[..., D_ATT:2 * D_ATT].reshape(b, L, H_ATT, HEAD_DIM)
    v = proj[..., 2 * D_ATT:3 * D_ATT].reshape(b, L, H_ATT, HEAD_DIM)
    z = proj[..., 3 * D_ATT:3 * D_ATT + D_SSM]
    o = 3 * D_ATT + D_SSM
    xbc = proj[..., o:o + CONV_DIM]
    dt_raw = proj[..., o + CONV_DIM:o + CONV_DIM + H_SSM]
    att_parts = attend(proj)
    ssd_o, h_last, conv_new = _ssd_branch(z, xbc, dt_raw, conv_buf, h0, p)
    x2 = _outproj_call(x2, att_parts, ssd_o.reshape(m, D_SSM), g2, p['w_out_att'], p['w_out_ssm'],
                       p['g_attn_out'], tm=min(tm, 256), rows_per_mod=L)
    x2 = _ffn_call(x2, p['g_ffn2'], sh3, sc3, g3, p['w_gate2'], p['w_up2'], p['w_down2'],
                   g_final, final_norm=True, **kw)
    return x2.reshape(b, L, D_MODEL), k, v, h_last, conv_new


def kernel(x_prompt, x_sample, cache_k, cache_v, state_ssm, state_conv, c_prompt, c_sample,
           rpb_table, w_ada, b_ada, g_ffn1, w_gate1, w_up1, w_down1, g_mix, w_in, conv_w, conv_b,
           dt_bias, a_log, d_skip, g_ssm_out, g_attn_out, w_out, g_ffn2, w_gate2, w_up2, w_down2, g_final):
    bp, sp, _ = x_prompt.shape
    bs, ls, _ = x_sample.shape
    wbp = min(WINDOW_MAX, sp)
    layer = 0
    w_in_pad = jnp.pad(w_in[layer].astype(BF16), ((0, 0), (0, D_IN_PAD - D_IN)))
    p = {'g_ffn1': g_ffn1[layer], 'w_gate1': w_gate1[layer].astype(BF16), 'w_up1': w_up1[layer].astype(BF16),
         'w_down1': w_down1[layer].astype(BF16), 'g_mix': g_mix[layer], 'w_in_pad': w_in_pad,
         'conv_w': conv_w[layer], 'conv_b': conv_b[layer], 'dt_bias': dt_bias[layer], 'a_log': a_log[layer],
         'd_skip': d_skip[layer], 'g_ssm_out': g_ssm_out[layer], 'g_attn_out': g_attn_out[layer],
         'w_out_att': w_out[layer, :D_ATT].astype(BF16), 'w_out_ssm': w_out[layer, D_ATT:].astype(BF16),
         'g_ffn2': g_ffn2[layer], 'w_gate2': w_gate2[layer].astype(BF16), 'w_up2': w_up2[layer].astype(BF16),
         'w_down2': w_down2[layer].astype(BF16)}

    n_c = bp + bs
    c_rows = -(-n_c // 16) * 16
    c_all = jnp.pad(jnp.concatenate([c_prompt, c_sample], axis=0), ((0, c_rows - n_c), (0, 0)))
    mod = _mod_call(c_all, w_ada[layer], b_ada[layer])
    mods_p = [mod[:bp, i * D_MODEL:(i + 1) * D_MODEL].reshape(bp, 1, D_MODEL) for i in range(N_MOD)]
    mods_s = [jnp.repeat(mod[bp:n_c, i * D_MODEL:(i + 1) * D_MODEL], ls, axis=0) for i in range(N_MOD)]

    def attend_prompt(proj):
        outs, lses = [], []
        for window, dil in DILATED_PATTERNS:
            o, lse = _attn_prompt_call(proj, _prompt_bias(rpb_table, window, dil), dil)
            outs.append(o)
            lses.append(lse)
        return outs + lses

    bias_c, bias_n = _sample_bias(rpb_table, ls, cache_k.shape[2])

    def attend_sample(proj):
        return [_attn_sample_call(proj, cache_k[layer], cache_v[layer], bias_c, bias_n, p['g_attn_out'])]

    conv0 = jnp.zeros((bp, CONV_W - 1, CONV_DIM), F32)
    h0 = jnp.zeros((bp, H_SSM, SSM_HEAD_DIM, D_STATE), F32)
    yp, kp, vp, hp, cp = _layer(x_prompt, mods_p, p, attend_prompt, conv0, h0, tm=1024, g_final=g_final)
    ys, kn, vn, hs, cs = _layer(x_sample, mods_s, p, attend_sample, state_conv[layer], state_ssm[layer],
                                tm=bs * ls, g_final=g_final)
    return (yp, ys, kp[None, :, -wbp:], vp[None, :, -wbp:], kn[None], vn[None],
            hp[None], hs[None], cp[None], cs[None])
```

```python
import functools
import math

import jax
import jax.numpy as jnp
from jax import lax
from jax.experimental import pallas as pl
from jax.experimental.pallas import tpu as pltpu

D_MODEL = 2048
HEAD_DIM = 64
D_ATT = 1024
D_SSM = 1024
H_ATT = 16
SSM_HEAD_DIM = 64
H_SSM = 16
SSM_GROUPS = 2
D_STATE = 128
CONV_W = 4
SSD_CHUNK = 128
CONV_DIM = D_SSM + 2 * SSM_GROUPS * D_STATE
D_FF = 5632
DILATED_PATTERNS = ((128, 1), (512, 4), (2048, 16))
WINDOW_MAX = 2048
ATT_BLOCK = 128
RPB_BUCKETS = 32
RPB_MAX_DIST = 2048
NORM_EPS = 1e-6
N_MOD = 9
PROJ_SIZES = (D_ATT, D_ATT, D_ATT, D_SSM, CONV_DIM, H_SSM)
D_IN = sum(PROJ_SIZES)
D_IN_PAD = 6144

VMEM_LIMIT_BYTES = 56 * 1024 * 1024

BF16 = jnp.bfloat16
F32 = jnp.float32


def _silu(v):
    return v * jax.nn.sigmoid(v)


def _norm_mod(x, g, shift, scale):
    y = x * lax.rsqrt(jnp.mean(x * x, axis=-1, keepdims=True) + NORM_EPS)
    return (y * g) * (1.0 + scale) + shift


def _mod_kernel(c_ref, w_ref, b_ref, o_ref, a_ref):
    @pl.when(pl.program_id(0) == 0)
    def _():
        a_ref[...] = _silu(c_ref[...]).astype(BF16)

    o_ref[...] = jnp.dot(a_ref[...], w_ref[...].astype(BF16),
                         preferred_element_type=F32) + b_ref[...]


def _mod_call(c_all, w_ada, b_ada):
    rows = c_all.shape[0]
    n = w_ada.shape[1]
    tn = 1024
    return pl.pallas_call(
        _mod_kernel,
        out_shape=jax.ShapeDtypeStruct((rows, n), F32),
        grid=(n // tn,),
        in_specs=[
            pl.BlockSpec((rows, D_MODEL), lambda j: (0, 0)),
            pl.BlockSpec((D_MODEL, tn), lambda j: (0, j)),
            pl.BlockSpec((1, tn), lambda j: (0, j)),
        ],
        out_specs=pl.BlockSpec((rows, tn), lambda j: (0, j)),
        scratch_shapes=[pltpu.VMEM((rows, D_MODEL), BF16)],
        compiler_params=pltpu.CompilerParams(
            dimension_semantics=("arbitrary",), vmem_limit_bytes=VMEM_LIMIT_BYTES),
        name="adaln_mod",
    )(c_all, w_ada, b_ada.reshape(1, n))


def _ffn_kernel(x_ref, g_ref, sh_ref, sc_ref, gt_ref, wg_ref, wu_ref, wd_ref, gf_ref,
                o_ref, h_ref, *, final_norm):
    j = pl.program_id(1)

    @pl.when(j == 0)
    def _():
        h_ref[...] = _norm_mod(x_ref[...], g_ref[...], sh_ref[...], sc_ref[...]).astype(BF16)
        o_ref[...] = jnp.zeros_like(o_ref)

    h = h_ref[...]
    a = jnp.dot(h, wg_ref[...], preferred_element_type=F32)
    b = jnp.dot(h, wu_ref[...], preferred_element_type=F32)
    s = (_silu(a) * b).astype(BF16)
    o_ref[...] += jnp.dot(s, wd_ref[...], preferred_element_type=F32)

    @pl.when(j == pl.num_programs(1) - 1)
    def _():
        y = x_ref[...] + 0.5 * gt_ref[...] * o_ref[...]
        if final_norm:
            y = y * lax.rsqrt(jnp.mean(y * y, axis=-1, keepdims=True) + NORM_EPS) * gf_ref[...]
        o_ref[...] = y


def _mod_spec(mod, tm, rows_per_mod):
    if mod.ndim == 3:
        tiles_per_row = rows_per_mod // tm
        return pl.BlockSpec((None, 1, D_MODEL), lambda i, j: (i // tiles_per_row, 0, 0))
    return pl.BlockSpec((tm, D_MODEL), lambda i, j: (i, 0))


def _ffn_call(x, g, sh, sc, gt, wg, wu, wd, g_final, *, tm, rows_per_mod, final_norm):
    m = x.shape[0]
    tf = 512
    row = lambda i, j: (i, 0)
    const = lambda i, j: (0, 0)
    return pl.pallas_call(
        functools.partial(_ffn_kernel, final_norm=final_norm),
        out_shape=jax.ShapeDtypeStruct((m, D_MODEL), F32),
        grid=(m // tm, D_FF // tf),
        in_specs=[
            pl.BlockSpec((tm, D_MODEL), row, pipeline_mode=pl.Buffered(1)),
            pl.BlockSpec((1, D_MODEL), const),
            _mod_spec(sh, tm, rows_per_mod),
            _mod_spec(sc, tm, rows_per_mod),
            _mod_spec(gt, tm, rows_per_mod),
            pl.BlockSpec((D_MODEL, tf), lambda i, j: (0, j)),
            pl.BlockSpec((D_MODEL, tf), lambda i, j: (0, j)),
            pl.BlockSpec((tf, D_MODEL), lambda i, j: (j, 0)),
            pl.BlockSpec((1, D_MODEL), const),
        ],
        out_specs=pl.BlockSpec((tm, D_MODEL), row),
        scratch_shapes=[pltpu.VMEM((tm, D_MODEL), BF16)],
        compiler_params=pltpu.CompilerParams(
            dimension_semantics=("parallel", "arbitrary"), vmem_limit_bytes=VMEM_LIMIT_BYTES),
        name="ffn_final" if final_norm else "ffn",
    )(x, g.reshape(1, D_MODEL), sh, sc, gt, wg, wu, wd, g_final.reshape(1, D_MODEL))


def _inproj_kernel(x_ref, g_ref, sh_ref, sc_ref, w_ref, o_ref, h_ref):
    @pl.when(pl.program_id(1) == 0)
    def _():
        h_ref[...] = _norm_mod(x_ref[...], g_ref[...], sh_ref[...], sc_ref[...]).astype(BF16)

    o_ref[...] = jnp.dot(h_ref[...], w_ref[...], preferred_element_type=F32)


def _inproj_call(x, g, sh, sc, w_pad, *, tm, rows_per_mod):
    m = x.shape[0]
    tn = 1024
    return pl.pallas_call(
        _inproj_kernel,
        out_shape=jax.ShapeDtypeStruct((m, D_IN_PAD), F32),
        grid=(m // tm, D_IN_PAD // tn),
        in_specs=[
            pl.BlockSpec((tm, D_MODEL), lambda i, j: (i, 0)),
            pl.BlockSpec((1, D_MODEL), lambda i, j: (0, 0)),
            _mod_spec(sh, tm, rows_per_mod),
            _mod_spec(sc, tm, rows_per_mod),
            pl.BlockSpec((D_MODEL, tn), lambda i, j: (0, j)),
        ],
        out_specs=pl.BlockSpec((tm, tn), lambda i, j: (i, j)),
        scratch_shapes=[pltpu.VMEM((tm, D_MODEL), BF16)],
        compiler_params=pltpu.CompilerParams(
            dimension_semantics=("parallel", "arbitrary"), vmem_limit_bytes=VMEM_LIMIT_BYTES),
        name="in_proj",
    )(x, g.reshape(1, D_MODEL), sh, sc, w_pad)


def _outproj_kernel(*refs, n_pat):
    x_ref = refs[0]
    if n_pat:
        o_refs = refs[1:1 + n_pat]
        l_refs = refs[1 + n_pat:1 + 2 * n_pat]
        ga_ref, ssd_ref, gt_ref, wa_ref, ws_ref, out_ref = refs[1 + 2 * n_pat:]
        lses = [r[...] for r in l_refs]
        mx = functools.reduce(jnp.maximum, lses)
        ws = [jnp.exp(v - mx) for v in lses]
        wsum = functools.reduce(lambda a, b: a + b, ws)
        expand = _head_lane_mask(128).astype(BF16)
        att = None
        for w, o_ref in zip(ws, o_refs):
            wn = w / wsum
            w_hi = wn.astype(BF16)
            w_lo = (wn - w_hi.astype(F32)).astype(BF16)
            wx = (jnp.dot(w_hi, expand, preferred_element_type=F32)
                  + jnp.dot(w_lo, expand, preferred_element_type=F32))
            term = wx * o_ref[...]
            att = term if att is None else att + term
        att = att * lax.rsqrt(jnp.mean(att * att, axis=-1, keepdims=True) + NORM_EPS) * ga_ref[...]
    else:
        att_ref, ssd_ref, gt_ref, wa_ref, ws_ref, out_ref = refs[1:]
        att = att_ref[...]
    mix = jnp.dot(att.astype(BF16), wa_ref[...], preferred_element_type=F32)
    mix += jnp.dot(ssd_ref[...].astype(BF16), ws_ref[...], preferred_element_type=F32)
    out_ref[...] = x_ref[...] + gt_ref[...] * mix


def _outproj_call(x, att_parts, ssd, gt, w_att, w_ssd, g_attn, *, tm, rows_per_mod):
    m = x.shape[0]
    n_pat = len(att_parts) // 2
    if gt.ndim == 3:
        tiles_per_row = rows_per_mod // tm
        gt_spec = pl.BlockSpec((None, 1, D_MODEL), lambda i: (i // tiles_per_row, 0, 0))
    else:
        gt_spec = pl.BlockSpec((tm, D_MODEL), lambda i: (i, 0))
    row = lambda width: pl.BlockSpec((tm, width), lambda i: (i, 0))
    if n_pat:
        att_specs = [row(D_ATT)] * n_pat + [row(128)] * n_pat + [pl.BlockSpec((1, D_ATT), lambda i: (0, 0))]
        att_args = list(att_parts) + [g_attn.reshape(1, D_ATT)]
    else:
        att_specs = [row(D_ATT)]
        att_args = list(att_parts)
    weight = lambda k: pl.BlockSpec((k, D_MODEL), lambda i: (0, 0), pipeline_mode=pl.Buffered(1))
    return pl.pallas_call(
        functools.partial(_outproj_kernel, n_pat=n_pat),
        out_shape=jax.ShapeDtypeStruct((m, D_MODEL), F32),
        grid=(m // tm,),
        in_specs=[row(D_MODEL)] + att_specs + [row(D_SSM), gt_spec, weight(D_ATT), weight(D_SSM)],
        out_specs=row(D_MODEL),
        compiler_params=pltpu.CompilerParams(
            dimension_semantics=("parallel",), vmem_limit_bytes=VMEM_LIMIT_BYTES),
        name="out_proj_merge" if n_pat else "out_proj",
    )(x, *att_args, ssd, gt, w_att, w_ssd)


def _rel_bucket(dist):
    max_exact = RPB_BUCKETS // 2
    df = jnp.maximum(dist, 1).astype(jnp.float32)
    large = max_exact + (jnp.log(df / max_exact) / math.log(RPB_MAX_DIST / max_exact)
                         * (RPB_BUCKETS - max_exact)).astype(jnp.int32)
    large = jnp.minimum(large, RPB_BUCKETS - 1)
    return jnp.where(dist < max_exact, dist, large)


def _head_lane_mask(rows=H_ATT):
    r = lax.broadcasted_iota(jnp.int32, (rows, D_ATT), 0)
    c = lax.broadcasted_iota(jnp.int32, (rows, D_ATT), 1)
    return (c // HEAD_DIM == r).astype(F32)


NT_DIMS = (((1,), (1,)), ((), ()))


def _attn_prompt_kernel(q_ref, kp_ref, kc_ref, vp_ref, vc_ref, bias_ref, o_ref, lse_ref):
    nb = pl.program_id(2)
    blk = ATT_BLOCK
    col = lax.broadcasted_iota(jnp.int32, (blk, 2 * blk), 1)
    has_prev = (col >= blk) | (nb > 0)
    lane = lax.broadcasted_iota(jnp.int32, (blk, 2 * HEAD_DIM), 1)
    lo = lane < HEAD_DIM
    lse_lane = lax.broadcasted_iota(jnp.int32, (blk, 128), 1)
    lse_tile = jnp.zeros((blk, 128), F32)
    scale = HEAD_DIM ** -0.5
    for hp in range(H_ATT // 2):
        sl = slice(hp * 2 * HEAD_DIM, (hp + 1) * 2 * HEAD_DIM)
        q2 = q_ref[:, sl] * scale
        k2 = jnp.concatenate([kp_ref[:, sl], kc_ref[:, sl]], axis=0).astype(BF16)
        v2 = jnp.concatenate([vp_ref[:, sl], vc_ref[:, sl]], axis=0).astype(BF16)
        outs = []
        for half in range(2):
            h = 2 * hp + half
            qh = jnp.where(lo if half == 0 else ~lo, q2, 0.0).astype(BF16)
            s = lax.dot_general(qh, k2, NT_DIMS, preferred_element_type=F32) + bias_ref[h]
            s = jnp.where(has_prev, s, -jnp.inf)
            m = jnp.max(s, axis=-1, keepdims=True)
            p = jnp.exp(s - m)
            l = jnp.sum(p, axis=-1, keepdims=True)
            pv = jnp.dot(p.astype(BF16), v2, preferred_element_type=F32)
            outs.append(pv / l)
            lse_tile = jnp.where(lse_lane == h, m + jnp.log(l), lse_tile)
        o_ref[:, sl] = jnp.where(lo, outs[0], outs[1])
    lse_ref[...] = lse_tile


def _attn_prompt_call(proj, bias, dil):
    b, s_len, _ = proj.shape
    blk = ATT_BLOCK
    n_sub = s_len // dil
    nbk = n_sub // blk
    tiles = D_IN_PAD // D_ATT
    pv = proj.reshape(b, n_sub, dil * D_IN_PAD)
    cur = lambda t: (lambda bi, r, nb: (bi, nb, r * tiles + t))
    prev = lambda t: (lambda bi, r, nb: (bi, jnp.maximum(nb - 1, 0), r * tiles + t))
    spec = lambda f: pl.BlockSpec((None, blk, D_ATT), f)
    o, lse = pl.pallas_call(
        _attn_prompt_kernel,
        out_shape=(jax.ShapeDtypeStruct((b, n_sub, dil * D_ATT), F32),
                   jax.ShapeDtypeStruct((b, n_sub, dil * 128), F32)),
        grid=(b, dil, nbk),
        in_specs=[spec(cur(0)), spec(prev(1)), spec(cur(1)), spec(prev(2)), spec(cur(2)),
                  pl.BlockSpec((H_ATT, blk, 2 * blk), lambda bi, r, nb: (0, 0, 0))],
        out_specs=(pl.BlockSpec((None, blk, D_ATT), lambda bi, r, nb: (bi, nb, r)),
                   pl.BlockSpec((None, blk, 128), lambda bi, r, nb: (bi, nb, r))),
        compiler_params=pltpu.CompilerParams(
            dimension_semantics=("parallel", "parallel", "arbitrary"),
            vmem_limit_bytes=VMEM_LIMIT_BYTES),
        name=f"attn_prompt_d{dil}",
    )(pv, pv, pv, pv, pv, bias)
    return o.reshape(b * s_len, D_ATT), lse.reshape(b * s_len, 128)


def _prompt_bias(rpb, window, dil):
    span = window // dil
    blk = ATT_BLOCK
    rel = jnp.arange(blk)[:, None] + blk - jnp.arange(2 * blk)[None, :]
    in_band = (rel >= 0) & (rel <= span)
    bias = rpb[_rel_bucket(jnp.clip(rel, 0, span) * dil)].transpose(2, 0, 1).astype(F32)
    return jnp.where(in_band[None], bias, -jnp.inf)


def _attn_sample_kernel(q_ref, kn_ref, vn_ref, kt_ref, vt_ref, lbc_ref, lbn_ref, g_ref, o_ref, acc_ref):
    n_q = q_ref.shape[0]
    pad = jnp.zeros((8 - n_q, D_ATT), F32)
    q8 = jnp.concatenate([q_ref[...] * (HEAD_DIM ** -0.5), pad], axis=0).astype(BF16)
    kn8 = jnp.concatenate([kn_ref[...], pad], axis=0).astype(BF16)
    vn8 = jnp.concatenate([vn_ref[...], pad], axis=0).astype(BF16)
    for h in range(H_ATT):
        sl = slice(h * HEAD_DIM, (h + 1) * HEAD_DIM)
        qh = q8[:, sl]
        t_c = jnp.dot(qh, kt_ref[h].astype(BF16), preferred_element_type=F32) + lbc_ref[h]
        t_n = lax.dot_general(qh, kn8[:, sl], NT_DIMS, preferred_element_type=F32) + lbn_ref[h]
        m = jnp.maximum(jnp.max(t_c, axis=-1, keepdims=True), jnp.max(t_n, axis=-1, keepdims=True))
        p_c = jnp.exp(t_c - m)
        p_n = jnp.exp(t_n - m)
        l = jnp.sum(p_c, axis=-1, keepdims=True) + jnp.sum(p_n, axis=-1, keepdims=True)
        acc = lax.dot_general(p_c.astype(BF16), vt_ref[h].astype(BF16), NT_DIMS, preferred_element_type=F32)
        acc += jnp.dot(p_n.astype(BF16), vn8[:, sl], preferred_element_type=F32)
        acc_ref[:, sl] = acc / l
    att = acc_ref[0:n_q, :]
    o_ref[...] = att * lax.rsqrt(jnp.mean(att * att, axis=-1, keepdims=True) + NORM_EPS) * g_ref[...]


def _attn_sample_call(proj, cache_kt, cache_vt, lb_c, lb_n, g_attn):
    b, n_q, _ = proj.shape
    wb = cache_kt.shape[3]
    qspec = lambda t: pl.BlockSpec((None, n_q, D_ATT), lambda bi: (bi, 0, t))
    cspec = pl.BlockSpec((None, H_ATT, HEAD_DIM, wb), lambda bi: (bi, 0, 0, 0))
    out = pl.pallas_call(
        _attn_sample_kernel,
        out_shape=jax.ShapeDtypeStruct((b, n_q, D_ATT), F32),
        grid=(b,),
        in_specs=[qspec(0), qspec(1), qspec(2), cspec, cspec,
                  pl.BlockSpec((H_ATT, 8, wb), lambda bi: (0, 0, 0)),
                  pl.BlockSpec((H_ATT, 8, 8), lambda bi: (0, 0, 0)),
                  pl.BlockSpec((1, D_ATT), lambda bi: (0, 0))],
        out_specs=pl.BlockSpec((None, n_q, D_ATT), lambda bi: (bi, 0, 0)),
        scratch_shapes=[pltpu.VMEM((8, D_ATT), F32)],
        compiler_params=pltpu.CompilerParams(
            dimension_semantics=("parallel",), vmem_limit_bytes=VMEM_LIMIT_BYTES),
        name="attn_sample",
    )(proj, proj, proj, cache_kt, cache_vt, lb_c, lb_n, g_attn.reshape(1, D_ATT))
    return out.reshape(b * n_q, D_ATT)


def _sample_log_bias(rpb, n_q, wb):
    n = jnp.arange(8)[:, None] % n_q

    def log_bias(dist, real):
        terms = []
        for window, dil in DILATED_PATTERNS:
            span = window // dil
            table = rpb[_rel_bucket(jnp.arange(span + 1) * dil)].T.astype(F32)
            ok = real & (dist >= 0) & (dist % dil == 0) & (dist // dil <= span)
            terms.append(jnp.where(ok[None], table[:, jnp.clip(dist // dil, 0, span)], -jnp.inf))
        return jax.nn.logsumexp(jnp.stack(terms), axis=0)

    pos = jnp.arange(wb)[None, :]
    tok = jnp.arange(8)[None, :]
    return log_bias(wb + n - pos, pos >= 0), log_bias(n - tok, tok < n_q)


TN_DIMS = (((0,), (0,)), ((), ()))
GROUP_W = D_SSM // SSM_GROUPS


def _split_bf16(v, pieces):
    out = []
    for _ in range(pieces):
        hi = v.astype(BF16)
        out.append(hi)
        v = v - hi.astype(F32)
    return out


def _dot_split(v, rhs_bf16, pieces, dims=None):
    acc = None
    for piece in _split_bf16(v, pieces):
        if dims is None:
            t = jnp.dot(piece, rhs_bf16, preferred_element_type=F32)
        else:
            t = lax.dot_general(piece, rhs_bf16, dims, preferred_element_type=F32)
        acc = t if acc is None else acc + t
    return acc


def _softplus(v):
    return jnp.maximum(v, 0.0) + jnp.log1p(jnp.exp(-jnp.abs(v)))


def _causal_conv(ext_ref, rows, cw_ref, cb_ref):
    conv = cb_ref[...]
    for k in range(CONV_W):
        conv = conv + cw_ref[CONV_W - 1 - k:CONV_W - k, :] * ext_ref[8 - k:8 - k + rows, :]
    return _silu(conv)


def _gated_group_norm(y, z, g):
    gated = y * _silu(z)
    parts = []
    for grp in range(SSM_GROUPS):
        seg = gated[:, grp * GROUP_W:(grp + 1) * GROUP_W]
        parts.append(seg * lax.rsqrt(jnp.mean(seg * seg, axis=-1, keepdims=True) + NORM_EPS))
    return jnp.concatenate(parts, axis=1) * g


def _ssd_prompt_kernel(z_ref, xd_ref, cw_ref, cb_ref, dtb_ref, alog_ref, dsk_ref, g_ref,
                       y_ref, hout_ref, ht_ref, ext_ref):
    c = pl.program_id(1)
    q = SSD_CHUNK

    @pl.when(c == 0)
    def _():
        ht_ref[...] = jnp.zeros_like(ht_ref)
        ext_ref[0:8, :] = jnp.zeros((8, CONV_DIM), F32)

    x_raw = xd_ref[:, 0:CONV_DIM]
    ext_ref[8:8 + q, :] = x_raw
    act = _causal_conv(ext_ref, q, cw_ref, cb_ref)
    ext_ref[0:8, :] = x_raw[q - 8:q, :]
    xs = act[:, 0:D_SSM]
    bm = act[:, D_SSM:D_SSM + SSM_GROUPS * D_STATE]
    cm = act[:, D_SSM + SSM_GROUPS * D_STATE:CONV_DIM]

    dt = _softplus(xd_ref[:, CONV_DIM:CONV_DIM + 128] + dtb_ref[...])
    da = dt * (-jnp.exp(alog_ref[...]))
    ti = lax.broadcasted_iota(jnp.int32, (q, q), 0)
    si = lax.broadcasted_iota(jnp.int32, (q, q), 1)
    causal = ti >= si
    a_cum = None
    for piece in _split_bf16(da, 3):
        t = jnp.dot(causal.astype(BF16), piece, preferred_element_type=F32)
        a_cum = t if a_cum is None else a_cum + t
    a_cum_t = a_cum.T
    expand = _head_lane_mask(128).astype(BF16)
    eac = jnp.exp(a_cum)
    dt_x = _dot_split(dt, expand, 2)
    eac_x = _dot_split(eac, expand, 2)
    dend_x = _dot_split(jnp.exp(a_cum[q - 1:q, :] - a_cum), expand, 2)
    xdt = xs * dt_x
    xdd = (xdt * dend_x).astype(BF16)
    xdt_b = xdt.astype(BF16)
    lane = lax.broadcasted_iota(jnp.int32, (q, 2 * SSM_HEAD_DIM), 1)
    lo = lane < SSM_HEAD_DIM
    heads_per_group = H_SSM // SSM_GROUPS
    for grp in range(SSM_GROUPS):
        gs = slice(grp * GROUP_W, (grp + 1) * GROUP_W)
        bm_g = bm[:, grp * D_STATE:(grp + 1) * D_STATE]
        cm_b = cm[:, grp * D_STATE:(grp + 1) * D_STATE].astype(BF16)
        cb_mat = lax.dot_general(cm_b, bm_g.astype(BF16), NT_DIMS, preferred_element_type=F32)
        h_in = ht_ref[:, gs]
        y_off = jnp.dot(cm_b, h_in.astype(BF16), preferred_element_type=F32) * eac_x[:, gs]
        states = jnp.dot(bm_g.T.astype(BF16), xdd[:, gs], preferred_element_type=F32)
        ht_ref[:, gs] = h_in * eac_x[q - 1:q, gs] + states
        for hp in range(heads_per_group // 2):
            h0 = grp * heads_per_group + 2 * hp
            ps = slice(h0 * SSM_HEAD_DIM, (h0 + 2) * SSM_HEAD_DIM)
            x_pair = xdt_b[:, ps]
            halves = []
            for half in range(2):
                h = h0 + half
                seg = a_cum[:, h:h + 1] - a_cum_t[h:h + 1, :]
                decay = jnp.exp(jnp.where(causal, seg, -jnp.inf))
                halves.append(jnp.dot((cb_mat * decay).astype(BF16), x_pair, preferred_element_type=F32))
            y_diag = jnp.where(lo, halves[0], halves[1])
            off = slice(2 * hp * SSM_HEAD_DIM, (2 * hp + 2) * SSM_HEAD_DIM)
            y_ref[:, ps] = y_diag + y_off[:, off] + dsk_ref[:, ps] * xs[:, ps]
    y_ref[...] = _gated_group_norm(y_ref[...], z_ref[...], g_ref[...])

    @pl.when(c == pl.num_programs(1) - 1)
    def _():
        hout_ref[...] = ht_ref[...].T


def _ssd_prompt_call(proj, sp):
    b, s_len, _ = proj.shape
    q = SSD_CHUNK
    const = lambda shape: pl.BlockSpec(shape, lambda bi, c: (0,) * len(shape))
    y, h_last = pl.pallas_call(
        _ssd_prompt_kernel,
        out_shape=(jax.ShapeDtypeStruct((b, s_len, D_SSM), F32),
                   jax.ShapeDtypeStruct((b, D_SSM, D_STATE), F32)),
        grid=(b, s_len // q),
        in_specs=[pl.BlockSpec((None, q, D_SSM), lambda bi, c: (bi, c, 3)),
                  pl.BlockSpec((None, q, 2048), lambda bi, c: (bi, c, 2)),
                  const((CONV_W, CONV_DIM)), const((1, CONV_DIM)), const((1, 128)), const((1, 128)),
                  const((1, D_SSM)), const((1, D_SSM))],
        out_specs=(pl.BlockSpec((None, q, D_SSM), lambda bi, c: (bi, c, 0)),
                   pl.BlockSpec((None, D_SSM, D_STATE), lambda bi, c: (bi, 0, 0))),
        scratch_shapes=[pltpu.VMEM((D_STATE, D_SSM), F32), pltpu.VMEM((8 + q, CONV_DIM), F32)],
        compiler_params=pltpu.CompilerParams(
            dimension_semantics=("parallel", "arbitrary"), vmem_limit_bytes=VMEM_LIMIT_BYTES),
        name="ssd_prompt",
    )(proj, proj, sp['conv_w'], sp['conv_b'], sp['dt_bias'], sp['a_log'], sp['d_skip_x'], sp['g_ssm_out'])
    return y.reshape(b * s_len, D_SSM), h_last.reshape(b, H_SSM, SSM_HEAD_DIM, D_STATE)


def _ssd_sample_kernel(z_ref, xd_ref, buf_ref, h0_ref, cw_ref, cb_ref, dtb_ref, alog_ref, dsk_ref, g_ref,
                       y_ref, hout_ref, ext_ref):
    n_seq, n_tok, _ = z_ref.shape
    expand = _head_lane_mask(128).astype(BF16)
    ti = lax.broadcasted_iota(jnp.int32, (8, 8), 0)
    si = lax.broadcasted_iota(jnp.int32, (8, 8), 1)
    causal = (ti >= si) & (si < n_tok)
    real = lax.broadcasted_iota(jnp.int32, (8, 1), 0) < n_tok
    ones = jnp.ones((8, D_STATE), BF16)
    ext_ref[...] = jnp.zeros_like(ext_ref)

    def body(j, carry):
        ext_ref[8 - (CONV_W - 1):8, :] = buf_ref[j]
        ext_ref[8:8 + n_tok, :] = xd_ref[j][:, 0:CONV_DIM]
        act = _causal_conv(ext_ref, 8, cw_ref, cb_ref)
        xs = act[:, 0:D_SSM]
        bm = act[:, D_SSM:D_SSM + SSM_GROUPS * D_STATE]
        cm = act[:, D_SSM + SSM_GROUPS * D_STATE:CONV_DIM]
        pad = jnp.zeros((8 - n_tok, 128), F32)
        dt = _softplus(jnp.concatenate([xd_ref[j][:, CONV_DIM:CONV_DIM + 128], pad], axis=0) + dtb_ref[...])
        da = jnp.where(real, dt * (-jnp.exp(alog_ref[...])), 0.0)
        a_cum = None
        for piece in _split_bf16(da, 3):
            t = jnp.dot((ti >= si).astype(BF16), piece, preferred_element_type=F32)
            a_cum = t if a_cum is None else a_cum + t
        a_last = a_cum[n_tok - 1:n_tok, :]
        acum_x = _dot_split(a_cum, expand, 3)
        eac_x = jnp.exp(acum_x)
        dt_x = _dot_split(dt, expand, 2)
        xdt = jnp.where(real, xs * dt_x, 0.0)
        xdd = (xdt * jnp.exp(acum_x[n_tok - 1:n_tok, :] - acum_x)).astype(BF16)
        last_x = jnp.where(lax.broadcasted_iota(jnp.int32, (8, 1), 0) == n_tok - 1, eac_x, 0.0)
        keep = _dot_split(last_x, ones, 2, dims=TN_DIMS)
        y_parts, new_state = [], []
        for grp in range(SSM_GROUPS):
            gs = slice(grp * GROUP_W, (grp + 1) * GROUP_W)
            bm_b = bm[:, grp * D_STATE:(grp + 1) * D_STATE].astype(BF16)
            cm_b = cm[:, grp * D_STATE:(grp + 1) * D_STATE].astype(BF16)
            cb_mat = jnp.where(causal, lax.dot_general(cm_b, bm_b, NT_DIMS, preferred_element_type=F32), 0.0)
            h0_g = h0_ref[j, gs, :]
            y_g = lax.dot_general(cm_b, h0_g.astype(BF16), NT_DIMS, preferred_element_type=F32) * eac_x[:, gs]
            for s in range(n_tok):
                coef = cb_mat[:, s:s + 1] * jnp.exp(jnp.where(ti[:, 0:1] >= s, acum_x[:, gs] - acum_x[s:s + 1, gs], -jnp.inf))
                y_g = y_g + coef * xdt[s:s + 1, gs]
            y_parts.append(y_g)
            states = lax.dot_general(xdd[:, gs], bm_b, TN_DIMS, preferred_element_type=F32)
            hout_ref[j, gs, :] = h0_g * keep[gs, :] + states
        y = jnp.concatenate(y_parts, axis=1) + dsk_ref[...] * xs
        zj = jnp.concatenate([z_ref[j], jnp.zeros((8 - n_tok, D_SSM), F32)], axis=0)
        y_ref[j] = _gated_group_norm(y, zj, g_ref[...])[0:n_tok, :]
        return carry

    lax.fori_loop(0, n_seq, body, 0)


def _ssd_sample_call(proj, conv_buf, h0, sp):
    b, n_tok, _ = proj.shape
    nseq = 8
    const = lambda shape: pl.BlockSpec(shape, lambda i: (0,) * len(shape))
    y, h_last = pl.pallas_call(
        _ssd_sample_kernel,
        out_shape=(jax.ShapeDtypeStruct((b, n_tok, D_SSM), F32),
                   jax.ShapeDtypeStruct((b, D_SSM, D_STATE), F32)),
        grid=(b // nseq,),
        in_specs=[pl.BlockSpec((nseq, n_tok, D_SSM), lambda i: (i, 0, 3)),
                  pl.BlockSpec((nseq, n_tok, 2048), lambda i: (i, 0, 2)),
                  pl.BlockSpec((nseq, CONV_W - 1, CONV_DIM), lambda i: (i, 0, 0)),
                  pl.BlockSpec((nseq, D_SSM, D_STATE), lambda i: (i, 0, 0)),
                  const((CONV_W, CONV_DIM)), const((1, CONV_DIM)), const((1, 128)), const((1, 128)),
                  const((1, D_SSM)), const((1, D_SSM))],
        out_specs=(pl.BlockSpec((nseq, n_tok, D_SSM), lambda i: (i, 0, 0)),
                   pl.BlockSpec((nseq, D_SSM, D_STATE), lambda i: (i, 0, 0))),
        scratch_shapes=[pltpu.VMEM((16, CONV_DIM), F32)],
        compiler_params=pltpu.CompilerParams(
            dimension_semantics=("parallel",), vmem_limit_bytes=VMEM_LIMIT_BYTES),
        name="ssd_sample",
    )(proj, proj, conv_buf, h0.reshape(b, D_SSM, D_STATE), sp['conv_w'], sp['conv_b'], sp['dt_bias'],
      sp['a_log'], sp['d_skip_x'], sp['g_ssm_out'])
    return y.reshape(b * n_tok, D_SSM), h_last.reshape(b, H_SSM, SSM_HEAD_DIM, D_STATE)


def _layer(x, mods, p, attend, conv_buf, h0, *, tm, g_final):
    b, L, _ = x.shape
    m = b * L
    sh1, sc1, g1, sh2, sc2, g2, sh3, sc3, g3 = mods
    x2 = x.reshape(m, D_MODEL)
    kw = dict(tm=tm, rows_per_mod=L)
    x2 = _ffn_call(x2, p['g_ffn1'], sh1, sc1, g1, p['w_gate1'], p['w_up1'], p['w_down1'],
                   g_final, final_norm=False, **kw)
    proj = _inproj_call(x2, p['g_mix'], sh2, sc2, p['w_in_pad'], tm=min(tm, 512), rows_per_mod=L)
    proj = proj.reshape(b, L, D_IN_PAD)
    k = proj[..., D_ATT:2 * D_ATT].reshape(b, L, H_ATT, HEAD_DIM)
    v = proj[..., 2 * D_ATT:3 * D_ATT].reshape(b, L, H_ATT, HEAD_DIM)
    o = 3 * D_ATT + D_SSM
    conv_new = proj[:, L - (CONV_W - 1):, o:o + CONV_DIM]
    att_parts = attend(proj)
    if h0 is None:
        ssd_o, h_last = _ssd_prompt_call(proj, p)
    else:
        ssd_o, h_last = _ssd_sample_call(proj, conv_buf, h0, p)
    x2 = _outproj_call(x2, att_parts, ssd_o, g2, p['w_out_att'], p['w_out_ssm'],
                       p['g_attn_out'], tm=min(tm, 256), rows_per_mod=L)
    x2 = _ffn_call(x2, p['g_ffn2'], sh3, sc3, g3, p['w_gate2'], p['w_up2'], p['w_down2'],
                   g_final, final_norm=True, **kw)
    return x2.reshape(b, L, D_MODEL), k, v, h_last, conv_new


def kernel(x_prompt, x_sample, cache_k, cache_v, state_ssm, state_conv, c_prompt, c_sample,
           rpb_table, w_ada, b_ada, g_ffn1, w_gate1, w_up1, w_down1, g_mix, w_in, conv_w, conv_b,
           dt_bias, a_log, d_skip, g_ssm_out, g_attn_out, w_out, g_ffn2, w_gate2, w_up2, w_down2, g_final):
    bp, sp, _ = x_prompt.shape
    bs, ls, _ = x_sample.shape
    wbp = min(WINDOW_MAX, sp)
    layer = 0
    w_in_pad = jnp.pad(w_in[layer].astype(BF16), ((0, 0), (0, D_IN_PAD - D_IN)))
    p = {'g_ffn1': g_ffn1[layer], 'w_gate1': w_gate1[layer].astype(BF16), 'w_up1': w_up1[layer].astype(BF16),
         'w_down1': w_down1[layer].astype(BF16), 'g_mix': g_mix[layer], 'w_in_pad': w_in_pad,
         'conv_w': conv_w[layer], 'conv_b': conv_b[layer].reshape(1, CONV_DIM),
         'dt_bias': jnp.pad(dt_bias[layer], (0, 128 - H_SSM)).reshape(1, 128),
         'a_log': jnp.pad(a_log[layer], (0, 128 - H_SSM)).reshape(1, 128),
         'd_skip_x': jnp.repeat(d_skip[layer], SSM_HEAD_DIM).reshape(1, D_SSM),
         'g_ssm_out': g_ssm_out[layer].reshape(1, D_SSM), 'g_attn_out': g_attn_out[layer],
         'w_out_att': w_out[layer, :D_ATT].astype(BF16), 'w_out_ssm': w_out[layer, D_ATT:].astype(BF16),
         'g_ffn2': g_ffn2[layer], 'w_gate2': w_gate2[layer].astype(BF16), 'w_up2': w_up2[layer].astype(BF16),
         'w_down2': w_down2[layer].astype(BF16)}

    n_c = bp + bs
    c_rows = -(-n_c // 16) * 16
    c_all = jnp.pad(jnp.concatenate([c_prompt, c_sample], axis=0), ((0, c_rows - n_c), (0, 0)))
    mod = _mod_call(c_all, w_ada[layer], b_ada[layer])
    mods_p = [mod[:bp, i * D_MODEL:(i + 1) * D_MODEL].reshape(bp, 1, D_MODEL) for i in range(N_MOD)]
    mods_s = [jnp.repeat(mod[bp:n_c, i * D_MODEL:(i + 1) * D_MODEL], ls, axis=0) for i in range(N_MOD)]

    def attend_prompt(proj):
        outs, lses = [], []
        for window, dil in DILATED_PATTERNS:
            o, lse = _attn_prompt_call(proj, _prompt_bias(rpb_table, window, dil), dil)
            outs.append(o)
            lses.append(lse)
        return outs + lses

    lb_c, lb_n = _sample_log_bias(rpb_table, ls, cache_k.shape[2])
    cache_kt = jnp.transpose(cache_k[layer], (0, 2, 3, 1))
    cache_vt = jnp.transpose(cache_v[layer], (0, 2, 3, 1))

    def attend_sample(proj):
        return [_attn_sample_call(proj, cache_kt, cache_vt, lb_c, lb_n, p['g_attn_out'])]

    yp, kp, vp, hp, cp = _layer(x_prompt, mods_p, p, attend_prompt, None, None, tm=1024, g_final=g_final)
    ys, kn, vn, hs, cs = _layer(x_sample, mods_s, p, attend_sample, state_conv[layer], state_ssm[layer],
                                tm=bs * ls, g_final=g_final)
    return (yp, ys, kp[None, :, -wbp:], vp[None, :, -wbp:], kn[None], vn[None],
            hp[None], hs[None], cp[None], cs[None])
```

```python
import functools
import math

import jax
import jax.numpy as jnp
from jax import lax
from jax.experimental import pallas as pl
from jax.experimental.pallas import tpu as pltpu

D_MODEL = 2048
HEAD_DIM = 64
D_ATT = 1024
D_SSM = 1024
H_ATT = 16
SSM_HEAD_DIM = 64
H_SSM = 16
SSM_GROUPS = 2
D_STATE = 128
CONV_W = 4
SSD_CHUNK = 128
CONV_DIM = D_SSM + 2 * SSM_GROUPS * D_STATE
D_FF = 5632
DILATED_PATTERNS = ((128, 1), (512, 4), (2048, 16))
WINDOW_MAX = 2048
ATT_BLOCK = 128
RPB_BUCKETS = 32
RPB_MAX_DIST = 2048
NORM_EPS = 1e-6
N_MOD = 9
PROJ_SIZES = (D_ATT, D_ATT, D_ATT, D_SSM, CONV_DIM, H_SSM)
D_IN = sum(PROJ_SIZES)
D_IN_PAD = 6144

VMEM_LIMIT_BYTES = 56 * 1024 * 1024

BF16 = jnp.bfloat16
F32 = jnp.float32


def _silu(v):
    return v * jax.nn.sigmoid(v)


def _norm_mod(x, g, shift, scale):
    y = x * lax.rsqrt(jnp.mean(x * x, axis=-1, keepdims=True) + NORM_EPS)
    return (y * g) * (1.0 + scale) + shift


def _mod_kernel(c_ref, w_ref, b_ref, o_ref, a_ref):
    @pl.when(pl.program_id(0) == 0)
    def _():
        a_ref[...] = _silu(c_ref[...]).astype(BF16)

    o_ref[...] = jnp.dot(a_ref[...], w_ref[...].astype(BF16),
                         preferred_element_type=F32) + b_ref[...]


def _mod_call(c_all, w_ada, b_ada):
    rows = c_all.shape[0]
    n = w_ada.shape[1]
    tn = 1024
    return pl.pallas_call(
        _mod_kernel,
        out_shape=jax.ShapeDtypeStruct((rows, n), F32),
        grid=(n // tn,),
        in_specs=[
            pl.BlockSpec((rows, D_MODEL), lambda j: (0, 0)),
            pl.BlockSpec((D_MODEL, tn), lambda j: (0, j)),
            pl.BlockSpec((1, tn), lambda j: (0, j)),
        ],
        out_specs=pl.BlockSpec((rows, tn), lambda j: (0, j)),
        scratch_shapes=[pltpu.VMEM((rows, D_MODEL), BF16)],
        compiler_params=pltpu.CompilerParams(
            dimension_semantics=("arbitrary",), vmem_limit_bytes=VMEM_LIMIT_BYTES),
        name="adaln_mod",
    )(c_all, w_ada, b_ada.reshape(1, n))


def _ffn_kernel(x_ref, g_ref, sh_ref, sc_ref, gt_ref, wg_ref, wu_ref, wd_ref, gf_ref,
                o_ref, h_ref, *, final_norm):
    j = pl.program_id(1)

    @pl.when(j == 0)
    def _():
        h_ref[...] = _norm_mod(x_ref[...], g_ref[...], sh_ref[...], sc_ref[...]).astype(BF16)
        o_ref[...] = jnp.zeros_like(o_ref)

    h = h_ref[...]
    a = jnp.dot(h, wg_ref[...], preferred_element_type=F32)
    b = jnp.dot(h, wu_ref[...], preferred_element_type=F32)
    s = (_silu(a) * b).astype(BF16)
    o_ref[...] += jnp.dot(s, wd_ref[...], preferred_element_type=F32)

    @pl.when(j == pl.num_programs(1) - 1)
    def _():
        y = x_ref[...] + 0.5 * gt_ref[...] * o_ref[...]
        if final_norm:
            y = y * lax.rsqrt(jnp.mean(y * y, axis=-1, keepdims=True) + NORM_EPS) * gf_ref[...]
        o_ref[...] = y


def _mod_spec(mod, tm, rows_per_mod):
    if mod.ndim == 3:
        tiles_per_row = rows_per_mod // tm
        return pl.BlockSpec((None, 1, D_MODEL), lambda i, j: (i // tiles_per_row, 0, 0))
    return pl.BlockSpec((tm, D_MODEL), lambda i, j: (i, 0))


def _ffn_call(x, g, sh, sc, gt, wg, wu, wd, g_final, *, tm, rows_per_mod, final_norm):
    m = x.shape[0]
    tf = 512
    row = lambda i, j: (i, 0)
    const = lambda i, j: (0, 0)
    return pl.pallas_call(
        functools.partial(_ffn_kernel, final_norm=final_norm),
        out_shape=jax.ShapeDtypeStruct((m, D_MODEL), F32),
        grid=(m // tm, D_FF // tf),
        in_specs=[
            pl.BlockSpec((tm, D_MODEL), row, pipeline_mode=pl.Buffered(1)),
            pl.BlockSpec((1, D_MODEL), const),
            _mod_spec(sh, tm, rows_per_mod),
            _mod_spec(sc, tm, rows_per_mod),
            _mod_spec(gt, tm, rows_per_mod),
            pl.BlockSpec((D_MODEL, tf), lambda i, j: (0, j)),
            pl.BlockSpec((D_MODEL, tf), lambda i, j: (0, j)),
            pl.BlockSpec((tf, D_MODEL), lambda i, j: (j, 0)),
            pl.BlockSpec((1, D_MODEL), const),
        ],
        out_specs=pl.BlockSpec((tm, D_MODEL), row),
        scratch_shapes=[pltpu.VMEM((tm, D_MODEL), BF16)],
        compiler_params=pltpu.CompilerParams(
            dimension_semantics=("parallel", "arbitrary"), vmem_limit_bytes=VMEM_LIMIT_BYTES),
        name="ffn_final" if final_norm else "ffn",
    )(x, g.reshape(1, D_MODEL), sh, sc, gt, wg, wu, wd, g_final.reshape(1, D_MODEL))


QKV_TILES = 3


def _inproj_kernel(x_ref, g_ref, sh_ref, sc_ref, w_ref, o_ref, qkv_ref, h_ref):
    j = pl.program_id(1)

    @pl.when(j == 0)
    def _():
        h_ref[...] = _norm_mod(x_ref[...], g_ref[...], sh_ref[...], sc_ref[...]).astype(BF16)

    res = jnp.dot(h_ref[...], w_ref[...], preferred_element_type=F32)
    o_ref[...] = res

    @pl.when(j < QKV_TILES)
    def _():
        qkv_ref[...] = res.astype(BF16)


def _inproj_call(x, g, sh, sc, w_pad, *, tm, rows_per_mod):
    m = x.shape[0]
    tn = D_ATT
    return pl.pallas_call(
        _inproj_kernel,
        out_shape=(jax.ShapeDtypeStruct((m, D_IN_PAD), F32),
                   jax.ShapeDtypeStruct((m, QKV_TILES * tn), BF16)),
        grid=(m // tm, D_IN_PAD // tn),
        in_specs=[
            pl.BlockSpec((tm, D_MODEL), lambda i, j: (i, 0), pipeline_mode=pl.Buffered(1)),
            pl.BlockSpec((1, D_MODEL), lambda i, j: (0, 0)),
            _mod_spec(sh, tm, rows_per_mod),
            _mod_spec(sc, tm, rows_per_mod),
            pl.BlockSpec((D_MODEL, tn), lambda i, j: (0, j)),
        ],
        out_specs=(pl.BlockSpec((tm, tn), lambda i, j: (i, j)),
                   pl.BlockSpec((tm, tn), lambda i, j: (i, jnp.minimum(j, QKV_TILES - 1)))),
        scratch_shapes=[pltpu.VMEM((tm, D_MODEL), BF16)],
        compiler_params=pltpu.CompilerParams(
            dimension_semantics=("parallel", "arbitrary"), vmem_limit_bytes=VMEM_LIMIT_BYTES),
        name="in_proj",
    )(x, g.reshape(1, D_MODEL), sh, sc, w_pad)


def _outproj_kernel(*refs, n_pat):
    x_ref = refs[0]
    if n_pat:
        o_refs = refs[1:1 + n_pat]
        l_refs = refs[1 + n_pat:1 + 2 * n_pat]
        ga_ref, ssd_ref, gt_ref, wa_ref, ws_ref, out_ref = refs[1 + 2 * n_pat:]
        lses = [r[...] for r in l_refs]
        mx = functools.reduce(jnp.maximum, lses)
        ws = [jnp.exp(v - mx) for v in lses]
        wsum = functools.reduce(lambda a, b: a + b, ws)
        expand = _head_lane_mask(128).astype(BF16)
        att = None
        for w, o_ref in zip(ws, o_refs):
            wn = w / wsum
            w_hi = wn.astype(BF16)
            w_lo = (wn - w_hi.astype(F32)).astype(BF16)
            wx = (jnp.dot(w_hi, expand, preferred_element_type=F32)
                  + jnp.dot(w_lo, expand, preferred_element_type=F32))
            term = wx * o_ref[...]
            att = term if att is None else att + term
        att = att * lax.rsqrt(jnp.mean(att * att, axis=-1, keepdims=True) + NORM_EPS) * ga_ref[...]
    else:
        att_ref, ssd_ref, gt_ref, wa_ref, ws_ref, out_ref = refs[1:]
        att = att_ref[...]
    mix = jnp.dot(att.astype(BF16), wa_ref[...], preferred_element_type=F32)
    mix += jnp.dot(ssd_ref[...].astype(BF16), ws_ref[...], preferred_element_type=F32)
    out_ref[...] = x_ref[...] + gt_ref[...] * mix


def _outproj_call(x, att_parts, ssd, gt, w_att, w_ssd, g_attn, *, tm, rows_per_mod):
    m = x.shape[0]
    n_pat = len(att_parts) // 2
    if gt.ndim == 3:
        tiles_per_row = rows_per_mod // tm
        gt_spec = pl.BlockSpec((None, 1, D_MODEL), lambda i: (i // tiles_per_row, 0, 0))
    else:
        gt_spec = pl.BlockSpec((tm, D_MODEL), lambda i: (i, 0))
    row = lambda width: pl.BlockSpec((tm, width), lambda i: (i, 0))
    if n_pat:
        att_specs = [row(D_ATT)] * n_pat + [row(128)] * n_pat + [pl.BlockSpec((1, D_ATT), lambda i: (0, 0))]
        att_args = list(att_parts) + [g_attn.reshape(1, D_ATT)]
    else:
        att_specs = [row(D_ATT)]
        att_args = list(att_parts)
    weight = lambda k: pl.BlockSpec((k, D_MODEL), lambda i: (0, 0), pipeline_mode=pl.Buffered(1))
    return pl.pallas_call(
        functools.partial(_outproj_kernel, n_pat=n_pat),
        out_shape=jax.ShapeDtypeStruct((m, D_MODEL), F32),
        grid=(m // tm,),
        in_specs=[row(D_MODEL)] + att_specs + [row(D_SSM), gt_spec, weight(D_ATT), weight(D_SSM)],
        out_specs=row(D_MODEL),
        compiler_params=pltpu.CompilerParams(
            dimension_semantics=("parallel",), vmem_limit_bytes=VMEM_LIMIT_BYTES),
        name="out_proj_merge" if n_pat else "out_proj",
    )(x, *att_args, ssd, gt, w_att, w_ssd)


def _rel_bucket(dist):
    max_exact = RPB_BUCKETS // 2
    df = jnp.maximum(dist, 1).astype(jnp.float32)
    large = max_exact + (jnp.log(df / max_exact) / math.log(RPB_MAX_DIST / max_exact)
                         * (RPB_BUCKETS - max_exact)).astype(jnp.int32)
    large = jnp.minimum(large, RPB_BUCKETS - 1)
    return jnp.where(dist < max_exact, dist, large)


def _head_lane_mask(rows=H_ATT):
    r = lax.broadcasted_iota(jnp.int32, (rows, D_ATT), 0)
    c = lax.broadcasted_iota(jnp.int32, (rows, D_ATT), 1)
    return (c // HEAD_DIM == r).astype(F32)


NT_DIMS = (((1,), (1,)), ((), ()))


def _attn_prompt_kernel(q_ref, kp_ref, kc_ref, vp_ref, vc_ref, bias_ref, o_ref, lse_ref):
    nb = pl.program_id(2)
    blk = ATT_BLOCK
    col = lax.broadcasted_iota(jnp.int32, (blk, 2 * blk), 1)
    has_prev = (col >= blk) | (nb > 0)
    lane = lax.broadcasted_iota(jnp.int32, (blk, 2 * HEAD_DIM), 1)
    lo = lane < HEAD_DIM
    lse_lane = lax.broadcasted_iota(jnp.int32, (blk, 128), 1)
    lse_tile = jnp.zeros((blk, 128), F32)
    scale = HEAD_DIM ** -0.5
    for hp in range(H_ATT // 2):
        sl = slice(hp * 2 * HEAD_DIM, (hp + 1) * 2 * HEAD_DIM)
        q2 = q_ref[:, sl] * scale
        k2 = jnp.concatenate([kp_ref[:, sl], kc_ref[:, sl]], axis=0)
        v2 = jnp.concatenate([vp_ref[:, sl], vc_ref[:, sl]], axis=0)
        outs = []
        for half in range(2):
            h = 2 * hp + half
            qh = jnp.where(lo if half == 0 else ~lo, q2, jnp.zeros_like(q2))
            s = lax.dot_general(qh, k2, NT_DIMS, preferred_element_type=F32) + bias_ref[h]
            s = jnp.where(has_prev, s, -jnp.inf)
            m = jnp.max(s, axis=-1, keepdims=True)
            p = jnp.exp(s - m)
            l = jnp.sum(p, axis=-1, keepdims=True)
            pv = jnp.dot(p.astype(BF16), v2, preferred_element_type=F32)
            outs.append(pv / l)
            lse_tile = jnp.where(lse_lane == h, m + jnp.log(l), lse_tile)
        o_ref[:, sl] = jnp.where(lo, outs[0], outs[1]).astype(o_ref.dtype)
    lse_ref[...] = lse_tile


def _attn_prompt_call(qkv, bias, dil):
    b, s_len, width = qkv.shape
    blk = ATT_BLOCK
    n_sub = s_len // dil
    nbk = n_sub // blk
    tiles = width // D_ATT
    pv = qkv.reshape(b, n_sub, dil * width)
    cur = lambda t: (lambda bi, r, nb: (bi, nb, r * tiles + t))
    prev = lambda t: (lambda bi, r, nb: (bi, jnp.maximum(nb - 1, 0), r * tiles + t))
    spec = lambda f: pl.BlockSpec((None, blk, D_ATT), f)
    o, lse = pl.pallas_call(
        _attn_prompt_kernel,
        out_shape=(jax.ShapeDtypeStruct((b, n_sub, dil * D_ATT), BF16),
                   jax.ShapeDtypeStruct((b, n_sub, dil * 128), F32)),
        grid=(b, dil, nbk),
        in_specs=[spec(cur(0)), spec(prev(1)), spec(cur(1)), spec(prev(2)), spec(cur(2)),
                  pl.BlockSpec((H_ATT, blk, 2 * blk), lambda bi, r, nb: (0, 0, 0))],
        out_specs=(pl.BlockSpec((None, blk, D_ATT), lambda bi, r, nb: (bi, nb, r)),
                   pl.BlockSpec((None, blk, 128), lambda bi, r, nb: (bi, nb, r))),
        compiler_params=pltpu.CompilerParams(
            dimension_semantics=("parallel", "parallel", "arbitrary"),
            vmem_limit_bytes=VMEM_LIMIT_BYTES),
        name=f"attn_prompt_d{dil}",
    )(pv, pv, pv, pv, pv, bias)
    return o.reshape(b * s_len, D_ATT), lse.reshape(b * s_len, 128)


def _prompt_bias(rpb, window, dil):
    span = window // dil
    blk = ATT_BLOCK
    assert span == blk
    table = rpb[_rel_bucket(jnp.arange(span + 1) * dil)].T.astype(F32)
    fill = jnp.full((H_ATT, blk), -jnp.inf, F32)
    row = jnp.concatenate([fill, table[:, ::-1], fill], axis=1)
    wrapped = jnp.tile(row, (1, blk))[:, :blk * 3 * blk].reshape(H_ATT, blk, 3 * blk)
    return wrapped[:, :, blk:]


def _attn_sample_kernel(q_ref, kn_ref, vn_ref, kt_ref, vt_ref, lbc_ref, lbn_ref, g_ref, o_ref, acc_ref):
    n_q = q_ref.shape[0]
    pad = jnp.zeros((8 - n_q, D_ATT), F32)
    q8 = jnp.concatenate([q_ref[...] * (HEAD_DIM ** -0.5), pad], axis=0).astype(BF16)
    kn8 = jnp.concatenate([kn_ref[...], pad], axis=0).astype(BF16)
    vn8 = jnp.concatenate([vn_ref[...], pad], axis=0).astype(BF16)
    for h in range(H_ATT):
        sl = slice(h * HEAD_DIM, (h + 1) * HEAD_DIM)
        qh = q8[:, sl]
        t_c = jnp.dot(qh, kt_ref[h].astype(BF16), preferred_element_type=F32) + lbc_ref[h]
        t_n = lax.dot_general(qh, kn8[:, sl], NT_DIMS, preferred_element_type=F32) + lbn_ref[h]
        m = jnp.maximum(jnp.max(t_c, axis=-1, keepdims=True), jnp.max(t_n, axis=-1, keepdims=True))
        p_c = jnp.exp(t_c - m)
        p_n = jnp.exp(t_n - m)
        l = jnp.sum(p_c, axis=-1, keepdims=True) + jnp.sum(p_n, axis=-1, keepdims=True)
        acc = lax.dot_general(p_c.astype(BF16), vt_ref[h].astype(BF16), NT_DIMS, preferred_element_type=F32)
        acc += jnp.dot(p_n.astype(BF16), vn8[:, sl], preferred_element_type=F32)
        acc_ref[:, sl] = acc / l
    att = acc_ref[0:n_q, :]
    o_ref[...] = att * lax.rsqrt(jnp.mean(att * att, axis=-1, keepdims=True) + NORM_EPS) * g_ref[...]


def _attn_sample_call(proj, cache_kt, cache_vt, lb_c, lb_n, g_attn):
    b, n_q, _ = proj.shape
    wb = cache_kt.shape[3]
    qspec = lambda t: pl.BlockSpec((None, n_q, D_ATT), lambda bi: (bi, 0, t))
    cspec = pl.BlockSpec((None, H_ATT, HEAD_DIM, wb), lambda bi: (bi, 0, 0, 0))
    out = pl.pallas_call(
        _attn_sample_kernel,
        out_shape=jax.ShapeDtypeStruct((b, n_q, D_ATT), F32),
        grid=(b,),
        in_specs=[qspec(0), qspec(1), qspec(2), cspec, cspec,
                  pl.BlockSpec((H_ATT, 8, wb), lambda bi: (0, 0, 0)),
                  pl.BlockSpec((H_ATT, 8, 8), lambda bi: (0, 0, 0)),
                  pl.BlockSpec((1, D_ATT), lambda bi: (0, 0))],
        out_specs=pl.BlockSpec((None, n_q, D_ATT), lambda bi: (bi, 0, 0)),
        scratch_shapes=[pltpu.VMEM((8, D_ATT), F32)],
        compiler_params=pltpu.CompilerParams(
            dimension_semantics=("parallel",), vmem_limit_bytes=VMEM_LIMIT_BYTES),
        name="attn_sample",
    )(proj, proj, proj, cache_kt, cache_vt, lb_c, lb_n, g_attn.reshape(1, D_ATT))
    return out.reshape(b * n_q, D_ATT)


def _sample_log_bias(rpb, n_q, wb):
    reach = max(w for w, _ in DILATED_PATTERNS) + 2 * 8
    assert wb + n_q <= reach
    terms = []
    for window, dil in DILATED_PATTERNS:
        span = window // dil
        table = rpb[_rel_bucket(jnp.arange(span + 1) * dil)].T.astype(F32)
        gaps = jnp.full((H_ATT, span + 1, dil - 1), -jnp.inf, F32)
        by_dist = jnp.concatenate([table[:, :, None], gaps], axis=2).reshape(H_ATT, (span + 1) * dil)
        by_dist = by_dist[:, :reach]
        terms.append(jnp.pad(by_dist, ((0, 0), (0, reach - by_dist.shape[1])), constant_values=-jnp.inf))
    lb = jax.nn.logsumexp(jnp.stack(terms), axis=0)
    neg = jnp.full((H_ATT,), -jnp.inf, F32)
    rows_c, rows_n = [], []
    for r in range(8):
        n = r % n_q
        rows_c.append(lb[:, n + 1:n + 1 + wb][:, ::-1])
        rows_n.append(jnp.stack([lb[:, n - m] if (m < n_q and m <= n) else neg for m in range(8)], axis=1))
    return jnp.stack(rows_c, axis=1), jnp.stack(rows_n, axis=1)


TN_DIMS = (((0,), (0,)), ((), ()))
GROUP_W = D_SSM // SSM_GROUPS


def _split_bf16(v, pieces):
    out = []
    for _ in range(pieces):
        hi = v.astype(BF16)
        out.append(hi)
        v = v - hi.astype(F32)
    return out


def _dot_split(v, rhs_bf16, pieces, dims=None):
    acc = None
    for piece in _split_bf16(v, pieces):
        if dims is None:
            t = jnp.dot(piece, rhs_bf16, preferred_element_type=F32)
        else:
            t = lax.dot_general(piece, rhs_bf16, dims, preferred_element_type=F32)
        acc = t if acc is None else acc + t
    return acc


def _softplus(v):
    return jnp.maximum(v, 0.0) + jnp.log1p(jnp.exp(-jnp.abs(v)))


def _causal_conv(ext_ref, rows, cw_ref, cb_ref):
    conv = cb_ref[...]
    for k in range(CONV_W):
        conv = conv + cw_ref[CONV_W - 1 - k:CONV_W - k, :] * ext_ref[8 - k:8 - k + rows, :]
    return _silu(conv)


def _gated_group_norm(y, z, g):
    gated = y * _silu(z)
    parts = []
    for grp in range(SSM_GROUPS):
        seg = gated[:, grp * GROUP_W:(grp + 1) * GROUP_W]
        parts.append(seg * lax.rsqrt(jnp.mean(seg * seg, axis=-1, keepdims=True) + NORM_EPS))
    return jnp.concatenate(parts, axis=1) * g


def _ssd_prompt_kernel(z_ref, xd_ref, cw_ref, cb_ref, dtb_ref, alog_ref, dsk_ref, g_ref,
                       y_ref, hout_ref, ht_ref, ext_ref):
    c = pl.program_id(1)
    q = SSD_CHUNK

    @pl.when(c == 0)
    def _():
        ht_ref[...] = jnp.zeros_like(ht_ref)
        ext_ref[0:8, :] = jnp.zeros((8, CONV_DIM), F32)

    x_raw = xd_ref[:, 0:CONV_DIM]
    ext_ref[8:8 + q, :] = x_raw
    act = _causal_conv(ext_ref, q, cw_ref, cb_ref)
    ext_ref[0:8, :] = x_raw[q - 8:q, :]
    xs = act[:, 0:D_SSM]
    bm = act[:, D_SSM:D_SSM + SSM_GROUPS * D_STATE]
    cm = act[:, D_SSM + SSM_GROUPS * D_STATE:CONV_DIM]

    dt = _softplus(xd_ref[:, CONV_DIM:CONV_DIM + 128] + dtb_ref[...])
    da = dt * (-jnp.exp(alog_ref[...]))
    ti = lax.broadcasted_iota(jnp.int32, (q, q), 0)
    si = lax.broadcasted_iota(jnp.int32, (q, q), 1)
    causal = ti >= si
    a_cum = None
    for piece in _split_bf16(da, 3):
        t = jnp.dot(causal.astype(BF16), piece, preferred_element_type=F32)
        a_cum = t if a_cum is None else a_cum + t
    a_cum_t = a_cum.T
    expand = _head_lane_mask(128).astype(BF16)
    eac = jnp.exp(a_cum)
    dt_x = _dot_split(dt, expand, 2)
    eac_x = _dot_split(eac, expand, 2)
    dend_x = _dot_split(jnp.exp(a_cum[q - 1:q, :] - a_cum), expand, 2)
    xdt = xs * dt_x
    xdd = (xdt * dend_x).astype(BF16)
    xdt_b = xdt.astype(BF16)
    lane = lax.broadcasted_iota(jnp.int32, (q, 2 * SSM_HEAD_DIM), 1)
    lo = lane < SSM_HEAD_DIM
    heads_per_group = H_SSM // SSM_GROUPS
    for grp in range(SSM_GROUPS):
        gs = slice(grp * GROUP_W, (grp + 1) * GROUP_W)
        bm_g = bm[:, grp * D_STATE:(grp + 1) * D_STATE]
        cm_b = cm[:, grp * D_STATE:(grp + 1) * D_STATE].astype(BF16)
        cb_mat = lax.dot_general(cm_b, bm_g.astype(BF16), NT_DIMS, preferred_element_type=F32)
        h_in = ht_ref[:, gs]
        y_off = jnp.dot(cm_b, h_in.astype(BF16), preferred_element_type=F32) * eac_x[:, gs]
        states = jnp.dot(bm_g.T.astype(BF16), xdd[:, gs], preferred_element_type=F32)
        ht_ref[:, gs] = h_in * eac_x[q - 1:q, gs] + states
        for hp in range(heads_per_group // 2):
            h0 = grp * heads_per_group + 2 * hp
            ps = slice(h0 * SSM_HEAD_DIM, (h0 + 2) * SSM_HEAD_DIM)
            x_pair = xdt_b[:, ps]
            halves = []
            for half in range(2):
                h = h0 + half
                seg = a_cum[:, h:h + 1] - a_cum_t[h:h + 1, :]
                decay = jnp.exp(jnp.where(causal, seg, -jnp.inf))
                halves.append(jnp.dot((cb_mat * decay).astype(BF16), x_pair, preferred_element_type=F32))
            y_diag = jnp.where(lo, halves[0], halves[1])
            off = slice(2 * hp * SSM_HEAD_DIM, (2 * hp + 2) * SSM_HEAD_DIM)
            y_ref[:, ps] = y_diag + y_off[:, off] + dsk_ref[:, ps] * xs[:, ps]
    y_ref[...] = _gated_group_norm(y_ref[...], z_ref[...], g_ref[...])

    @pl.when(c == pl.num_programs(1) - 1)
    def _():
        hout_ref[...] = ht_ref[...].T


def _ssd_prompt_call(proj, sp):
    b, s_len, _ = proj.shape
    q = SSD_CHUNK
    const = lambda shape: pl.BlockSpec(shape, lambda bi, c: (0,) * len(shape))
    y, h_last = pl.pallas_call(
        _ssd_prompt_kernel,
        out_shape=(jax.ShapeDtypeStruct((b, s_len, D_SSM), F32),
                   jax.ShapeDtypeStruct((b, D_SSM, D_STATE), F32)),
        grid=(b, s_len // q),
        in_specs=[pl.BlockSpec((None, q, D_SSM), lambda bi, c: (bi, c, 3)),
                  pl.BlockSpec((None, q, 2048), lambda bi, c: (bi, c, 2)),
                  const((CONV_W, CONV_DIM)), const((1, CONV_DIM)), const((1, 128)), const((1, 128)),
                  const((1, D_SSM)), const((1, D_SSM))],
        out_specs=(pl.BlockSpec((None, q, D_SSM), lambda bi, c: (bi, c, 0)),
                   pl.BlockSpec((None, D_SSM, D_STATE), lambda bi, c: (bi, 0, 0))),
        scratch_shapes=[pltpu.VMEM((D_STATE, D_SSM), F32), pltpu.VMEM((8 + q, CONV_DIM), F32)],
        compiler_params=pltpu.CompilerParams(
            dimension_semantics=("parallel", "arbitrary"), vmem_limit_bytes=VMEM_LIMIT_BYTES),
        name="ssd_prompt",
    )(proj, proj, sp['conv_w'], sp['conv_b'], sp['dt_bias'], sp['a_log'], sp['d_skip_x'], sp['g_ssm_out'])
    return y.reshape(b * s_len, D_SSM), h_last.reshape(b, H_SSM, SSM_HEAD_DIM, D_STATE)


def _ssd_sample_kernel(z_ref, xd_ref, buf_ref, h0_ref, cw_ref, cb_ref, dtb_ref, alog_ref, dsk_ref, g_ref,
                       y_ref, hout_ref, ext_ref):
    n_seq, n_tok, _ = z_ref.shape
    expand = _head_lane_mask(128).astype(BF16)
    ti = lax.broadcasted_iota(jnp.int32, (8, 8), 0)
    si = lax.broadcasted_iota(jnp.int32, (8, 8), 1)
    causal = (ti >= si) & (si < n_tok)
    real = lax.broadcasted_iota(jnp.int32, (8, 1), 0) < n_tok
    ones = jnp.ones((8, D_STATE), BF16)
    ext_ref[...] = jnp.zeros_like(ext_ref)

    def body(j, carry):
        ext_ref[8 - (CONV_W - 1):8, :] = buf_ref[j]
        ext_ref[8:8 + n_tok, :] = xd_ref[j][:, 0:CONV_DIM]
        act = _causal_conv(ext_ref, 8, cw_ref, cb_ref)
        xs = act[:, 0:D_SSM]
        bm = act[:, D_SSM:D_SSM + SSM_GROUPS * D_STATE]
        cm = act[:, D_SSM + SSM_GROUPS * D_STATE:CONV_DIM]
        pad = jnp.zeros((8 - n_tok, 128), F32)
        dt = _softplus(jnp.concatenate([xd_ref[j][:, CONV_DIM:CONV_DIM + 128], pad], axis=0) + dtb_ref[...])
        da = jnp.where(real, dt * (-jnp.exp(alog_ref[...])), 0.0)
        a_cum = None
        for piece in _split_bf16(da, 3):
            t = jnp.dot((ti >= si).astype(BF16), piece, preferred_element_type=F32)
            a_cum = t if a_cum is None else a_cum + t
        a_last = a_cum[n_tok - 1:n_tok, :]
        acum_x = _dot_split(a_cum, expand, 3)
        eac_x = jnp.exp(acum_x)
        dt_x = _dot_split(dt, expand, 2)
        xdt = jnp.where(real, xs * dt_x, 0.0)
        xdd = (xdt * jnp.exp(acum_x[n_tok - 1:n_tok, :] - acum_x)).astype(BF16)
        last_x = jnp.where(lax.broadcasted_iota(jnp.int32, (8, 1), 0) == n_tok - 1, eac_x, 0.0)
        keep = _dot_split(last_x, ones, 2, dims=TN_DIMS)
        y_parts, new_state = [], []
        for grp in range(SSM_GROUPS):
            gs = slice(grp * GROUP_W, (grp + 1) * GROUP_W)
            bm_b = bm[:, grp * D_STATE:(grp + 1) * D_STATE].astype(BF16)
            cm_b = cm[:, grp * D_STATE:(grp + 1) * D_STATE].astype(BF16)
            cb_mat = jnp.where(causal, lax.dot_general(cm_b, bm_b, NT_DIMS, preferred_element_type=F32), 0.0)
            h0_g = h0_ref[j, gs, :]
            y_g = lax.dot_general(cm_b, h0_g.astype(BF16), NT_DIMS, preferred_element_type=F32) * eac_x[:, gs]
            for s in range(n_tok):
                coef = cb_mat[:, s:s + 1] * jnp.exp(jnp.where(ti[:, 0:1] >= s, acum_x[:, gs] - acum_x[s:s + 1, gs], -jnp.inf))
                y_g = y_g + coef * xdt[s:s + 1, gs]
            y_parts.append(y_g)
            states = lax.dot_general(xdd[:, gs], bm_b, TN_DIMS, preferred_element_type=F32)
            hout_ref[j, gs, :] = h0_g * keep[gs, :] + states
        y = jnp.concatenate(y_parts, axis=1) + dsk_ref[...] * xs
        zj = jnp.concatenate([z_ref[j], jnp.zeros((8 - n_tok, D_SSM), F32)], axis=0)
        y_ref[j] = _gated_group_norm(y, zj, g_ref[...])[0:n_tok, :]
        return carry

    lax.fori_loop(0, n_seq, body, 0)


def _ssd_sample_call(proj, conv_buf, h0, sp):
    b, n_tok, _ = proj.shape
    nseq = 8
    const = lambda shape: pl.BlockSpec(shape, lambda i: (0,) * len(shape))
    y, h_last = pl.pallas_call(
        _ssd_sample_kernel,
        out_shape=(jax.ShapeDtypeStruct((b, n_tok, D_SSM), F32),
                   jax.ShapeDtypeStruct((b, D_SSM, D_STATE), F32)),
        grid=(b // nseq,),
        in_specs=[pl.BlockSpec((nseq, n_tok, D_SSM), lambda i: (i, 0, 3)),
                  pl.BlockSpec((nseq, n_tok, 2048), lambda i: (i, 0, 2)),
                  pl.BlockSpec((nseq, CONV_W - 1, CONV_DIM), lambda i: (i, 0, 0)),
                  pl.BlockSpec((nseq, D_SSM, D_STATE), lambda i: (i, 0, 0)),
                  const((CONV_W, CONV_DIM)), const((1, CONV_DIM)), const((1, 128)), const((1, 128)),
                  const((1, D_SSM)), const((1, D_SSM))],
        out_specs=(pl.BlockSpec((nseq, n_tok, D_SSM), lambda i: (i, 0, 0)),
                   pl.BlockSpec((nseq, D_SSM, D_STATE), lambda i: (i, 0, 0))),
        scratch_shapes=[pltpu.VMEM((16, CONV_DIM), F32)],
        compiler_params=pltpu.CompilerParams(
            dimension_semantics=("parallel",), vmem_limit_bytes=VMEM_LIMIT_BYTES),
        name="ssd_sample",
    )(proj, proj, conv_buf, h0.reshape(b, D_SSM, D_STATE), sp['conv_w'], sp['conv_b'], sp['dt_bias'],
      sp['a_log'], sp['d_skip_x'], sp['g_ssm_out'])
    return y.reshape(b * n_tok, D_SSM), h_last.reshape(b, H_SSM, SSM_HEAD_DIM, D_STATE)


def _layer(x, mods, p, attend, conv_buf, h0, *, tm, g_final):
    b, L, _ = x.shape
    m = b * L
    sh1, sc1, g1, sh2, sc2, g2, sh3, sc3, g3 = mods
    x2 = x.reshape(m, D_MODEL)
    kw = dict(tm=tm, rows_per_mod=L)
    x2 = _ffn_call(x2, p['g_ffn1'], sh1, sc1, g1, p['w_gate1'], p['w_up1'], p['w_down1'],
                   g_final, final_norm=False, **kw)
    proj, qkv = _inproj_call(x2, p['g_mix'], sh2, sc2, p['w_in_pad'], **kw)
    proj = proj.reshape(b, L, D_IN_PAD)
    k = proj[..., D_ATT:2 * D_ATT].reshape(b, L, H_ATT, HEAD_DIM)
    v = proj[..., 2 * D_ATT:3 * D_ATT].reshape(b, L, H_ATT, HEAD_DIM)
    o = 3 * D_ATT + D_SSM
    conv_new = proj[:, L - (CONV_W - 1):, o:o + CONV_DIM]
    att_parts = attend(proj, qkv.reshape(b, L, -1))
    if h0 is None:
        ssd_o, h_last = _ssd_prompt_call(proj, p)
    else:
        ssd_o, h_last = _ssd_sample_call(proj, conv_buf, h0, p)
    x2 = _outproj_call(x2, att_parts, ssd_o, g2, p['w_out_att'], p['w_out_ssm'],
                       p['g_attn_out'], tm=min(tm, 256), rows_per_mod=L)
    x2 = _ffn_call(x2, p['g_ffn2'], sh3, sc3, g3, p['w_gate2'], p['w_up2'], p['w_down2'],
                   g_final, final_norm=True, **kw)
    return x2.reshape(b, L, D_MODEL), k, v, h_last, conv_new


def kernel(x_prompt, x_sample, cache_k, cache_v, state_ssm, state_conv, c_prompt, c_sample,
           rpb_table, w_ada, b_ada, g_ffn1, w_gate1, w_up1, w_down1, g_mix, w_in, conv_w, conv_b,
           dt_bias, a_log, d_skip, g_ssm_out, g_attn_out, w_out, g_ffn2, w_gate2, w_up2, w_down2, g_final):
    bp, sp, _ = x_prompt.shape
    bs, ls, _ = x_sample.shape
    wbp = min(WINDOW_MAX, sp)
    layer = 0
    w_in_pad = jnp.pad(w_in[layer].astype(BF16), ((0, 0), (0, D_IN_PAD - D_IN)))
    p = {'g_ffn1': g_ffn1[layer], 'w_gate1': w_gate1[layer].astype(BF16), 'w_up1': w_up1[layer].astype(BF16),
         'w_down1': w_down1[layer].astype(BF16), 'g_mix': g_mix[layer], 'w_in_pad': w_in_pad,
         'conv_w': conv_w[layer], 'conv_b': conv_b[layer].reshape(1, CONV_DIM),
         'dt_bias': jnp.pad(dt_bias[layer], (0, 128 - H_SSM)).reshape(1, 128),
         'a_log': jnp.pad(a_log[layer], (0, 128 - H_SSM)).reshape(1, 128),
         'd_skip_x': jnp.repeat(d_skip[layer], SSM_HEAD_DIM).reshape(1, D_SSM),
         'g_ssm_out': g_ssm_out[layer].reshape(1, D_SSM), 'g_attn_out': g_attn_out[layer],
         'w_out_att': w_out[layer, :D_ATT].astype(BF16), 'w_out_ssm': w_out[layer, D_ATT:].astype(BF16),
         'g_ffn2': g_ffn2[layer], 'w_gate2': w_gate2[layer].astype(BF16), 'w_up2': w_up2[layer].astype(BF16),
         'w_down2': w_down2[layer].astype(BF16)}

    n_c = bp + bs
    c_rows = -(-n_c // 16) * 16
    c_all = jnp.pad(jnp.concatenate([c_prompt, c_sample], axis=0), ((0, c_rows - n_c), (0, 0)))
    mod = _mod_call(c_all, w_ada[layer], b_ada[layer])
    mods_p = [mod[:bp, i * D_MODEL:(i + 1) * D_MODEL].reshape(bp, 1, D_MODEL) for i in range(N_MOD)]
    mods_s = [jnp.repeat(mod[bp:n_c, i * D_MODEL:(i + 1) * D_MODEL], ls, axis=0) for i in range(N_MOD)]

    def attend_prompt(proj, qkv):
        outs, lses = [], []
        for window, dil in DILATED_PATTERNS:
            o, lse = _attn_prompt_call(qkv, _prompt_bias(rpb_table, window, dil), dil)
            outs.append(o)
            lses.append(lse)
        return outs + lses

    lb_c, lb_n = _sample_log_bias(rpb_table, ls, cache_k.shape[2])
    cache_kt = jnp.transpose(cache_k[layer], (0, 2, 3, 1))
    cache_vt = jnp.transpose(cache_v[layer], (0, 2, 3, 1))

    def attend_sample(proj, qkv):
        return [_attn_sample_call(proj, cache_kt, cache_vt, lb_c, lb_n, p['g_attn_out'])]

    yp, kp, vp, hp, cp = _layer(x_prompt, mods_p, p, attend_prompt, None, None, tm=1024, g_final=g_final)
    ys, kn, vn, hs, cs = _layer(x_sample, mods_s, p, attend_sample, state_conv[layer], state_ssm[layer],
                                tm=bs * ls, g_final=g_final)
    return (yp, ys, kp[None, :, -wbp:], vp[None, :, -wbp:], kn[None], vn[None],
            hp[None], hs[None], cp[None], cs[None])
```

```python
import functools
import math

import jax
import jax.numpy as jnp
from jax import lax
from jax.experimental import pallas as pl
from jax.experimental.pallas import tpu as pltpu

D_MODEL = 2048
HEAD_DIM = 64
D_ATT = 1024
D_SSM = 1024
H_ATT = 16
SSM_HEAD_DIM = 64
H_SSM = 16
SSM_GROUPS = 2
D_STATE = 128
CONV_W = 4
SSD_CHUNK = 128
CONV_DIM = D_SSM + 2 * SSM_GROUPS * D_STATE
D_FF = 5632
DILATED_PATTERNS = ((128, 1), (512, 4), (2048, 16))
WINDOW_MAX = 2048
ATT_BLOCK = 128
RPB_BUCKETS = 32
RPB_MAX_DIST = 2048
NORM_EPS = 1e-6
N_MOD = 9
PROJ_SIZES = (D_ATT, D_ATT, D_ATT, D_SSM, CONV_DIM, H_SSM)
D_IN = sum(PROJ_SIZES)
D_IN_PAD = 6144

VMEM_LIMIT_BYTES = 56 * 1024 * 1024

BF16 = jnp.bfloat16
F32 = jnp.float32


def _silu(v):
    return v * jax.nn.sigmoid(v)


def _norm_mod(x, g, shift, scale):
    y = x * lax.rsqrt(jnp.mean(x * x, axis=-1, keepdims=True) + NORM_EPS)
    return (y * g) * (1.0 + scale) + shift


def _mod_kernel(c_ref, w_ref, b_ref, o_ref, a_ref):
    @pl.when(pl.program_id(0) == 0)
    def _():
        a_ref[...] = _silu(c_ref[...]).astype(BF16)

    o_ref[...] = jnp.dot(a_ref[...], w_ref[...].astype(BF16),
                         preferred_element_type=F32) + b_ref[...]


def _mod_call(c_all, w_ada, b_ada):
    rows = c_all.shape[0]
    n = w_ada.shape[1]
    tn = 1024
    return pl.pallas_call(
        _mod_kernel,
        out_shape=jax.ShapeDtypeStruct((rows, n), F32),
        grid=(n // tn,),
        in_specs=[
            pl.BlockSpec((rows, D_MODEL), lambda j: (0, 0)),
            pl.BlockSpec((D_MODEL, tn), lambda j: (0, j)),
            pl.BlockSpec((1, tn), lambda j: (0, j)),
        ],
        out_specs=pl.BlockSpec((rows, tn), lambda j: (0, j)),
        scratch_shapes=[pltpu.VMEM((rows, D_MODEL), BF16)],
        compiler_params=pltpu.CompilerParams(
            dimension_semantics=("arbitrary",), vmem_limit_bytes=VMEM_LIMIT_BYTES),
        name="adaln_mod",
    )(c_all, w_ada, b_ada.reshape(1, n))


def _ffn_kernel(x_ref, g_ref, sh_ref, sc_ref, gt_ref, wg_ref, wu_ref, wd_ref, gf_ref,
                o_ref, h_ref, *, final_norm):
    j = pl.program_id(1)

    @pl.when(j == 0)
    def _():
        h_ref[...] = _norm_mod(x_ref[...], g_ref[...], sh_ref[...], sc_ref[...]).astype(BF16)
        o_ref[...] = jnp.zeros_like(o_ref)

    h = h_ref[...]
    a = jnp.dot(h, wg_ref[...], preferred_element_type=F32)
    b = jnp.dot(h, wu_ref[...], preferred_element_type=F32)
    s = (_silu(a) * b).astype(BF16)
    o_ref[...] += jnp.dot(s, wd_ref[...], preferred_element_type=F32)

    @pl.when(j == pl.num_programs(1) - 1)
    def _():
        y = x_ref[...] + 0.5 * gt_ref[...] * o_ref[...]
        if final_norm:
            y = y * lax.rsqrt(jnp.mean(y * y, axis=-1, keepdims=True) + NORM_EPS) * gf_ref[...]
        o_ref[...] = y


def _mod_spec(mod, tm, rows_per_mod):
    if mod.ndim == 3:
        tiles_per_row = rows_per_mod // tm
        return pl.BlockSpec((None, 1, D_MODEL), lambda i, j: (i // tiles_per_row, 0, 0))
    return pl.BlockSpec((tm, D_MODEL), lambda i, j: (i, 0))


def _ffn_call(x, g, sh, sc, gt, wg, wu, wd, g_final, *, tm, rows_per_mod, final_norm):
    m = x.shape[0]
    tf = 512
    row = lambda i, j: (i, 0)
    const = lambda i, j: (0, 0)
    return pl.pallas_call(
        functools.partial(_ffn_kernel, final_norm=final_norm),
        out_shape=jax.ShapeDtypeStruct((m, D_MODEL), F32),
        grid=(m // tm, D_FF // tf),
        in_specs=[
            pl.BlockSpec((tm, D_MODEL), row, pipeline_mode=pl.Buffered(1)),
            pl.BlockSpec((1, D_MODEL), const),
            _mod_spec(sh, tm, rows_per_mod),
            _mod_spec(sc, tm, rows_per_mod),
            _mod_spec(gt, tm, rows_per_mod),
            pl.BlockSpec((D_MODEL, tf), lambda i, j: (0, j)),
            pl.BlockSpec((D_MODEL, tf), lambda i, j: (0, j)),
            pl.BlockSpec((tf, D_MODEL), lambda i, j: (j, 0)),
            pl.BlockSpec((1, D_MODEL), const),
        ],
        out_specs=pl.BlockSpec((tm, D_MODEL), row),
        scratch_shapes=[pltpu.VMEM((tm, D_MODEL), BF16)],
        compiler_params=pltpu.CompilerParams(
            dimension_semantics=("parallel", "arbitrary"), vmem_limit_bytes=VMEM_LIMIT_BYTES),
        name="ffn_final" if final_norm else "ffn",
    )(x, g.reshape(1, D_MODEL), sh, sc, gt, wg, wu, wd, g_final.reshape(1, D_MODEL))


QKV_TILES = 3


def _inproj_kernel(x_ref, g_ref, sh_ref, sc_ref, w_ref, o_ref, *rest, dilations):
    qkv_refs, h_ref, lanes_ref = rest[:-2], rest[-2], rest[-1]
    j = pl.program_id(1)
    tm = o_ref.shape[0]

    @pl.when(j == 0)
    def _():
        h_ref[...] = _norm_mod(x_ref[...], g_ref[...], sh_ref[...], sc_ref[...]).astype(BF16)

    res = jnp.dot(h_ref[...], w_ref[...], preferred_element_type=F32)
    o_ref[...] = res

    @pl.when(j < QKV_TILES)
    def _():
        n_chunks = D_ATT // 128
        if any(d > 1 for d in dilations):
            for c in range(n_chunks):
                lanes_ref[c] = res[:, c * 128:(c + 1) * 128]
        for d, ref in zip(dilations, qkv_refs):
            if d == 1:
                ref[...] = res.astype(BF16)
                continue
            for r in range(d):
                for c in range(n_chunks):
                    rows = lanes_ref[c, pl.ds(r, tm // d, stride=d), :]
                    ref[:, r * D_ATT + c * 128:r * D_ATT + (c + 1) * 128] = rows.astype(BF16)


def _inproj_call(x, g, sh, sc, w_pad, *, tm, rows_per_mod, dilations):
    m = x.shape[0]
    tn = D_ATT
    qkv_j = lambda j: jnp.minimum(j, QKV_TILES - 1)
    return pl.pallas_call(
        functools.partial(_inproj_kernel, dilations=dilations),
        out_shape=(jax.ShapeDtypeStruct((m, D_IN_PAD), F32),)
        + tuple(jax.ShapeDtypeStruct((m // d, QKV_TILES * d * tn), BF16) for d in dilations),
        grid=(m // tm, D_IN_PAD // tn),
        in_specs=[
            pl.BlockSpec((tm, D_MODEL), lambda i, j: (i, 0), pipeline_mode=pl.Buffered(1)),
            pl.BlockSpec((1, D_MODEL), lambda i, j: (0, 0)),
            _mod_spec(sh, tm, rows_per_mod),
            _mod_spec(sc, tm, rows_per_mod),
            pl.BlockSpec((D_MODEL, tn), lambda i, j: (0, j)),
        ],
        out_specs=(pl.BlockSpec((tm, tn), lambda i, j: (i, j)),)
        + tuple(pl.BlockSpec((tm // d, d * tn), lambda i, j: (i, qkv_j(j))) for d in dilations),
        scratch_shapes=[pltpu.VMEM((tm, D_MODEL), BF16), pltpu.VMEM((tn // 128, tm, 128), F32)],
        compiler_params=pltpu.CompilerParams(
            dimension_semantics=("parallel", "arbitrary"), vmem_limit_bytes=VMEM_LIMIT_BYTES),
        name="in_proj",
    )(x, g.reshape(1, D_MODEL), sh, sc, w_pad)


def _outproj_kernel(*refs, dilations):
    x_ref = refs[0]
    n_pat = len(dilations)
    if n_pat:
        o_refs = refs[1:1 + n_pat]
        l_refs = refs[1 + n_pat:1 + 2 * n_pat]
        ga_ref, ssd_ref, gt_ref, wa_ref, ws_ref, out_ref, o_scr, l_scr = refs[1 + 2 * n_pat:]
        tm = x_ref.shape[0]
        n_chunks = D_ATT // 128
        outs, lses = [], []
        for p, (d, o_ref, l_ref) in enumerate(zip(dilations, o_refs, l_refs)):
            if d == 1:
                outs.append(o_ref[...].astype(F32))
                lses.append(l_ref[...])
                continue
            for r in range(d):
                l_scr[p, pl.ds(r, tm // d, stride=d), :] = l_ref[:, r * 128:(r + 1) * 128]
                for c in range(n_chunks):
                    lanes = slice(r * D_ATT + c * 128, r * D_ATT + (c + 1) * 128)
                    o_scr[p, c, pl.ds(r, tm // d, stride=d), :] = o_ref[:, lanes].astype(F32)
            outs.append(jnp.concatenate([o_scr[p, c] for c in range(n_chunks)], axis=1))
            lses.append(l_scr[p])
        mx = functools.reduce(jnp.maximum, lses)
        ws = [jnp.exp(v - mx) for v in lses]
        wsum = functools.reduce(lambda a, b: a + b, ws)
        expand = _head_lane_mask(128).astype(BF16)
        att = None
        for w, o in zip(ws, outs):
            wn = w / wsum
            w_hi = wn.astype(BF16)
            w_lo = (wn - w_hi.astype(F32)).astype(BF16)
            wx = (jnp.dot(w_hi, expand, preferred_element_type=F32)
                  + jnp.dot(w_lo, expand, preferred_element_type=F32))
            att = wx * o if att is None else att + wx * o
        att = att * lax.rsqrt(jnp.mean(att * att, axis=-1, keepdims=True) + NORM_EPS) * ga_ref[...]
    else:
        att_ref, ssd_ref, gt_ref, wa_ref, ws_ref, out_ref = refs[1:]
        att = att_ref[...]
    mix = jnp.dot(att.astype(BF16), wa_ref[...], preferred_element_type=F32)
    mix += jnp.dot(ssd_ref[...].astype(BF16), ws_ref[...], preferred_element_type=F32)
    out_ref[...] = x_ref[...] + gt_ref[...] * mix


def _outproj_call(x, att_parts, ssd, gt, w_att, w_ssd, g_attn, *, tm, rows_per_mod, dilations):
    m = x.shape[0]
    n_pat = len(dilations)
    if gt.ndim == 3:
        tiles_per_row = rows_per_mod // tm
        gt_spec = pl.BlockSpec((None, 1, D_MODEL), lambda i: (i // tiles_per_row, 0, 0))
    else:
        gt_spec = pl.BlockSpec((tm, D_MODEL), lambda i: (i, 0))
    row = lambda width, d=1: pl.BlockSpec((tm // d, d * width), lambda i: (i, 0))
    scratch = []
    if n_pat:
        att_specs = ([row(D_ATT, d) for d in dilations] + [row(128, d) for d in dilations]
                     + [pl.BlockSpec((1, D_ATT), lambda i: (0, 0))])
        att_args = list(att_parts) + [g_attn.reshape(1, D_ATT)]
        scratch = [pltpu.VMEM((n_pat, D_ATT // 128, tm, 128), F32), pltpu.VMEM((n_pat, tm, 128), F32)]
    else:
        att_specs = [row(D_ATT)]
        att_args = list(att_parts)
    weight = lambda k: pl.BlockSpec((k, D_MODEL), lambda i: (0, 0), pipeline_mode=pl.Buffered(1))
    return pl.pallas_call(
        functools.partial(_outproj_kernel, dilations=dilations),
        out_shape=jax.ShapeDtypeStruct((m, D_MODEL), F32),
        grid=(m // tm,),
        in_specs=[row(D_MODEL)] + att_specs + [row(D_SSM), gt_spec, weight(D_ATT), weight(D_SSM)],
        out_specs=row(D_MODEL),
        scratch_shapes=scratch,
        compiler_params=pltpu.CompilerParams(
            dimension_semantics=("parallel",), vmem_limit_bytes=VMEM_LIMIT_BYTES),
        name="out_proj_merge" if n_pat else "out_proj",
    )(x, *att_args, ssd, gt, w_att, w_ssd)


def _rel_bucket(dist):
    max_exact = RPB_BUCKETS // 2
    df = jnp.maximum(dist, 1).astype(jnp.float32)
    large = max_exact + (jnp.log(df / max_exact) / math.log(RPB_MAX_DIST / max_exact)
                         * (RPB_BUCKETS - max_exact)).astype(jnp.int32)
    large = jnp.minimum(large, RPB_BUCKETS - 1)
    return jnp.where(dist < max_exact, dist, large)


def _head_lane_mask(rows=H_ATT):
    r = lax.broadcasted_iota(jnp.int32, (rows, D_ATT), 0)
    c = lax.broadcasted_iota(jnp.int32, (rows, D_ATT), 1)
    return (c // HEAD_DIM == r).astype(F32)


NT_DIMS = (((1,), (1,)), ((), ()))


def _attn_prompt_kernel(q_ref, kp_ref, kc_ref, vp_ref, vc_ref, bias_ref, o_ref, lse_ref):
    nb = pl.program_id(2)
    blk = ATT_BLOCK
    col = lax.broadcasted_iota(jnp.int32, (blk, 2 * blk), 1)
    has_prev = (col >= blk) | (nb > 0)
    lane = lax.broadcasted_iota(jnp.int32, (blk, 2 * HEAD_DIM), 1)
    lo = lane < HEAD_DIM
    lse_lane = lax.broadcasted_iota(jnp.int32, (blk, 128), 1)
    lse_tile = jnp.zeros((blk, 128), F32)
    scale = HEAD_DIM ** -0.5
    for hp in range(H_ATT // 2):
        sl = slice(hp * 2 * HEAD_DIM, (hp + 1) * 2 * HEAD_DIM)
        q2 = q_ref[:, sl] * scale
        k2 = jnp.concatenate([kp_ref[:, sl], kc_ref[:, sl]], axis=0)
        v2 = jnp.concatenate([vp_ref[:, sl], vc_ref[:, sl]], axis=0)
        outs = []
        for half in range(2):
            h = 2 * hp + half
            qh = jnp.where(lo if half == 0 else ~lo, q2, jnp.zeros_like(q2))
            s = lax.dot_general(qh, k2, NT_DIMS, preferred_element_type=F32) + bias_ref[h]
            s = jnp.where(has_prev, s, -jnp.inf)
            m = jnp.max(s, axis=-1, keepdims=True)
            p = jnp.exp(s - m)
            l = jnp.sum(p, axis=-1, keepdims=True)
            pv = jnp.dot(p.astype(BF16), v2, preferred_element_type=F32)
            outs.append(pv / l)
            lse_tile = jnp.where(lse_lane == h, m + jnp.log(l), lse_tile)
        o_ref[:, sl] = jnp.where(lo, outs[0], outs[1]).astype(o_ref.dtype)
    lse_ref[...] = lse_tile


def _attn_prompt_call(qkv, bias, dil, b):
    blk = ATT_BLOCK
    n_sub = qkv.shape[0] // b
    s_len = n_sub * dil
    nbk = n_sub // blk
    pv = qkv.reshape(b, n_sub, qkv.shape[1])
    cur = lambda t: (lambda bi, r, nb: (bi, nb, t * dil + r))
    prev = lambda t: (lambda bi, r, nb: (bi, jnp.maximum(nb - 1, 0), t * dil + r))
    spec = lambda f: pl.BlockSpec((None, blk, D_ATT), f)
    o, lse = pl.pallas_call(
        _attn_prompt_kernel,
        out_shape=(jax.ShapeDtypeStruct((b, n_sub, dil * D_ATT), BF16),
                   jax.ShapeDtypeStruct((b, n_sub, dil * 128), F32)),
        grid=(b, dil, nbk),
        in_specs=[spec(cur(0)), spec(prev(1)), spec(cur(1)), spec(prev(2)), spec(cur(2)),
                  pl.BlockSpec((H_ATT, blk, 2 * blk), lambda bi, r, nb: (0, 0, 0))],
        out_specs=(pl.BlockSpec((None, blk, D_ATT), lambda bi, r, nb: (bi, nb, r)),
                   pl.BlockSpec((None, blk, 128), lambda bi, r, nb: (bi, nb, r))),
        compiler_params=pltpu.CompilerParams(
            dimension_semantics=("parallel", "parallel", "arbitrary"),
            vmem_limit_bytes=VMEM_LIMIT_BYTES),
        name=f"attn_prompt_d{dil}",
    )(pv, pv, pv, pv, pv, bias)
    return o.reshape(b * n_sub, dil * D_ATT), lse.reshape(b * n_sub, dil * 128)


def _prompt_bias(rpb, window, dil):
    span = window // dil
    blk = ATT_BLOCK
    assert span == blk
    table = rpb[_rel_bucket(jnp.arange(span + 1) * dil)].T.astype(F32)
    fill = jnp.full((H_ATT, blk), -jnp.inf, F32)
    row = jnp.concatenate([fill, table[:, ::-1], fill], axis=1)
    wrapped = jnp.tile(row, (1, blk))[:, :blk * 3 * blk].reshape(H_ATT, blk, 3 * blk)
    return wrapped[:, :, blk:]


def _attn_sample_kernel(q_ref, kn_ref, vn_ref, kt_ref, vt_ref, lbc_ref, lbn_ref, g_ref, o_ref, acc_ref):
    n_q = q_ref.shape[0]
    pad = jnp.zeros((8 - n_q, D_ATT), F32)
    q8 = jnp.concatenate([q_ref[...] * (HEAD_DIM ** -0.5), pad], axis=0).astype(BF16)
    kn8 = jnp.concatenate([kn_ref[...], pad], axis=0).astype(BF16)
    vn8 = jnp.concatenate([vn_ref[...], pad], axis=0).astype(BF16)
    for h in range(H_ATT):
        sl = slice(h * HEAD_DIM, (h + 1) * HEAD_DIM)
        qh = q8[:, sl]
        t_c = jnp.dot(qh, kt_ref[h].astype(BF16), preferred_element_type=F32) + lbc_ref[h]
        t_n = lax.dot_general(qh, kn8[:, sl], NT_DIMS, preferred_element_type=F32) + lbn_ref[h]
        m = jnp.maximum(jnp.max(t_c, axis=-1, keepdims=True), jnp.max(t_n, axis=-1, keepdims=True))
        p_c = jnp.exp(t_c - m)
        p_n = jnp.exp(t_n - m)
        l = jnp.sum(p_c, axis=-1, keepdims=True) + jnp.sum(p_n, axis=-1, keepdims=True)
        acc = lax.dot_general(p_c.astype(BF16), vt_ref[h].astype(BF16), NT_DIMS, preferred_element_type=F32)
        acc += jnp.dot(p_n.astype(BF16), vn8[:, sl], preferred_element_type=F32)
        acc_ref[:, sl] = acc / l
    att = acc_ref[0:n_q, :]
    o_ref[...] = att * lax.rsqrt(jnp.mean(att * att, axis=-1, keepdims=True) + NORM_EPS) * g_ref[...]


def _attn_sample_call(proj, cache_kt, cache_vt, lb_c, lb_n, g_attn):
    b, n_q, _ = proj.shape
    wb = cache_kt.shape[3]
    qspec = lambda t: pl.BlockSpec((None, n_q, D_ATT), lambda bi: (bi, 0, t))
    cspec = pl.BlockSpec((None, H_ATT, HEAD_DIM, wb), lambda bi: (bi, 0, 0, 0))
    out = pl.pallas_call(
        _attn_sample_kernel,
        out_shape=jax.ShapeDtypeStruct((b, n_q, D_ATT), F32),
        grid=(b,),
        in_specs=[qspec(0), qspec(1), qspec(2), cspec, cspec,
                  pl.BlockSpec((H_ATT, 8, wb), lambda bi: (0, 0, 0)),
                  pl.BlockSpec((H_ATT, 8, 8), lambda bi: (0, 0, 0)),
                  pl.BlockSpec((1, D_ATT), lambda bi: (0, 0))],
        out_specs=pl.BlockSpec((None, n_q, D_ATT), lambda bi: (bi, 0, 0)),
        scratch_shapes=[pltpu.VMEM((8, D_ATT), F32)],
        compiler_params=pltpu.CompilerParams(
            dimension_semantics=("parallel",), vmem_limit_bytes=VMEM_LIMIT_BYTES),
        name="attn_sample",
    )(proj, proj, proj, cache_kt, cache_vt, lb_c, lb_n, g_attn.reshape(1, D_ATT))
    return out.reshape(b * n_q, D_ATT)


def _sample_log_bias(rpb, n_q, wb):
    reach = max(w for w, _ in DILATED_PATTERNS) + 2 * 8
    assert wb + n_q <= reach
    terms = []
    for window, dil in DILATED_PATTERNS:
        span = window // dil
        table = rpb[_rel_bucket(jnp.arange(span + 1) * dil)].T.astype(F32)
        gaps = jnp.full((H_ATT, span + 1, dil - 1), -jnp.inf, F32)
        by_dist = jnp.concatenate([table[:, :, None], gaps], axis=2).reshape(H_ATT, (span + 1) * dil)
        by_dist = by_dist[:, :reach]
        terms.append(jnp.pad(by_dist, ((0, 0), (0, reach - by_dist.shape[1])), constant_values=-jnp.inf))
    lb = jax.nn.logsumexp(jnp.stack(terms), axis=0)
    neg = jnp.full((H_ATT,), -jnp.inf, F32)
    rows_c, rows_n = [], []
    for r in range(8):
        n = r % n_q
        rows_c.append(lb[:, n + 1:n + 1 + wb][:, ::-1])
        rows_n.append(jnp.stack([lb[:, n - m] if (m < n_q and m <= n) else neg for m in range(8)], axis=1))
    return jnp.stack(rows_c, axis=1), jnp.stack(rows_n, axis=1)


TN_DIMS = (((0,), (0,)), ((), ()))
GROUP_W = D_SSM // SSM_GROUPS


def _split_bf16(v, pieces):
    out = []
    for _ in range(pieces):
        hi = v.astype(BF16)
        out.append(hi)
        v = v - hi.astype(F32)
    return out


def _dot_split(v, rhs_bf16, pieces, dims=None):
    acc = None
    for piece in _split_bf16(v, pieces):
        if dims is None:
            t = jnp.dot(piece, rhs_bf16, preferred_element_type=F32)
        else:
            t = lax.dot_general(piece, rhs_bf16, dims, preferred_element_type=F32)
        acc = t if acc is None else acc + t
    return acc


def _softplus(v):
    return jnp.maximum(v, 0.0) + jnp.log1p(jnp.exp(-jnp.abs(v)))


def _causal_conv(ext_ref, rows, cw_ref, cb_ref):
    conv = cb_ref[...]
    for k in range(CONV_W):
        conv = conv + cw_ref[CONV_W - 1 - k:CONV_W - k, :] * ext_ref[8 - k:8 - k + rows, :]
    return _silu(conv)


def _gated_group_norm(y, z, g):
    gated = y * _silu(z)
    parts = []
    for grp in range(SSM_GROUPS):
        seg = gated[:, grp * GROUP_W:(grp + 1) * GROUP_W]
        parts.append(seg * lax.rsqrt(jnp.mean(seg * seg, axis=-1, keepdims=True) + NORM_EPS))
    return jnp.concatenate(parts, axis=1) * g


def _ssd_prompt_kernel(z_ref, xd_ref, cw_ref, cb_ref, dtb_ref, alog_ref, dsk_ref, g_ref,
                       y_ref, hout_ref, ht_ref, ext_ref):
    c = pl.program_id(1)
    q = SSD_CHUNK

    @pl.when(c == 0)
    def _():
        ht_ref[...] = jnp.zeros_like(ht_ref)
        ext_ref[0:8, :] = jnp.zeros((8, CONV_DIM), F32)

    x_raw = xd_ref[:, 0:CONV_DIM]
    ext_ref[8:8 + q, :] = x_raw
    act = _causal_conv(ext_ref, q, cw_ref, cb_ref)
    ext_ref[0:8, :] = x_raw[q - 8:q, :]
    xs = act[:, 0:D_SSM]
    bm = act[:, D_SSM:D_SSM + SSM_GROUPS * D_STATE]
    cm = act[:, D_SSM + SSM_GROUPS * D_STATE:CONV_DIM]

    dt = _softplus(xd_ref[:, CONV_DIM:CONV_DIM + 128] + dtb_ref[...])
    da = dt * (-jnp.exp(alog_ref[...]))
    ti = lax.broadcasted_iota(jnp.int32, (q, q), 0)
    si = lax.broadcasted_iota(jnp.int32, (q, q), 1)
    causal = ti >= si
    a_cum = None
    for piece in _split_bf16(da, 3):
        t = jnp.dot(causal.astype(BF16), piece, preferred_element_type=F32)
        a_cum = t if a_cum is None else a_cum + t
    a_cum_t = a_cum.T
    expand = _head_lane_mask(128).astype(BF16)
    eac = jnp.exp(a_cum)
    dt_x = _dot_split(dt, expand, 2)
    eac_x = _dot_split(eac, expand, 2)
    dend_x = _dot_split(jnp.exp(a_cum[q - 1:q, :] - a_cum), expand, 2)
    xdt = xs * dt_x
    xdd = (xdt * dend_x).astype(BF16)
    xdt_b = xdt.astype(BF16)
    lane = lax.broadcasted_iota(jnp.int32, (q, 2 * SSM_HEAD_DIM), 1)
    lo = lane < SSM_HEAD_DIM
    heads_per_group = H_SSM // SSM_GROUPS
    for grp in range(SSM_GROUPS):
        gs = slice(grp * GROUP_W, (grp + 1) * GROUP_W)
        bm_g = bm[:, grp * D_STATE:(grp + 1) * D_STATE]
        cm_b = cm[:, grp * D_STATE:(grp + 1) * D_STATE].astype(BF16)
        cb_mat = lax.dot_general(cm_b, bm_g.astype(BF16), NT_DIMS, preferred_element_type=F32)
        h_in = ht_ref[:, gs]
        y_off = jnp.dot(cm_b, h_in.astype(BF16), preferred_element_type=F32) * eac_x[:, gs]
        states = jnp.dot(bm_g.T.astype(BF16), xdd[:, gs], preferred_element_type=F32)
        ht_ref[:, gs] = h_in * eac_x[q - 1:q, gs] + states
        for hp in range(heads_per_group // 2):
            h0 = grp * heads_per_group + 2 * hp
            ps = slice(h0 * SSM_HEAD_DIM, (h0 + 2) * SSM_HEAD_DIM)
            x_pair = xdt_b[:, ps]
            halves = []
            for half in range(2):
                h = h0 + half
                seg = a_cum[:, h:h + 1] - a_cum_t[h:h + 1, :]
                decay = jnp.exp(jnp.where(causal, seg, -jnp.inf))
                halves.append(jnp.dot((cb_mat * decay).astype(BF16), x_pair, preferred_element_type=F32))
            y_diag = jnp.where(lo, halves[0], halves[1])
            off = slice(2 * hp * SSM_HEAD_DIM, (2 * hp + 2) * SSM_HEAD_DIM)
            y_ref[:, ps] = y_diag + y_off[:, off] + dsk_ref[:, ps] * xs[:, ps]
    y_ref[...] = _gated_group_norm(y_ref[...], z_ref[...], g_ref[...])

    @pl.when(c == pl.num_programs(1) - 1)
    def _():
        hout_ref[...] = ht_ref[...].T


def _ssd_prompt_call(proj, sp):
    b, s_len, _ = proj.shape
    q = SSD_CHUNK
    const = lambda shape: pl.BlockSpec(shape, lambda bi, c: (0,) * len(shape))
    y, h_last = pl.pallas_call(
        _ssd_prompt_kernel,
        out_shape=(jax.ShapeDtypeStruct((b, s_len, D_SSM), F32),
                   jax.ShapeDtypeStruct((b, D_SSM, D_STATE), F32)),
        grid=(b, s_len // q),
        in_specs=[pl.BlockSpec((None, q, D_SSM), lambda bi, c: (bi, c, 3)),
                  pl.BlockSpec((None, q, 2048), lambda bi, c: (bi, c, 2)),
                  const((CONV_W, CONV_DIM)), const((1, CONV_DIM)), const((1, 128)), const((1, 128)),
                  const((1, D_SSM)), const((1, D_SSM))],
        out_specs=(pl.BlockSpec((None, q, D_SSM), lambda bi, c: (bi, c, 0)),
                   pl.BlockSpec((None, D_SSM, D_STATE), lambda bi, c: (bi, 0, 0))),
        scratch_shapes=[pltpu.VMEM((D_STATE, D_SSM), F32), pltpu.VMEM((8 + q, CONV_DIM), F32)],
        compiler_params=pltpu.CompilerParams(
            dimension_semantics=("parallel", "arbitrary"), vmem_limit_bytes=VMEM_LIMIT_BYTES),
        name="ssd_prompt",
    )(proj, proj, sp['conv_w'], sp['conv_b'], sp['dt_bias'], sp['a_log'], sp['d_skip_x'], sp['g_ssm_out'])
    return y.reshape(b * s_len, D_SSM), h_last.reshape(b, H_SSM, SSM_HEAD_DIM, D_STATE)


def _ssd_sample_kernel(z_ref, xd_ref, buf_ref, h0_ref, cw_ref, cb_ref, dtb_ref, alog_ref, dsk_ref, g_ref,
                       y_ref, hout_ref, ext_ref):
    n_seq, n_tok, _ = z_ref.shape
    expand = _head_lane_mask(128).astype(BF16)
    ti = lax.broadcasted_iota(jnp.int32, (8, 8), 0)
    si = lax.broadcasted_iota(jnp.int32, (8, 8), 1)
    causal = (ti >= si) & (si < n_tok)
    real = lax.broadcasted_iota(jnp.int32, (8, 1), 0) < n_tok
    ones = jnp.ones((8, D_STATE), BF16)
    ext_ref[...] = jnp.zeros_like(ext_ref)

    def body(j, carry):
        ext_ref[8 - (CONV_W - 1):8, :] = buf_ref[j]
        ext_ref[8:8 + n_tok, :] = xd_ref[j][:, 0:CONV_DIM]
        act = _causal_conv(ext_ref, 8, cw_ref, cb_ref)
        xs = act[:, 0:D_SSM]
        bm = act[:, D_SSM:D_SSM + SSM_GROUPS * D_STATE]
        cm = act[:, D_SSM + SSM_GROUPS * D_STATE:CONV_DIM]
        pad = jnp.zeros((8 - n_tok, 128), F32)
        dt = _softplus(jnp.concatenate([xd_ref[j][:, CONV_DIM:CONV_DIM + 128], pad], axis=0) + dtb_ref[...])
        da = jnp.where(real, dt * (-jnp.exp(alog_ref[...])), 0.0)
        a_cum = None
        for piece in _split_bf16(da, 3):
            t = jnp.dot((ti >= si).astype(BF16), piece, preferred_element_type=F32)
            a_cum = t if a_cum is None else a_cum + t
        a_last = a_cum[n_tok - 1:n_tok, :]
        acum_x = _dot_split(a_cum, expand, 3)
        eac_x = jnp.exp(acum_x)
        dt_x = _dot_split(dt, expand, 2)
        xdt = jnp.where(real, xs * dt_x, 0.0)
        xdd = (xdt * jnp.exp(acum_x[n_tok - 1:n_tok, :] - acum_x)).astype(BF16)
        last_x = jnp.where(lax.broadcasted_iota(jnp.int32, (8, 1), 0) == n_tok - 1, eac_x, 0.0)
        keep = _dot_split(last_x, ones, 2, dims=TN_DIMS)
        y_parts, new_state = [], []
        for grp in range(SSM_GROUPS):
            gs = slice(grp * GROUP_W, (grp + 1) * GROUP_W)
            bm_b = bm[:, grp * D_STATE:(grp + 1) * D_STATE].astype(BF16)
            cm_b = cm[:, grp * D_STATE:(grp + 1) * D_STATE].astype(BF16)
            cb_mat = jnp.where(causal, lax.dot_general(cm_b, bm_b, NT_DIMS, preferred_element_type=F32), 0.0)
            h0_g = h0_ref[j, gs, :]
            y_g = lax.dot_general(cm_b, h0_g.astype(BF16), NT_DIMS, preferred_element_type=F32) * eac_x[:, gs]
            for s in range(n_tok):
                coef = cb_mat[:, s:s + 1] * jnp.exp(jnp.where(ti[:, 0:1] >= s, acum_x[:, gs] - acum_x[s:s + 1, gs], -jnp.inf))
                y_g = y_g + coef * xdt[s:s + 1, gs]
            y_parts.append(y_g)
            states = lax.dot_general(xdd[:, gs], bm_b, TN_DIMS, preferred_element_type=F32)
            hout_ref[j, gs, :] = h0_g * keep[gs, :] + states
        y = jnp.concatenate(y_parts, axis=1) + dsk_ref[...] * xs
        zj = jnp.concatenate([z_ref[j], jnp.zeros((8 - n_tok, D_SSM), F32)], axis=0)
        y_ref[j] = _gated_group_norm(y, zj, g_ref[...])[0:n_tok, :]
        return carry

    lax.fori_loop(0, n_seq, body, 0)


def _ssd_sample_call(proj, conv_buf, h0, sp):
    b, n_tok, _ = proj.shape
    nseq = 8
    const = lambda shape: pl.BlockSpec(shape, lambda i: (0,) * len(shape))
    y, h_last = pl.pallas_call(
        _ssd_sample_kernel,
        out_shape=(jax.ShapeDtypeStruct((b, n_tok, D_SSM), F32),
                   jax.ShapeDtypeStruct((b, D_SSM, D_STATE), F32)),
        grid=(b // nseq,),
        in_specs=[pl.BlockSpec((nseq, n_tok, D_SSM), lambda i: (i, 0, 3)),
                  pl.BlockSpec((nseq, n_tok, 2048), lambda i: (i, 0, 2)),
                  pl.BlockSpec((nseq, CONV_W - 1, CONV_DIM), lambda i: (i, 0, 0)),
                  pl.BlockSpec((nseq, D_SSM, D_STATE), lambda i: (i, 0, 0)),
                  const((CONV_W, CONV_DIM)), const((1, CONV_DIM)), const((1, 128)), const((1, 128)),
                  const((1, D_SSM)), const((1, D_SSM))],
        out_specs=(pl.BlockSpec((nseq, n_tok, D_SSM), lambda i: (i, 0, 0)),
                   pl.BlockSpec((nseq, D_SSM, D_STATE), lambda i: (i, 0, 0))),
        scratch_shapes=[pltpu.VMEM((16, CONV_DIM), F32)],
        compiler_params=pltpu.CompilerParams(
            dimension_semantics=("parallel",), vmem_limit_bytes=VMEM_LIMIT_BYTES),
        name="ssd_sample",
    )(proj, proj, conv_buf, h0.reshape(b, D_SSM, D_STATE), sp['conv_w'], sp['conv_b'], sp['dt_bias'],
      sp['a_log'], sp['d_skip_x'], sp['g_ssm_out'])
    return y.reshape(b * n_tok, D_SSM), h_last.reshape(b, H_SSM, SSM_HEAD_DIM, D_STATE)


def _layer(x, mods, p, attend, conv_buf, h0, *, tm, g_final, qkv_dilations):
    b, L, _ = x.shape
    m = b * L
    sh1, sc1, g1, sh2, sc2, g2, sh3, sc3, g3 = mods
    x2 = x.reshape(m, D_MODEL)
    kw = dict(tm=tm, rows_per_mod=L)
    x2 = _ffn_call(x2, p['g_ffn1'], sh1, sc1, g1, p['w_gate1'], p['w_up1'], p['w_down1'],
                   g_final, final_norm=False, **kw)
    proj, *qkv = _inproj_call(x2, p['g_mix'], sh2, sc2, p['w_in_pad'], dilations=qkv_dilations,
                              tm=min(tm, 512), rows_per_mod=L)
    proj = proj.reshape(b, L, D_IN_PAD)
    k = proj[..., D_ATT:2 * D_ATT].reshape(b, L, H_ATT, HEAD_DIM)
    v = proj[..., 2 * D_ATT:3 * D_ATT].reshape(b, L, H_ATT, HEAD_DIM)
    o = 3 * D_ATT + D_SSM
    conv_new = proj[:, L - (CONV_W - 1):, o:o + CONV_DIM]
    att_parts = attend(proj, qkv)
    if h0 is None:
        ssd_o, h_last = _ssd_prompt_call(proj, p)
    else:
        ssd_o, h_last = _ssd_sample_call(proj, conv_buf, h0, p)
    x2 = _outproj_call(x2, att_parts, ssd_o, g2, p['w_out_att'], p['w_out_ssm'],
                       p['g_attn_out'], tm=min(tm, 256), rows_per_mod=L, dilations=qkv_dilations)
    x2 = _ffn_call(x2, p['g_ffn2'], sh3, sc3, g3, p['w_gate2'], p['w_up2'], p['w_down2'],
                   g_final, final_norm=True, **kw)
    return x2.reshape(b, L, D_MODEL), k, v, h_last, conv_new


def kernel(x_prompt, x_sample, cache_k, cache_v, state_ssm, state_conv, c_prompt, c_sample,
           rpb_table, w_ada, b_ada, g_ffn1, w_gate1, w_up1, w_down1, g_mix, w_in, conv_w, conv_b,
           dt_bias, a_log, d_skip, g_ssm_out, g_attn_out, w_out, g_ffn2, w_gate2, w_up2, w_down2, g_final):
    bp, sp, _ = x_prompt.shape
    bs, ls, _ = x_sample.shape
    wbp = min(WINDOW_MAX, sp)
    layer = 0
    w_in_pad = jnp.pad(w_in[layer].astype(BF16), ((0, 0), (0, D_IN_PAD - D_IN)))
    p = {'g_ffn1': g_ffn1[layer], 'w_gate1': w_gate1[layer].astype(BF16), 'w_up1': w_up1[layer].astype(BF16),
         'w_down1': w_down1[layer].astype(BF16), 'g_mix': g_mix[layer], 'w_in_pad': w_in_pad,
         'conv_w': conv_w[layer], 'conv_b': conv_b[layer].reshape(1, CONV_DIM),
         'dt_bias': jnp.pad(dt_bias[layer], (0, 128 - H_SSM)).reshape(1, 128),
         'a_log': jnp.pad(a_log[layer], (0, 128 - H_SSM)).reshape(1, 128),
         'd_skip_x': jnp.repeat(d_skip[layer], SSM_HEAD_DIM).reshape(1, D_SSM),
         'g_ssm_out': g_ssm_out[layer].reshape(1, D_SSM), 'g_attn_out': g_attn_out[layer],
         'w_out_att': w_out[layer, :D_ATT].astype(BF16), 'w_out_ssm': w_out[layer, D_ATT:].astype(BF16),
         'g_ffn2': g_ffn2[layer], 'w_gate2': w_gate2[layer].astype(BF16), 'w_up2': w_up2[layer].astype(BF16),
         'w_down2': w_down2[layer].astype(BF16)}

    n_c = bp + bs
    c_rows = -(-n_c // 16) * 16
    c_all = jnp.pad(jnp.concatenate([c_prompt, c_sample], axis=0), ((0, c_rows - n_c), (0, 0)))
    mod = _mod_call(c_all, w_ada[layer], b_ada[layer])
    mods_p = [mod[:bp, i * D_MODEL:(i + 1) * D_MODEL].reshape(bp, 1, D_MODEL) for i in range(N_MOD)]
    mods_s = [jnp.repeat(mod[bp:n_c, i * D_MODEL:(i + 1) * D_MODEL], ls, axis=0) for i in range(N_MOD)]

    def attend_prompt(proj, qkv):
        outs, lses = [], []
        for (window, dil), qkv_d in zip(DILATED_PATTERNS, qkv):
            o, lse = _attn_prompt_call(qkv_d, _prompt_bias(rpb_table, window, dil), dil, bp)
            outs.append(o)
            lses.append(lse)
        return outs + lses

    lb_c, lb_n = _sample_log_bias(rpb_table, ls, cache_k.shape[2])
    cache_kt = jnp.transpose(cache_k[layer], (0, 2, 3, 1))
    cache_vt = jnp.transpose(cache_v[layer], (0, 2, 3, 1))

    def attend_sample(proj, qkv):
        return [_attn_sample_call(proj, cache_kt, cache_vt, lb_c, lb_n, p['g_attn_out'])]

    yp, kp, vp, hp, cp = _layer(x_prompt, mods_p, p, attend_prompt, None, None, tm=1024, g_final=g_final,
                                qkv_dilations=tuple(d for _, d in DILATED_PATTERNS))
    ys, kn, vn, hs, cs = _layer(x_sample, mods_s, p, attend_sample, state_conv[layer], state_ssm[layer],
                                tm=bs * ls, g_final=g_final, qkv_dilations=())
    return (yp, ys, kp[None, :, -wbp:], vp[None, :, -wbp:], kn[None], vn[None],
            hp[None], hs[None], cp[None], cs[None])
```

```python
import functools
import math

import jax
import jax.numpy as jnp
from jax import lax
from jax.experimental import pallas as pl
from jax.experimental.pallas import tpu as pltpu

D_MODEL = 2048
HEAD_DIM = 64
D_ATT = 1024
D_SSM = 1024
H_ATT = 16
SSM_HEAD_DIM = 64
H_SSM = 16
SSM_GROUPS = 2
D_STATE = 128
CONV_W = 4
SSD_CHUNK = 128
CONV_DIM = D_SSM + 2 * SSM_GROUPS * D_STATE
D_FF = 5632
DILATED_PATTERNS = ((128, 1), (512, 4), (2048, 16))
WINDOW_MAX = 2048
ATT_BLOCK = 128
RPB_BUCKETS = 32
RPB_MAX_DIST = 2048
NORM_EPS = 1e-6
N_MOD = 9
PROJ_SIZES = (D_ATT, D_ATT, D_ATT, D_SSM, CONV_DIM, H_SSM)
D_IN = sum(PROJ_SIZES)
D_IN_PAD = 6144

VMEM_LIMIT_BYTES = 56 * 1024 * 1024

BF16 = jnp.bfloat16
F32 = jnp.float32


def _silu(v):
    return v * jax.nn.sigmoid(v)


def _norm_mod(x, g, shift, scale):
    y = x * lax.rsqrt(jnp.mean(x * x, axis=-1, keepdims=True) + NORM_EPS)
    return (y * g) * (1.0 + scale) + shift


def _mod_kernel(c_ref, w_ref, b_ref, o_ref, a_ref):
    @pl.when(pl.program_id(0) == 0)
    def _():
        a_ref[...] = _silu(c_ref[...]).astype(BF16)

    o_ref[...] = jnp.dot(a_ref[...], w_ref[...].astype(BF16),
                         preferred_element_type=F32) + b_ref[...]


def _mod_call(c_all, w_ada, b_ada):
    rows = c_all.shape[0]
    n = w_ada.shape[1]
    tn = 1024
    return pl.pallas_call(
        _mod_kernel,
        out_shape=jax.ShapeDtypeStruct((rows, n), F32),
        grid=(n // tn,),
        in_specs=[
            pl.BlockSpec((rows, D_MODEL), lambda j: (0, 0)),
            pl.BlockSpec((D_MODEL, tn), lambda j: (0, j)),
            pl.BlockSpec((1, tn), lambda j: (0, j)),
        ],
        out_specs=pl.BlockSpec((rows, tn), lambda j: (0, j)),
        scratch_shapes=[pltpu.VMEM((rows, D_MODEL), BF16)],
        compiler_params=pltpu.CompilerParams(
            dimension_semantics=("arbitrary",), vmem_limit_bytes=VMEM_LIMIT_BYTES),
        name="adaln_mod",
    )(c_all, w_ada, b_ada.reshape(1, n))


NT_DIMS = (((1,), (1,)), ((), ()))
RIDER_HEADS = 8


def _sample_attn_heads(q_ref, kn_ref, vn_ref, kt_ref, vt_ref, lbc_ref, lbn_ref, o_ref):
    n_q, width = q_ref.shape
    pad = jnp.zeros((8 - n_q, width), F32)
    q8 = jnp.concatenate([q_ref[...] * (HEAD_DIM ** -0.5), pad], axis=0).astype(BF16)
    kn8 = jnp.concatenate([kn_ref[...], pad], axis=0).astype(BF16)
    vn8 = jnp.concatenate([vn_ref[...], pad], axis=0).astype(BF16)
    for h in range(width // HEAD_DIM):
        sl = slice(h * HEAD_DIM, (h + 1) * HEAD_DIM)
        qh = q8[:, sl]
        t_c = jnp.dot(qh, kt_ref[h].astype(BF16), preferred_element_type=F32) + lbc_ref[h]
        t_n = lax.dot_general(qh, kn8[:, sl], NT_DIMS, preferred_element_type=F32) + lbn_ref[h]
        m = jnp.maximum(jnp.max(t_c, axis=-1, keepdims=True), jnp.max(t_n, axis=-1, keepdims=True))
        p_c = jnp.exp(t_c - m)
        p_n = jnp.exp(t_n - m)
        l = jnp.sum(p_c, axis=-1, keepdims=True) + jnp.sum(p_n, axis=-1, keepdims=True)
        acc = lax.dot_general(p_c.astype(BF16), vt_ref[h].astype(BF16), NT_DIMS, preferred_element_type=F32)
        acc += jnp.dot(p_n.astype(BF16), vn8[:, sl], preferred_element_type=F32)
        o_ref[:, sl] = (acc / l)[0:n_q, :]


def _ffn_kernel(*refs, final_norm, rider):
    x_ref, g_ref, sh_ref, sc_ref, gt_ref, wg_ref, wu_ref, wd_ref, gf_ref = refs[:9]
    if rider:
        rider_in, (o_ref, att_ref, h_ref) = refs[9:16], refs[16:]
    else:
        o_ref, h_ref = refs[9:]
    j = pl.program_id(1)

    @pl.when(j == 0)
    def _():
        h_ref[...] = _norm_mod(x_ref[...], g_ref[...], sh_ref[...], sc_ref[...]).astype(BF16)
        o_ref[...] = jnp.zeros_like(o_ref)

    h = h_ref[...]
    a = jnp.dot(h, wg_ref[...], preferred_element_type=F32)
    b = jnp.dot(h, wu_ref[...], preferred_element_type=F32)
    s = (_silu(a) * b).astype(BF16)
    o_ref[...] += jnp.dot(s, wd_ref[...], preferred_element_type=F32)
    if rider:
        _sample_attn_heads(*rider_in, att_ref)

    @pl.when(j == pl.num_programs(1) - 1)
    def _():
        y = x_ref[...] + 0.5 * gt_ref[...] * o_ref[...]
        if final_norm:
            y = y * lax.rsqrt(jnp.mean(y * y, axis=-1, keepdims=True) + NORM_EPS) * gf_ref[...]
        o_ref[...] = y


def _mod_spec(mod, tm, rows_per_mod):
    if mod.ndim == 3:
        tiles_per_row = rows_per_mod // tm
        return pl.BlockSpec((None, 1, D_MODEL), lambda i, j: (i // tiles_per_row, 0, 0))
    return pl.BlockSpec((tm, D_MODEL), lambda i, j: (i, 0))


def _ffn_call(x, g, sh, sc, gt, wg, wu, wd, g_final, *, tm, rows_per_mod, final_norm, rider=None):
    m = x.shape[0]
    tf = 256 if rider else 512
    nj = D_FF // tf
    row = lambda i, j: (i, 0)
    const = lambda i, j: (0, 0)
    in_specs = [
        pl.BlockSpec((tm, D_MODEL), row, pipeline_mode=pl.Buffered(1)),
        pl.BlockSpec((1, D_MODEL), const),
        _mod_spec(sh, tm, rows_per_mod),
        _mod_spec(sc, tm, rows_per_mod),
        _mod_spec(gt, tm, rows_per_mod),
        pl.BlockSpec((D_MODEL, tf), lambda i, j: (0, j)),
        pl.BlockSpec((D_MODEL, tf), lambda i, j: (0, j)),
        pl.BlockSpec((tf, D_MODEL), lambda i, j: (j, 0)),
        pl.BlockSpec((1, D_MODEL), const),
    ]
    args = [x, g.reshape(1, D_MODEL), sh, sc, gt, wg, wu, wd, g_final.reshape(1, D_MODEL)]
    out_shape = jax.ShapeDtypeStruct((m, D_MODEL), F32)
    out_specs = pl.BlockSpec((tm, D_MODEL), row)
    if rider:
        proj_s, cache_kt, cache_vt, lb_c, lb_n, first_seq, n_seq = rider
        n_q = proj_s.shape[1]
        wb = cache_kt.shape[3]
        halves = H_ATT // RIDER_HEADS
        n_units = n_seq * halves
        assert n_units <= (m // tm) * nj
        rw = RIDER_HEADS * HEAD_DIM
        unit = lambda i, j: jnp.minimum(i * nj + j, n_units - 1)
        seq = lambda i, j: first_seq + unit(i, j) // halves
        half = lambda i, j: unit(i, j) % halves
        tiles = D_ATT // rw
        qspec = lambda t: pl.BlockSpec((None, n_q, rw), lambda i, j: (seq(i, j), 0, t * tiles + half(i, j)))
        cspec = pl.BlockSpec((None, RIDER_HEADS, HEAD_DIM, wb), lambda i, j: (seq(i, j), half(i, j), 0, 0))
        in_specs += [qspec(0), qspec(1), qspec(2), cspec, cspec,
                     pl.BlockSpec((RIDER_HEADS, 8, wb), lambda i, j: (half(i, j), 0, 0)),
                     pl.BlockSpec((RIDER_HEADS, 8, 8), lambda i, j: (half(i, j), 0, 0))]
        args += [proj_s, proj_s, proj_s, cache_kt, cache_vt, lb_c, lb_n]
        out_shape = (out_shape, jax.ShapeDtypeStruct((n_seq, n_q, D_ATT), F32))
        out_specs = (pl.BlockSpec((tm, D_MODEL), row, pipeline_mode=pl.Buffered(1)),
                     pl.BlockSpec((None, n_q, rw), lambda i, j: (seq(i, j) - first_seq, 0, half(i, j))))
    return pl.pallas_call(
        functools.partial(_ffn_kernel, final_norm=final_norm, rider=bool(rider)),
        out_shape=out_shape,
        grid=(m // tm, nj),
        in_specs=in_specs,
        out_specs=out_specs,
        scratch_shapes=[pltpu.VMEM((tm, D_MODEL), BF16)],
        compiler_params=pltpu.CompilerParams(
            dimension_semantics=("arbitrary", "arbitrary") if rider else ("parallel", "arbitrary"),
            vmem_limit_bytes=VMEM_LIMIT_BYTES),
        name=("ffn_final" if final_norm else "ffn") + ("_rider" if rider else ""),
    )(*args)


QKV_TILES = 3


def _inproj_kernel(x_ref, g_ref, sh_ref, sc_ref, w_ref, o_ref, *rest, dilations):
    qkv_refs, h_ref, lanes_ref = rest[:-2], rest[-2], rest[-1]
    j = pl.program_id(1)
    tm = o_ref.shape[0]

    @pl.when(j == 0)
    def _():
        h_ref[...] = _norm_mod(x_ref[...], g_ref[...], sh_ref[...], sc_ref[...]).astype(BF16)

    res = jnp.dot(h_ref[...], w_ref[...], preferred_element_type=F32)
    o_ref[...] = res

    @pl.when(j < QKV_TILES)
    def _():
        n_chunks = D_ATT // 128
        if any(d > 1 for d in dilations):
            for c in range(n_chunks):
                lanes_ref[c] = res[:, c * 128:(c + 1) * 128]
        for d, ref in zip(dilations, qkv_refs):
            if d == 1:
                ref[...] = res.astype(BF16)
                continue
            for r in range(d):
                for c in range(n_chunks):
                    rows = lanes_ref[c, pl.ds(r, tm // d, stride=d), :]
                    ref[:, r * D_ATT + c * 128:r * D_ATT + (c + 1) * 128] = rows.astype(BF16)


def _inproj_call(x, g, sh, sc, w_pad, *, tm, rows_per_mod, dilations):
    m = x.shape[0]
    tn = D_ATT
    qkv_j = lambda j: jnp.minimum(j, QKV_TILES - 1)
    return pl.pallas_call(
        functools.partial(_inproj_kernel, dilations=dilations),
        out_shape=(jax.ShapeDtypeStruct((m, D_IN_PAD), F32),)
        + tuple(jax.ShapeDtypeStruct((m // d, QKV_TILES * d * tn), BF16) for d in dilations),
        grid=(m // tm, D_IN_PAD // tn),
        in_specs=[
            pl.BlockSpec((tm, D_MODEL), lambda i, j: (i, 0), pipeline_mode=pl.Buffered(1)),
            pl.BlockSpec((1, D_MODEL), lambda i, j: (0, 0)),
            _mod_spec(sh, tm, rows_per_mod),
            _mod_spec(sc, tm, rows_per_mod),
            pl.BlockSpec((D_MODEL, tn), lambda i, j: (0, j)),
        ],
        out_specs=(pl.BlockSpec((tm, tn), lambda i, j: (i, j)),)
        + tuple(pl.BlockSpec((tm // d, d * tn), lambda i, j: (i, qkv_j(j))) for d in dilations),
        scratch_shapes=[pltpu.VMEM((tm, D_MODEL), BF16), pltpu.VMEM((tn // 128, tm, 128), F32)],
        compiler_params=pltpu.CompilerParams(
            dimension_semantics=("parallel", "arbitrary"), vmem_limit_bytes=VMEM_LIMIT_BYTES),
        name="in_proj",
    )(x, g.reshape(1, D_MODEL), sh, sc, w_pad)


def _outproj_kernel(*refs, dilations):
    x_ref = refs[0]
    n_pat = len(dilations)
    if n_pat:
        o_refs = refs[1:1 + n_pat]
        l_refs = refs[1 + n_pat:1 + 2 * n_pat]
        ga_ref, ssd_ref, gt_ref, wa_ref, ws_ref, out_ref, o_scr, l_scr = refs[1 + 2 * n_pat:]
        tm = x_ref.shape[0]
        n_chunks = D_ATT // 128
        outs, lses = [], []
        for p, (d, o_ref, l_ref) in enumerate(zip(dilations, o_refs, l_refs)):
            if d == 1:
                outs.append(o_ref[...].astype(F32))
                lses.append(l_ref[...])
                continue
            for r in range(d):
                l_scr[p, pl.ds(r, tm // d, stride=d), :] = l_ref[:, r * 128:(r + 1) * 128]
                for c in range(n_chunks):
                    lanes = slice(r * D_ATT + c * 128, r * D_ATT + (c + 1) * 128)
                    o_scr[p, c, pl.ds(r, tm // d, stride=d), :] = o_ref[:, lanes].astype(F32)
            outs.append(jnp.concatenate([o_scr[p, c] for c in range(n_chunks)], axis=1))
            lses.append(l_scr[p])
        mx = functools.reduce(jnp.maximum, lses)
        ws = [jnp.exp(v - mx) for v in lses]
        wsum = functools.reduce(lambda a, b: a + b, ws)
        expand = _head_lane_mask(128).astype(BF16)
        att = None
        for w, o in zip(ws, outs):
            wn = w / wsum
            w_hi = wn.astype(BF16)
            w_lo = (wn - w_hi.astype(F32)).astype(BF16)
            wx = (jnp.dot(w_hi, expand, preferred_element_type=F32)
                  + jnp.dot(w_lo, expand, preferred_element_type=F32))
            att = wx * o if att is None else att + wx * o
    else:
        att_ref, ga_ref, ssd_ref, gt_ref, wa_ref, ws_ref, out_ref = refs[1:]
        att = att_ref[...]
    att = att * lax.rsqrt(jnp.mean(att * att, axis=-1, keepdims=True) + NORM_EPS) * ga_ref[...]
    mix = jnp.dot(att.astype(BF16), wa_ref[...], preferred_element_type=F32)
    mix += jnp.dot(ssd_ref[...].astype(BF16), ws_ref[...], preferred_element_type=F32)
    out_ref[...] = x_ref[...] + gt_ref[...] * mix


def _outproj_call(x, att_parts, ssd, gt, w_att, w_ssd, g_attn, *, tm, rows_per_mod, dilations):
    m = x.shape[0]
    n_pat = len(dilations)
    if gt.ndim == 3:
        tiles_per_row = rows_per_mod // tm
        gt_spec = pl.BlockSpec((None, 1, D_MODEL), lambda i: (i // tiles_per_row, 0, 0))
    else:
        gt_spec = pl.BlockSpec((tm, D_MODEL), lambda i: (i, 0))
    row = lambda width, d=1: pl.BlockSpec((tm // d, d * width), lambda i: (i, 0))
    scratch = []
    if n_pat:
        att_specs = ([row(D_ATT, d) for d in dilations] + [row(128, d) for d in dilations]
                     + [pl.BlockSpec((1, D_ATT), lambda i: (0, 0))])
        att_args = list(att_parts) + [g_attn.reshape(1, D_ATT)]
        scratch = [pltpu.VMEM((n_pat, D_ATT // 128, tm, 128), F32), pltpu.VMEM((n_pat, tm, 128), F32)]
    else:
        att_specs = [row(D_ATT), pl.BlockSpec((1, D_ATT), lambda i: (0, 0))]
        att_args = list(att_parts) + [g_attn.reshape(1, D_ATT)]
    weight = lambda k: pl.BlockSpec((k, D_MODEL), lambda i: (0, 0), pipeline_mode=pl.Buffered(1))
    return pl.pallas_call(
        functools.partial(_outproj_kernel, dilations=dilations),
        out_shape=jax.ShapeDtypeStruct((m, D_MODEL), F32),
        grid=(m // tm,),
        in_specs=[row(D_MODEL)] + att_specs + [row(D_SSM), gt_spec, weight(D_ATT), weight(D_SSM)],
        out_specs=row(D_MODEL),
        scratch_shapes=scratch,
        compiler_params=pltpu.CompilerParams(
            dimension_semantics=("parallel",), vmem_limit_bytes=VMEM_LIMIT_BYTES),
        name="out_proj_merge" if n_pat else "out_proj",
    )(x, *att_args, ssd, gt, w_att, w_ssd)


def _rel_bucket(dist):
    max_exact = RPB_BUCKETS // 2
    df = jnp.maximum(dist, 1).astype(jnp.float32)
    large = max_exact + (jnp.log(df / max_exact) / math.log(RPB_MAX_DIST / max_exact)
                         * (RPB_BUCKETS - max_exact)).astype(jnp.int32)
    large = jnp.minimum(large, RPB_BUCKETS - 1)
    return jnp.where(dist < max_exact, dist, large)


def _head_lane_mask(rows=H_ATT):
    r = lax.broadcasted_iota(jnp.int32, (rows, D_ATT), 0)
    c = lax.broadcasted_iota(jnp.int32, (rows, D_ATT), 1)
    return (c // HEAD_DIM == r).astype(F32)


def _attn_prompt_kernel(q_ref, kp_ref, kc_ref, vp_ref, vc_ref, bias_ref, o_ref, lse_ref):
    nb = pl.program_id(2)
    blk = ATT_BLOCK
    col = lax.broadcasted_iota(jnp.int32, (blk, 2 * blk), 1)
    has_prev = (col >= blk) | (nb > 0)
    lane = lax.broadcasted_iota(jnp.int32, (blk, 2 * HEAD_DIM), 1)
    lo = lane < HEAD_DIM
    lse_lane = lax.broadcasted_iota(jnp.int32, (blk, 128), 1)
    lse_tile = jnp.zeros((blk, 128), F32)
    scale = HEAD_DIM ** -0.5
    for hp in range(H_ATT // 2):
        sl = slice(hp * 2 * HEAD_DIM, (hp + 1) * 2 * HEAD_DIM)
        q2 = q_ref[:, sl] * scale
        k2 = jnp.concatenate([kp_ref[:, sl], kc_ref[:, sl]], axis=0)
        v2 = jnp.concatenate([vp_ref[:, sl], vc_ref[:, sl]], axis=0)
        outs = []
        for half in range(2):
            h = 2 * hp + half
            qh = jnp.where(lo if half == 0 else ~lo, q2, jnp.zeros_like(q2))
            s = lax.dot_general(qh, k2, NT_DIMS, preferred_element_type=F32) + bias_ref[h]
            s = jnp.where(has_prev, s, -jnp.inf)
            m = jnp.max(s, axis=-1, keepdims=True)
            p = jnp.exp(s - m)
            l = jnp.sum(p, axis=-1, keepdims=True)
            pv = jnp.dot(p.astype(BF16), v2, preferred_element_type=F32)
            outs.append(pv / l)
            lse_tile = jnp.where(lse_lane == h, m + jnp.log(l), lse_tile)
        o_ref[:, sl] = jnp.where(lo, outs[0], outs[1]).astype(o_ref.dtype)
    lse_ref[...] = lse_tile


def _attn_prompt_call(qkv, bias, dil, b):
    blk = ATT_BLOCK
    n_sub = qkv.shape[0] // b
    s_len = n_sub * dil
    nbk = n_sub // blk
    pv = qkv.reshape(b, n_sub, qkv.shape[1])
    cur = lambda t: (lambda bi, r, nb: (bi, nb, t * dil + r))
    prev = lambda t: (lambda bi, r, nb: (bi, jnp.maximum(nb - 1, 0), t * dil + r))
    spec = lambda f: pl.BlockSpec((None, blk, D_ATT), f)
    o, lse = pl.pallas_call(
        _attn_prompt_kernel,
        out_shape=(jax.ShapeDtypeStruct((b, n_sub, dil * D_ATT), BF16),
                   jax.ShapeDtypeStruct((b, n_sub, dil * 128), F32)),
        grid=(b, dil, nbk),
        in_specs=[spec(cur(0)), spec(prev(1)), spec(cur(1)), spec(prev(2)), spec(cur(2)),
                  pl.BlockSpec((H_ATT, blk, 2 * blk), lambda bi, r, nb: (0, 0, 0))],
        out_specs=(pl.BlockSpec((None, blk, D_ATT), lambda bi, r, nb: (bi, nb, r)),
                   pl.BlockSpec((None, blk, 128), lambda bi, r, nb: (bi, nb, r))),
        compiler_params=pltpu.CompilerParams(
            dimension_semantics=("parallel", "parallel", "arbitrary"),
            vmem_limit_bytes=VMEM_LIMIT_BYTES),
        name=f"attn_prompt_d{dil}",
    )(pv, pv, pv, pv, pv, bias)
    return o.reshape(b * n_sub, dil * D_ATT), lse.reshape(b * n_sub, dil * 128)


def _prompt_bias(rpb, window, dil):
    span = window // dil
    blk = ATT_BLOCK
    assert span == blk
    table = rpb[_rel_bucket(jnp.arange(span + 1) * dil)].T.astype(F32)
    fill = jnp.full((H_ATT, blk), -jnp.inf, F32)
    row = jnp.concatenate([fill, table[:, ::-1], fill], axis=1)
    wrapped = jnp.tile(row, (1, blk))[:, :blk * 3 * blk].reshape(H_ATT, blk, 3 * blk)
    return wrapped[:, :, blk:]


def _sample_log_bias(rpb, n_q, wb):
    reach = max(w for w, _ in DILATED_PATTERNS) + 2 * 8
    assert wb + n_q <= reach
    terms = []
    for window, dil in DILATED_PATTERNS:
        span = window // dil
        table = rpb[_rel_bucket(jnp.arange(span + 1) * dil)].T.astype(F32)
        gaps = jnp.full((H_ATT, span + 1, dil - 1), -jnp.inf, F32)
        by_dist = jnp.concatenate([table[:, :, None], gaps], axis=2).reshape(H_ATT, (span + 1) * dil)
        by_dist = by_dist[:, :reach]
        terms.append(jnp.pad(by_dist, ((0, 0), (0, reach - by_dist.shape[1])), constant_values=-jnp.inf))
    lb = jax.nn.logsumexp(jnp.stack(terms), axis=0)
    neg = jnp.full((H_ATT,), -jnp.inf, F32)
    rows_c, rows_n = [], []
    for r in range(8):
        n = r % n_q
        rows_c.append(lb[:, n + 1:n + 1 + wb][:, ::-1])
        rows_n.append(jnp.stack([lb[:, n - m] if (m < n_q and m <= n) else neg for m in range(8)], axis=1))
    return jnp.stack(rows_c, axis=1), jnp.stack(rows_n, axis=1)


TN_DIMS = (((0,), (0,)), ((), ()))
GROUP_W = D_SSM // SSM_GROUPS


def _split_bf16(v, pieces):
    out = []
    for _ in range(pieces):
        hi = v.astype(BF16)
        out.append(hi)
        v = v - hi.astype(F32)
    return out


def _dot_split(v, rhs_bf16, pieces, dims=None):
    acc = None
    for piece in _split_bf16(v, pieces):
        if dims is None:
            t = jnp.dot(piece, rhs_bf16, preferred_element_type=F32)
        else:
            t = lax.dot_general(piece, rhs_bf16, dims, preferred_element_type=F32)
        acc = t if acc is None else acc + t
    return acc


def _softplus(v):
    return jnp.maximum(v, 0.0) + jnp.log1p(jnp.exp(-jnp.abs(v)))


def _causal_conv(ext_ref, rows, cw_ref, cb_ref):
    conv = cb_ref[...]
    for k in range(CONV_W):
        conv = conv + cw_ref[CONV_W - 1 - k:CONV_W - k, :] * ext_ref[8 - k:8 - k + rows, :]
    return _silu(conv)


def _gated_group_norm(y, z, g):
    gated = y * _silu(z)
    parts = []
    for grp in range(SSM_GROUPS):
        seg = gated[:, grp * GROUP_W:(grp + 1) * GROUP_W]
        parts.append(seg * lax.rsqrt(jnp.mean(seg * seg, axis=-1, keepdims=True) + NORM_EPS))
    return jnp.concatenate(parts, axis=1) * g


def _ssd_prompt_kernel(z_ref, xd_ref, cw_ref, cb_ref, dtb_ref, alog_ref, dsk_ref, g_ref,
                       y_ref, hout_ref, ht_ref, ext_ref):
    c = pl.program_id(1)
    q = SSD_CHUNK

    @pl.when(c == 0)
    def _():
        ht_ref[...] = jnp.zeros_like(ht_ref)
        ext_ref[0:8, :] = jnp.zeros((8, CONV_DIM), F32)

    x_raw = xd_ref[:, 0:CONV_DIM]
    ext_ref[8:8 + q, :] = x_raw
    act = _causal_conv(ext_ref, q, cw_ref, cb_ref)
    ext_ref[0:8, :] = x_raw[q - 8:q, :]
    xs = act[:, 0:D_SSM]
    bm = act[:, D_SSM:D_SSM + SSM_GROUPS * D_STATE]
    cm = act[:, D_SSM + SSM_GROUPS * D_STATE:CONV_DIM]

    dt = _softplus(xd_ref[:, CONV_DIM:CONV_DIM + 128] + dtb_ref[...])
    da = dt * (-jnp.exp(alog_ref[...]))
    ti = lax.broadcasted_iota(jnp.int32, (q, q), 0)
    si = lax.broadcasted_iota(jnp.int32, (q, q), 1)
    causal = ti >= si
    a_cum = None
    for piece in _split_bf16(da, 3):
        t = jnp.dot(causal.astype(BF16), piece, preferred_element_type=F32)
        a_cum = t if a_cum is None else a_cum + t
    a_cum_t = a_cum.T
    expand = _head_lane_mask(128).astype(BF16)
    eac = jnp.exp(a_cum)
    dt_x = _dot_split(dt, expand, 2)
    eac_x = _dot_split(eac, expand, 2)
    dend_x = _dot_split(jnp.exp(a_cum[q - 1:q, :] - a_cum), expand, 2)
    xdt = xs * dt_x
    xdd = (xdt * dend_x).astype(BF16)
    xdt_b = xdt.astype(BF16)
    lane = lax.broadcasted_iota(jnp.int32, (q, 2 * SSM_HEAD_DIM), 1)
    lo = lane < SSM_HEAD_DIM
    heads_per_group = H_SSM // SSM_GROUPS
    for grp in range(SSM_GROUPS):
        gs = slice(grp * GROUP_W, (grp + 1) * GROUP_W)
        bm_g = bm[:, grp * D_STATE:(grp + 1) * D_STATE]
        cm_b = cm[:, grp * D_STATE:(grp + 1) * D_STATE].astype(BF16)
        cb_mat = lax.dot_general(cm_b, bm_g.astype(BF16), NT_DIMS, preferred_element_type=F32)
        h_in = ht_ref[:, gs]
        y_off = jnp.dot(cm_b, h_in.astype(BF16), preferred_element_type=F32) * eac_x[:, gs]
        states = jnp.dot(bm_g.T.astype(BF16), xdd[:, gs], preferred_element_type=F32)
        ht_ref[:, gs] = h_in * eac_x[q - 1:q, gs] + states
        for hp in range(heads_per_group // 2):
            h0 = grp * heads_per_group + 2 * hp
            ps = slice(h0 * SSM_HEAD_DIM, (h0 + 2) * SSM_HEAD_DIM)
            x_pair = xdt_b[:, ps]
            halves = []
            for half in range(2):
                h = h0 + half
                seg = a_cum[:, h:h + 1] - a_cum_t[h:h + 1, :]
                decay = jnp.exp(jnp.where(causal, seg, -jnp.inf))
                halves.append(jnp.dot((cb_mat * decay).astype(BF16), x_pair, preferred_element_type=F32))
            y_diag = jnp.where(lo, halves[0], halves[1])
            off = slice(2 * hp * SSM_HEAD_DIM, (2 * hp + 2) * SSM_HEAD_DIM)
            y_ref[:, ps] = y_diag + y_off[:, off] + dsk_ref[:, ps] * xs[:, ps]
    y_ref[...] = _gated_group_norm(y_ref[...], z_ref[...], g_ref[...])

    @pl.when(c == pl.num_programs(1) - 1)
    def _():
        hout_ref[...] = ht_ref[...].T


def _ssd_prompt_call(proj, sp):
    b, s_len, _ = proj.shape
    q = SSD_CHUNK
    const = lambda shape: pl.BlockSpec(shape, lambda bi, c: (0,) * len(shape))
    y, h_last = pl.pallas_call(
        _ssd_prompt_kernel,
        out_shape=(jax.ShapeDtypeStruct((b, s_len, D_SSM), F32),
                   jax.ShapeDtypeStruct((b, D_SSM, D_STATE), F32)),
        grid=(b, s_len // q),
        in_specs=[pl.BlockSpec((None, q, D_SSM), lambda bi, c: (bi, c, 3)),
                  pl.BlockSpec((None, q, 2048), lambda bi, c: (bi, c, 2)),
                  const((CONV_W, CONV_DIM)), const((1, CONV_DIM)), const((1, 128)), const((1, 128)),
                  const((1, D_SSM)), const((1, D_SSM))],
        out_specs=(pl.BlockSpec((None, q, D_SSM), lambda bi, c: (bi, c, 0)),
                   pl.BlockSpec((None, D_SSM, D_STATE), lambda bi, c: (bi, 0, 0))),
        scratch_shapes=[pltpu.VMEM((D_STATE, D_SSM), F32), pltpu.VMEM((8 + q, CONV_DIM), F32)],
        compiler_params=pltpu.CompilerParams(
            dimension_semantics=("parallel", "arbitrary"), vmem_limit_bytes=VMEM_LIMIT_BYTES),
        name="ssd_prompt",
    )(proj, proj, sp['conv_w'], sp['conv_b'], sp['dt_bias'], sp['a_log'], sp['d_skip_x'], sp['g_ssm_out'])
    return y.reshape(b * s_len, D_SSM), h_last.reshape(b, H_SSM, SSM_HEAD_DIM, D_STATE)


def _ssd_sample_kernel(z_ref, xd_ref, buf_ref, h0_ref, cw_ref, cb_ref, dtb_ref, alog_ref, dsk_ref, g_ref,
                       y_ref, hout_ref, ext_ref):
    n_seq, n_tok, _ = z_ref.shape
    expand = _head_lane_mask(128).astype(BF16)
    ti = lax.broadcasted_iota(jnp.int32, (8, 8), 0)
    si = lax.broadcasted_iota(jnp.int32, (8, 8), 1)
    causal = (ti >= si) & (si < n_tok)
    real = lax.broadcasted_iota(jnp.int32, (8, 1), 0) < n_tok
    ones = jnp.ones((8, D_STATE), BF16)
    ext_ref[...] = jnp.zeros_like(ext_ref)

    def one_sequence(j, ext_ref):
        ext_ref[8 - (CONV_W - 1):8, :] = buf_ref[j]
        ext_ref[8:8 + n_tok, :] = xd_ref[j][:, 0:CONV_DIM]
        act = _causal_conv(ext_ref, 8, cw_ref, cb_ref)
        xs = act[:, 0:D_SSM]
        bm = act[:, D_SSM:D_SSM + SSM_GROUPS * D_STATE]
        cm = act[:, D_SSM + SSM_GROUPS * D_STATE:CONV_DIM]
        pad = jnp.zeros((8 - n_tok, 128), F32)
        dt = _softplus(jnp.concatenate([xd_ref[j][:, CONV_DIM:CONV_DIM + 128], pad], axis=0) + dtb_ref[...])
        da = jnp.where(real, dt * (-jnp.exp(alog_ref[...])), 0.0)
        a_cum = None
        for piece in _split_bf16(da, 3):
            t = jnp.dot((ti >= si).astype(BF16), piece, preferred_element_type=F32)
            a_cum = t if a_cum is None else a_cum + t
        a_last = a_cum[n_tok - 1:n_tok, :]
        acum_x = _dot_split(a_cum, expand, 3)
        eac_x = jnp.exp(acum_x)
        dt_x = _dot_split(dt, expand, 2)
        xdt = jnp.where(real, xs * dt_x, 0.0)
        xdd = (xdt * jnp.exp(acum_x[n_tok - 1:n_tok, :] - acum_x)).astype(BF16)
        last_x = jnp.where(lax.broadcasted_iota(jnp.int32, (8, 1), 0) == n_tok - 1, eac_x, 0.0)
        keep = _dot_split(last_x, ones, 2, dims=TN_DIMS)
        y_parts, new_state = [], []
        for grp in range(SSM_GROUPS):
            gs = slice(grp * GROUP_W, (grp + 1) * GROUP_W)
            bm_b = bm[:, grp * D_STATE:(grp + 1) * D_STATE].astype(BF16)
            cm_b = cm[:, grp * D_STATE:(grp + 1) * D_STATE].astype(BF16)
            cb_mat = jnp.where(causal, lax.dot_general(cm_b, bm_b, NT_DIMS, preferred_element_type=F32), 0.0)
            h0_g = h0_ref[j, gs, :]
            y_g = lax.dot_general(cm_b, h0_g.astype(BF16), NT_DIMS, preferred_element_type=F32) * eac_x[:, gs]
            for s in range(n_tok):
                coef = cb_mat[:, s:s + 1] * jnp.exp(jnp.where(ti[:, 0:1] >= s, acum_x[:, gs] - acum_x[s:s + 1, gs], -jnp.inf))
                y_g = y_g + coef * xdt[s:s + 1, gs]
            y_parts.append(y_g)
            states = lax.dot_general(xdd[:, gs], bm_b, TN_DIMS, preferred_element_type=F32)
            hout_ref[j, gs, :] = h0_g * keep[gs, :] + states
        y = jnp.concatenate(y_parts, axis=1) + dsk_ref[...] * xs
        zj = jnp.concatenate([z_ref[j], jnp.zeros((8 - n_tok, D_SSM), F32)], axis=0)
        y_ref[j] = _gated_group_norm(y, zj, g_ref[...])[0:n_tok, :]

    def body(jj, carry):
        for slot in range(2):
            one_sequence(2 * jj + slot, ext_ref.at[slot])
        return carry

    lax.fori_loop(0, n_seq // 2, body, 0)


def _ssd_sample_call(proj, conv_buf, h0, sp):
    b, n_tok, _ = proj.shape
    nseq = 8
    const = lambda shape: pl.BlockSpec(shape, lambda i: (0,) * len(shape))
    y, h_last = pl.pallas_call(
        _ssd_sample_kernel,
        out_shape=(jax.ShapeDtypeStruct((b, n_tok, D_SSM), F32),
                   jax.ShapeDtypeStruct((b, D_SSM, D_STATE), F32)),
        grid=(b // nseq,),
        in_specs=[pl.BlockSpec((nseq, n_tok, D_SSM), lambda i: (i, 0, 3)),
                  pl.BlockSpec((nseq, n_tok, 2048), lambda i: (i, 0, 2)),
                  pl.BlockSpec((nseq, CONV_W - 1, CONV_DIM), lambda i: (i, 0, 0)),
                  pl.BlockSpec((nseq, D_SSM, D_STATE), lambda i: (i, 0, 0)),
                  const((CONV_W, CONV_DIM)), const((1, CONV_DIM)), const((1, 128)), const((1, 128)),
                  const((1, D_SSM)), const((1, D_SSM))],
        out_specs=(pl.BlockSpec((nseq, n_tok, D_SSM), lambda i: (i, 0, 0)),
                   pl.BlockSpec((nseq, D_SSM, D_STATE), lambda i: (i, 0, 0))),
        scratch_shapes=[pltpu.VMEM((2, 16, CONV_DIM), F32)],
        compiler_params=pltpu.CompilerParams(
            dimension_semantics=("parallel",), vmem_limit_bytes=VMEM_LIMIT_BYTES),
        name="ssd_sample",
    )(proj, proj, conv_buf, h0.reshape(b, D_SSM, D_STATE), sp['conv_w'], sp['conv_b'], sp['dt_bias'],
      sp['a_log'], sp['d_skip_x'], sp['g_ssm_out'])
    return y.reshape(b * n_tok, D_SSM), h_last.reshape(b, H_SSM, SSM_HEAD_DIM, D_STATE)


def _split_proj(proj, b, L):
    proj = proj.reshape(b, L, D_IN_PAD)
    k = proj[..., D_ATT:2 * D_ATT].reshape(b, L, H_ATT, HEAD_DIM)
    v = proj[..., 2 * D_ATT:3 * D_ATT].reshape(b, L, H_ATT, HEAD_DIM)
    o = 3 * D_ATT + D_SSM
    return proj, k, v, proj[:, L - (CONV_W - 1):, o:o + CONV_DIM]


def kernel(x_prompt, x_sample, cache_k, cache_v, state_ssm, state_conv, c_prompt, c_sample,
           rpb_table, w_ada, b_ada, g_ffn1, w_gate1, w_up1, w_down1, g_mix, w_in, conv_w, conv_b,
           dt_bias, a_log, d_skip, g_ssm_out, g_attn_out, w_out, g_ffn2, w_gate2, w_up2, w_down2, g_final):
    bp, sp, _ = x_prompt.shape
    bs, ls, _ = x_sample.shape
    wbp = min(WINDOW_MAX, sp)
    layer = 0
    w_in_pad = jnp.pad(w_in[layer].astype(BF16), ((0, 0), (0, D_IN_PAD - D_IN)))
    p = {'g_ffn1': g_ffn1[layer], 'w_gate1': w_gate1[layer].astype(BF16), 'w_up1': w_up1[layer].astype(BF16),
         'w_down1': w_down1[layer].astype(BF16), 'g_mix': g_mix[layer], 'w_in_pad': w_in_pad,
         'conv_w': conv_w[layer], 'conv_b': conv_b[layer].reshape(1, CONV_DIM),
         'dt_bias': jnp.pad(dt_bias[layer], (0, 128 - H_SSM)).reshape(1, 128),
         'a_log': jnp.pad(a_log[layer], (0, 128 - H_SSM)).reshape(1, 128),
         'd_skip_x': jnp.repeat(d_skip[layer], SSM_HEAD_DIM).reshape(1, D_SSM),
         'g_ssm_out': g_ssm_out[layer].reshape(1, D_SSM), 'g_attn_out': g_attn_out[layer],
         'w_out_att': w_out[layer, :D_ATT].astype(BF16), 'w_out_ssm': w_out[layer, D_ATT:].astype(BF16),
         'g_ffn2': g_ffn2[layer], 'w_gate2': w_gate2[layer].astype(BF16), 'w_up2': w_up2[layer].astype(BF16),
         'w_down2': w_down2[layer].astype(BF16)}

    n_c = bp + bs
    c_rows = -(-n_c // 16) * 16
    c_all = jnp.pad(jnp.concatenate([c_prompt, c_sample], axis=0), ((0, c_rows - n_c), (0, 0)))
    mod = _mod_call(c_all, w_ada[layer], b_ada[layer])
    mods_p = [mod[:bp, i * D_MODEL:(i + 1) * D_MODEL].reshape(bp, 1, D_MODEL) for i in range(N_MOD)]
    mods_s = [jnp.repeat(mod[bp:n_c, i * D_MODEL:(i + 1) * D_MODEL], ls, axis=0) for i in range(N_MOD)]

    lb_c, lb_n = _sample_log_bias(rpb_table, ls, cache_k.shape[2])
    cache_kt = jnp.transpose(cache_k[layer], (0, 2, 3, 1))
    cache_vt = jnp.transpose(cache_v[layer], (0, 2, 3, 1))
    dils = tuple(d for _, d in DILATED_PATTERNS)
    sh1, sc1, g1, sh2, sc2, g2, sh3, sc3, g3 = mods_p
    th1, tc1, t1, th2, tc2, t2, th3, tc3, t3 = mods_s
    ffn1 = (p['g_ffn1'], p['w_gate1'], p['w_up1'], p['w_down1'])
    ffn2 = (p['g_ffn2'], p['w_gate2'], p['w_up2'], p['w_down2'])
    kw_p = dict(tm=1024, rows_per_mod=sp)
    kw_s = dict(tm=bs * ls, rows_per_mod=ls)

    xs = _ffn_call(x_sample.reshape(bs * ls, D_MODEL), ffn1[0], th1, tc1, t1, *ffn1[1:], g_final,
                   final_norm=False, **kw_s)
    (proj_s,) = _inproj_call(xs, p['g_mix'], th2, tc2, p['w_in_pad'], dilations=(), **kw_s)
    proj_s, kn, vn, cs = _split_proj(proj_s, bs, ls)
    rider = lambda first, n: (proj_s, cache_kt, cache_vt, lb_c, lb_n, first, n)

    xp, att_a = _ffn_call(x_prompt.reshape(bp * sp, D_MODEL), ffn1[0], sh1, sc1, g1, *ffn1[1:], g_final,
                          final_norm=False, rider=rider(0, bs // 2), **kw_p)
    proj_p, *qkv = _inproj_call(xp, p['g_mix'], sh2, sc2, p['w_in_pad'], dilations=dils,
                                tm=512, rows_per_mod=sp)
    proj_p, kp, vp, cp = _split_proj(proj_p, bp, sp)
    outs, lses = [], []
    for (window, dil), qkv_d in zip(DILATED_PATTERNS, qkv):
        o, lse = _attn_prompt_call(qkv_d, _prompt_bias(rpb_table, window, dil), dil, bp)
        outs.append(o)
        lses.append(lse)
    ssd_p, hp = _ssd_prompt_call(proj_p, p)
    xp = _outproj_call(xp, outs + lses, ssd_p, g2, p['w_out_att'], p['w_out_ssm'], p['g_attn_out'],
                       tm=256, rows_per_mod=sp, dilations=dils)
    yp, att_b = _ffn_call(xp, ffn2[0], sh3, sc3, g3, *ffn2[1:], g_final,
                          final_norm=True, rider=rider(bs // 2, bs - bs // 2), **kw_p)

    att_s = jnp.concatenate([att_a, att_b], axis=0).reshape(bs * ls, D_ATT)
    ssd_s, hs = _ssd_sample_call(proj_s, state_conv[layer], state_ssm[layer], p)
    xs = _outproj_call(xs, [att_s], ssd_s, t2, p['w_out_att'], p['w_out_ssm'], p['g_attn_out'],
                       tm=256, rows_per_mod=ls, dilations=())
    ys = _ffn_call(xs, ffn2[0], th3, tc3, t3, *ffn2[1:], g_final, final_norm=True, **kw_s)
    return (yp.reshape(bp, sp, D_MODEL), ys.reshape(bs, ls, D_MODEL), kp[None, :, -wbp:], vp[None, :, -wbp:],
            kn[None], vn[None], hp[None], hs[None], cp[None], cs[None])
```

```python
import functools
import math

import jax
import jax.numpy as jnp
from jax import lax
from jax.experimental import pallas as pl
from jax.experimental.pallas import tpu as pltpu

D_MODEL = 2048
HEAD_DIM = 64
D_ATT = 1024
D_SSM = 1024
H_ATT = 16
SSM_HEAD_DIM = 64
H_SSM = 16
SSM_GROUPS = 2
D_STATE = 128
CONV_W = 4
SSD_CHUNK = 128
CONV_DIM = D_SSM + 2 * SSM_GROUPS * D_STATE
D_FF = 5632
DILATED_PATTERNS = ((128, 1), (512, 4), (2048, 16))
WINDOW_MAX = 2048
ATT_BLOCK = 128
RPB_BUCKETS = 32
RPB_MAX_DIST = 2048
NORM_EPS = 1e-6
N_MOD = 9
PROJ_SIZES = (D_ATT, D_ATT, D_ATT, D_SSM, CONV_DIM, H_SSM)
D_IN = sum(PROJ_SIZES)
D_IN_PAD = 6144

VMEM_LIMIT_BYTES = 56 * 1024 * 1024

BF16 = jnp.bfloat16
F32 = jnp.float32


def _silu(v):
    return v * jax.nn.sigmoid(v)


def _norm_mod(x, g, shift, scale):
    y = x * lax.rsqrt(jnp.mean(x * x, axis=-1, keepdims=True) + NORM_EPS)
    return (y * g) * (1.0 + scale) + shift


def _mod_kernel(c_ref, w_ref, b_ref, o_ref, a_ref):
    @pl.when(pl.program_id(0) == 0)
    def _():
        a_ref[...] = _silu(c_ref[...]).astype(BF16)

    o_ref[...] = jnp.dot(a_ref[...], w_ref[...].astype(BF16),
                         preferred_element_type=F32) + b_ref[...]


def _mod_call(c_all, w_ada, b_ada):
    rows = c_all.shape[0]
    n = w_ada.shape[1]
    tn = 1024
    return pl.pallas_call(
        _mod_kernel,
        out_shape=jax.ShapeDtypeStruct((rows, n), F32),
        grid=(n // tn,),
        in_specs=[
            pl.BlockSpec((rows, D_MODEL), lambda j: (0, 0)),
            pl.BlockSpec((D_MODEL, tn), lambda j: (0, j)),
            pl.BlockSpec((1, tn), lambda j: (0, j)),
        ],
        out_specs=pl.BlockSpec((rows, tn), lambda j: (0, j)),
        scratch_shapes=[pltpu.VMEM((rows, D_MODEL), BF16)],
        compiler_params=pltpu.CompilerParams(
            dimension_semantics=("arbitrary",), vmem_limit_bytes=VMEM_LIMIT_BYTES),
        name="adaln_mod",
    )(c_all, w_ada, b_ada.reshape(1, n))


NT_DIMS = (((1,), (1,)), ((), ()))
RIDER_HEADS = 8


def _sample_attn_heads(q_ref, kn_ref, vn_ref, kt_ref, vt_ref, lbc_ref, lbn_ref, o_ref):
    n_q, width = q_ref.shape
    pad = jnp.zeros((8 - n_q, width), F32)
    q8 = jnp.concatenate([q_ref[...] * (HEAD_DIM ** -0.5), pad], axis=0).astype(BF16)
    kn8 = jnp.concatenate([kn_ref[...], pad], axis=0).astype(BF16)
    vn8 = jnp.concatenate([vn_ref[...], pad], axis=0).astype(BF16)
    heads = width // HEAD_DIM
    wb = kt_ref.shape[2]
    hs = lambda h: slice(h * HEAD_DIM, (h + 1) * HEAD_DIM)
    rows = lambda h: slice(h * 8, (h + 1) * 8)
    t_c = jnp.concatenate([jnp.dot(q8[:, hs(h)], kt_ref[h].astype(BF16), preferred_element_type=F32)
                           for h in range(heads)], axis=0) + lbc_ref[...].reshape(heads * 8, wb)
    t_n = jnp.concatenate([lax.dot_general(q8[:, hs(h)], kn8[:, hs(h)], NT_DIMS, preferred_element_type=F32)
                           for h in range(heads)], axis=0) + lbn_ref[...].reshape(heads * 8, 8)
    m = jnp.maximum(jnp.max(t_c, axis=-1, keepdims=True), jnp.max(t_n, axis=-1, keepdims=True))
    p_c = jnp.exp(t_c - m)
    p_n = jnp.exp(t_n - m)
    inv_l = 1.0 / (jnp.sum(p_c, axis=-1, keepdims=True) + jnp.sum(p_n, axis=-1, keepdims=True))
    for h in range(heads):
        acc = lax.dot_general(p_c[rows(h)].astype(BF16), vt_ref[h].astype(BF16), NT_DIMS,
                              preferred_element_type=F32)
        acc += jnp.dot(p_n[rows(h)].astype(BF16), vn8[:, hs(h)], preferred_element_type=F32)
        o_ref[:, hs(h)] = (acc * inv_l[rows(h)])[0:n_q, :]


def _ffn_kernel(*refs, final_norm, rider):
    x_ref, g_ref, sh_ref, sc_ref, gt_ref, wg_ref, wu_ref, wd_ref, gf_ref = refs[:9]
    if rider:
        rider_in, (o_ref, att_ref, h_ref) = refs[9:16], refs[16:]
    else:
        o_ref, h_ref = refs[9:]
    j = pl.program_id(1)

    @pl.when(j == 0)
    def _():
        h_ref[...] = _norm_mod(x_ref[...], g_ref[...], sh_ref[...], sc_ref[...]).astype(BF16)
        o_ref[...] = jnp.zeros_like(o_ref)

    h = h_ref[...]
    a = jnp.dot(h, wg_ref[...], preferred_element_type=F32)
    b = jnp.dot(h, wu_ref[...], preferred_element_type=F32)
    s = (_silu(a) * b).astype(BF16)
    o_ref[...] += jnp.dot(s, wd_ref[...], preferred_element_type=F32)
    if rider:
        @pl.when(pl.program_id(0) * pl.num_programs(1) + j < rider)
        def _():
            _sample_attn_heads(*rider_in, att_ref)

    @pl.when(j == pl.num_programs(1) - 1)
    def _():
        y = x_ref[...] + 0.5 * gt_ref[...] * o_ref[...]
        if final_norm:
            y = y * lax.rsqrt(jnp.mean(y * y, axis=-1, keepdims=True) + NORM_EPS) * gf_ref[...]
        o_ref[...] = y


def _mod_spec(mod, tm, rows_per_mod):
    if mod.ndim == 3:
        tiles_per_row = rows_per_mod // tm
        return pl.BlockSpec((None, 1, D_MODEL), lambda i, j: (i // tiles_per_row, 0, 0))
    return pl.BlockSpec((tm, D_MODEL), lambda i, j: (i, 0))


def _ffn_call(x, g, sh, sc, gt, wg, wu, wd, g_final, *, tm, rows_per_mod, final_norm, rider=None):
    m = x.shape[0]
    tf = 256 if rider else 512
    nj = D_FF // tf
    row = lambda i, j: (i, 0)
    const = lambda i, j: (0, 0)
    in_specs = [
        pl.BlockSpec((tm, D_MODEL), row, pipeline_mode=pl.Buffered(1)),
        pl.BlockSpec((1, D_MODEL), const),
        _mod_spec(sh, tm, rows_per_mod),
        _mod_spec(sc, tm, rows_per_mod),
        _mod_spec(gt, tm, rows_per_mod),
        pl.BlockSpec((D_MODEL, tf), lambda i, j: (0, j)),
        pl.BlockSpec((D_MODEL, tf), lambda i, j: (0, j)),
        pl.BlockSpec((tf, D_MODEL), lambda i, j: (j, 0)),
        pl.BlockSpec((1, D_MODEL), const),
    ]
    args = [x, g.reshape(1, D_MODEL), sh, sc, gt, wg, wu, wd, g_final.reshape(1, D_MODEL)]
    out_shape = jax.ShapeDtypeStruct((m, D_MODEL), F32)
    out_specs = pl.BlockSpec((tm, D_MODEL), row)
    if rider:
        proj_s, cache_kt, cache_vt, lb_c, lb_n, first_seq, n_seq = rider
        n_q = proj_s.shape[1]
        wb = cache_kt.shape[3]
        halves = H_ATT // RIDER_HEADS
        n_units = n_seq * halves
        assert n_units <= (m // tm) * nj
        rw = RIDER_HEADS * HEAD_DIM
        unit = lambda i, j: jnp.minimum(i * nj + j, n_units - 1)
        seq = lambda i, j: first_seq + unit(i, j) // halves
        half = lambda i, j: unit(i, j) % halves
        tiles = D_ATT // rw
        qspec = lambda t: pl.BlockSpec((None, n_q, rw), lambda i, j: (seq(i, j), 0, t * tiles + half(i, j)))
        cspec = pl.BlockSpec((None, RIDER_HEADS, HEAD_DIM, wb), lambda i, j: (seq(i, j), half(i, j), 0, 0))
        in_specs += [qspec(0), qspec(1), qspec(2), cspec, cspec,
                     pl.BlockSpec((RIDER_HEADS, 8, wb), lambda i, j: (half(i, j), 0, 0)),
                     pl.BlockSpec((RIDER_HEADS, 8, 8), lambda i, j: (half(i, j), 0, 0))]
        args += [proj_s, proj_s, proj_s, cache_kt, cache_vt, lb_c, lb_n]
        out_shape = (out_shape, jax.ShapeDtypeStruct((n_seq, n_q, D_ATT), F32))
        out_specs = (pl.BlockSpec((tm, D_MODEL), row, pipeline_mode=pl.Buffered(1)),
                     pl.BlockSpec((None, n_q, rw), lambda i, j: (seq(i, j) - first_seq, 0, half(i, j))))
    return pl.pallas_call(
        functools.partial(_ffn_kernel, final_norm=final_norm, rider=n_units if rider else 0),
        out_shape=out_shape,
        grid=(m // tm, nj),
        in_specs=in_specs,
        out_specs=out_specs,
        scratch_shapes=[pltpu.VMEM((tm, D_MODEL), BF16)],
        compiler_params=pltpu.CompilerParams(
            dimension_semantics=("arbitrary", "arbitrary") if rider else ("parallel", "arbitrary"),
            vmem_limit_bytes=VMEM_LIMIT_BYTES),
        name=("ffn_final" if final_norm else "ffn") + ("_rider" if rider else ""),
    )(*args)


QKV_TILES = 3


def _inproj_kernel(x_ref, g_ref, sh_ref, sc_ref, w_ref, o_ref, *rest, dilations):
    qkv_refs, h_ref, lanes_ref = rest[:-2], rest[-2], rest[-1]
    j = pl.program_id(1)
    tm = o_ref.shape[0]

    @pl.when(j == 0)
    def _():
        h_ref[...] = _norm_mod(x_ref[...], g_ref[...], sh_ref[...], sc_ref[...]).astype(BF16)

    res = jnp.dot(h_ref[...], w_ref[...], preferred_element_type=F32)
    o_ref[...] = res

    @pl.when(j < QKV_TILES)
    def _():
        n_chunks = D_ATT // 128
        if any(d > 1 for d in dilations):
            for c in range(n_chunks):
                lanes_ref[c] = res[:, c * 128:(c + 1) * 128]
        for d, ref in zip(dilations, qkv_refs):
            if d == 1:
                ref[...] = res.astype(BF16)
                continue
            for r in range(d):
                for c in range(n_chunks):
                    rows = lanes_ref[c, pl.ds(r, tm // d, stride=d), :]
                    ref[:, r * D_ATT + c * 128:r * D_ATT + (c + 1) * 128] = rows.astype(BF16)


def _inproj_call(x, g, sh, sc, w_pad, *, tm, rows_per_mod, dilations):
    m = x.shape[0]
    tn = D_ATT
    qkv_j = lambda j: jnp.minimum(j, QKV_TILES - 1)
    return pl.pallas_call(
        functools.partial(_inproj_kernel, dilations=dilations),
        out_shape=(jax.ShapeDtypeStruct((m, D_IN_PAD), F32),)
        + tuple(jax.ShapeDtypeStruct((m // d, QKV_TILES * d * tn), BF16) for d in dilations),
        grid=(m // tm, D_IN_PAD // tn),
        in_specs=[
            pl.BlockSpec((tm, D_MODEL), lambda i, j: (i, 0), pipeline_mode=pl.Buffered(1)),
            pl.BlockSpec((1, D_MODEL), lambda i, j: (0, 0)),
            _mod_spec(sh, tm, rows_per_mod),
            _mod_spec(sc, tm, rows_per_mod),
            pl.BlockSpec((D_MODEL, tn), lambda i, j: (0, j)),
        ],
        out_specs=(pl.BlockSpec((tm, tn), lambda i, j: (i, j)),)
        + tuple(pl.BlockSpec((tm // d, d * tn), lambda i, j: (i, qkv_j(j))) for d in dilations),
        scratch_shapes=[pltpu.VMEM((tm, D_MODEL), BF16), pltpu.VMEM((tn // 128, tm, 128), F32)],
        compiler_params=pltpu.CompilerParams(
            dimension_semantics=("parallel", "arbitrary"), vmem_limit_bytes=VMEM_LIMIT_BYTES),
        name="in_proj",
    )(x, g.reshape(1, D_MODEL), sh, sc, w_pad)


def _outproj_kernel(*refs, dilations):
    x_ref = refs[0]
    n_pat = len(dilations)
    if n_pat:
        o_refs = refs[1:1 + n_pat]
        l_refs = refs[1 + n_pat:1 + 2 * n_pat]
        ga_ref, ssd_ref, gt_ref, wa_ref, ws_ref, out_ref, o_scr, l_scr = refs[1 + 2 * n_pat:]
        tm = x_ref.shape[0]
        n_chunks = D_ATT // 128
        outs, lses = [], []
        for p, (d, o_ref, l_ref) in enumerate(zip(dilations, o_refs, l_refs)):
            if d == 1:
                outs.append(o_ref[...].astype(F32))
                lses.append(l_ref[...])
                continue
            for r in range(d):
                l_scr[p, pl.ds(r, tm // d, stride=d), :] = l_ref[:, r * 128:(r + 1) * 128]
                for c in range(n_chunks):
                    lanes = slice(r * D_ATT + c * 128, r * D_ATT + (c + 1) * 128)
                    o_scr[p, c, pl.ds(r, tm // d, stride=d), :] = o_ref[:, lanes].astype(F32)
            outs.append(jnp.concatenate([o_scr[p, c] for c in range(n_chunks)], axis=1))
            lses.append(l_scr[p])
        mx = functools.reduce(jnp.maximum, lses)
        ws = [jnp.exp(v - mx) for v in lses]
        wsum = functools.reduce(lambda a, b: a + b, ws)
        expand = _head_lane_mask(128).astype(BF16)
        att = None
        for w, o in zip(ws, outs):
            wn = w / wsum
            w_hi = wn.astype(BF16)
            w_lo = (wn - w_hi.astype(F32)).astype(BF16)
            wx = (jnp.dot(w_hi, expand, preferred_element_type=F32)
                  + jnp.dot(w_lo, expand, preferred_element_type=F32))
            att = wx * o if att is None else att + wx * o
    else:
        att_ref, ga_ref, ssd_ref, gt_ref, wa_ref, ws_ref, out_ref = refs[1:]
        att = att_ref[...]
    att = att * lax.rsqrt(jnp.mean(att * att, axis=-1, keepdims=True) + NORM_EPS) * ga_ref[...]
    mix = jnp.dot(att.astype(BF16), wa_ref[...], preferred_element_type=F32)
    mix += jnp.dot(ssd_ref[...].astype(BF16), ws_ref[...], preferred_element_type=F32)
    out_ref[...] = x_ref[...] + gt_ref[...] * mix


def _outproj_call(x, att_parts, ssd, gt, w_att, w_ssd, g_attn, *, tm, rows_per_mod, dilations):
    m = x.shape[0]
    n_pat = len(dilations)
    if gt.ndim == 3:
        tiles_per_row = rows_per_mod // tm
        gt_spec = pl.BlockSpec((None, 1, D_MODEL), lambda i: (i // tiles_per_row, 0, 0))
    else:
        gt_spec = pl.BlockSpec((tm, D_MODEL), lambda i: (i, 0))
    row = lambda width, d=1: pl.BlockSpec((tm // d, d * width), lambda i: (i, 0))
    scratch = []
    if n_pat:
        att_specs = ([row(D_ATT, d) for d in dilations] + [row(128, d) for d in dilations]
                     + [pl.BlockSpec((1, D_ATT), lambda i: (0, 0))])
        att_args = list(att_parts) + [g_attn.reshape(1, D_ATT)]
        scratch = [pltpu.VMEM((n_pat, D_ATT // 128, tm, 128), F32), pltpu.VMEM((n_pat, tm, 128), F32)]
    else:
        att_specs = [row(D_ATT), pl.BlockSpec((1, D_ATT), lambda i: (0, 0))]
        att_args = list(att_parts) + [g_attn.reshape(1, D_ATT)]
    weight = lambda k: pl.BlockSpec((k, D_MODEL), lambda i: (0, 0), pipeline_mode=pl.Buffered(1))
    return pl.pallas_call(
        functools.partial(_outproj_kernel, dilations=dilations),
        out_shape=jax.ShapeDtypeStruct((m, D_MODEL), F32),
        grid=(m // tm,),
        in_specs=[row(D_MODEL)] + att_specs + [row(D_SSM), gt_spec, weight(D_ATT), weight(D_SSM)],
        out_specs=row(D_MODEL),
        scratch_shapes=scratch,
        compiler_params=pltpu.CompilerParams(
            dimension_semantics=("parallel",), vmem_limit_bytes=VMEM_LIMIT_BYTES),
        name="out_proj_merge" if n_pat else "out_proj",
    )(x, *att_args, ssd, gt, w_att, w_ssd)


def _rel_bucket(dist):
    max_exact = RPB_BUCKETS // 2
    df = jnp.maximum(dist, 1).astype(jnp.float32)
    large = max_exact + (jnp.log(df / max_exact) / math.log(RPB_MAX_DIST / max_exact)
                         * (RPB_BUCKETS - max_exact)).astype(jnp.int32)
    large = jnp.minimum(large, RPB_BUCKETS - 1)
    return jnp.where(dist < max_exact, dist, large)


def _head_lane_mask(rows=H_ATT):
    r = lax.broadcasted_iota(jnp.int32, (rows, D_ATT), 0)
    c = lax.broadcasted_iota(jnp.int32, (rows, D_ATT), 1)
    return (c // HEAD_DIM == r).astype(F32)


def _attn_prompt_kernel(q_ref, kp_ref, kc_ref, vp_ref, vc_ref, bias_ref, o_ref, lse_ref):
    nb = pl.program_id(2)
    blk = ATT_BLOCK
    col = lax.broadcasted_iota(jnp.int32, (blk, 2 * blk), 1)
    has_prev = (col >= blk) | (nb > 0)
    lane = lax.broadcasted_iota(jnp.int32, (blk, 2 * HEAD_DIM), 1)
    lo = lane < HEAD_DIM
    lse_lane = lax.broadcasted_iota(jnp.int32, (blk, 128), 1)
    lse_tile = jnp.zeros((blk, 128), F32)
    scale = HEAD_DIM ** -0.5
    for hp in range(H_ATT // 2):
        sl = slice(hp * 2 * HEAD_DIM, (hp + 1) * 2 * HEAD_DIM)
        q2 = q_ref[:, sl] * scale
        k2 = jnp.concatenate([kp_ref[:, sl], kc_ref[:, sl]], axis=0)
        v2 = jnp.concatenate([vp_ref[:, sl], vc_ref[:, sl]], axis=0)
        outs = []
        for half in range(2):
            h = 2 * hp + half
            qh = jnp.where(lo if half == 0 else ~lo, q2, jnp.zeros_like(q2))
            s = lax.dot_general(qh, k2, NT_DIMS, preferred_element_type=F32) + bias_ref[h]
            s = jnp.where(has_prev, s, -jnp.inf)
            m = jnp.max(s, axis=-1, keepdims=True)
            p = jnp.exp(s - m)
            l = jnp.sum(p, axis=-1, keepdims=True)
            pv = jnp.dot(p.astype(BF16), v2, preferred_element_type=F32)
            outs.append(pv / l)
            lse_tile = jnp.where(lse_lane == h, m + jnp.log(l), lse_tile)
        o_ref[:, sl] = jnp.where(lo, outs[0], outs[1]).astype(o_ref.dtype)
    lse_ref[...] = lse_tile


def _attn_prompt_call(qkv, bias, dil, b):
    blk = ATT_BLOCK
    n_sub = qkv.shape[0] // b
    s_len = n_sub * dil
    nbk = n_sub // blk
    pv = qkv.reshape(b, n_sub, qkv.shape[1])
    cur = lambda t: (lambda bi, r, nb: (bi, nb, t * dil + r))
    prev = lambda t: (lambda bi, r, nb: (bi, jnp.maximum(nb - 1, 0), t * dil + r))
    spec = lambda f: pl.BlockSpec((None, blk, D_ATT), f)
    o, lse = pl.pallas_call(
        _attn_prompt_kernel,
        out_shape=(jax.ShapeDtypeStruct((b, n_sub, dil * D_ATT), BF16),
                   jax.ShapeDtypeStruct((b, n_sub, dil * 128), F32)),
        grid=(b, dil, nbk),
        in_specs=[spec(cur(0)), spec(prev(1)), spec(cur(1)), spec(prev(2)), spec(cur(2)),
                  pl.BlockSpec((H_ATT, blk, 2 * blk), lambda bi, r, nb: (0, 0, 0))],
        out_specs=(pl.BlockSpec((None, blk, D_ATT), lambda bi, r, nb: (bi, nb, r)),
                   pl.BlockSpec((None, blk, 128), lambda bi, r, nb: (bi, nb, r))),
        compiler_params=pltpu.CompilerParams(
            dimension_semantics=("parallel", "parallel", "arbitrary"),
            vmem_limit_bytes=VMEM_LIMIT_BYTES),
        name=f"attn_prompt_d{dil}",
    )(pv, pv, pv, pv, pv, bias)
    return o.reshape(b * n_sub, dil * D_ATT), lse.reshape(b * n_sub, dil * 128)


def _prompt_bias(rpb, window, dil):
    span = window // dil
    blk = ATT_BLOCK
    assert span == blk
    table = rpb[_rel_bucket(jnp.arange(span + 1) * dil)].T.astype(F32)
    fill = jnp.full((H_ATT, blk), -jnp.inf, F32)
    row = jnp.concatenate([fill, table[:, ::-1], fill], axis=1)
    wrapped = jnp.tile(row, (1, blk))[:, :blk * 3 * blk].reshape(H_ATT, blk, 3 * blk)
    return wrapped[:, :, blk:]


def _sample_log_bias(rpb, n_q, wb):
    reach = max(w for w, _ in DILATED_PATTERNS) + 2 * 8
    assert wb + n_q <= reach
    terms = []
    for window, dil in DILATED_PATTERNS:
        span = window // dil
        table = rpb[_rel_bucket(jnp.arange(span + 1) * dil)].T.astype(F32)
        gaps = jnp.full((H_ATT, span + 1, dil - 1), -jnp.inf, F32)
        by_dist = jnp.concatenate([table[:, :, None], gaps], axis=2).reshape(H_ATT, (span + 1) * dil)
        by_dist = by_dist[:, :reach]
        terms.append(jnp.pad(by_dist, ((0, 0), (0, reach - by_dist.shape[1])), constant_values=-jnp.inf))
    lb = jax.nn.logsumexp(jnp.stack(terms), axis=0)
    neg = jnp.full((H_ATT,), -jnp.inf, F32)
    rows_c, rows_n = [], []
    for r in range(8):
        n = r % n_q
        rows_c.append(lb[:, n + 1:n + 1 + wb][:, ::-1])
        rows_n.append(jnp.stack([lb[:, n - m] if (m < n_q and m <= n) else neg for m in range(8)], axis=1))
    return jnp.stack(rows_c, axis=1), jnp.stack(rows_n, axis=1)


TN_DIMS = (((0,), (0,)), ((), ()))
GROUP_W = D_SSM // SSM_GROUPS


def _split_bf16(v, pieces):
    out = []
    for _ in range(pieces):
        hi = v.astype(BF16)
        out.append(hi)
        v = v - hi.astype(F32)
    return out


def _dot_split(v, rhs_bf16, pieces, dims=None):
    acc = None
    for piece in _split_bf16(v, pieces):
        if dims is None:
            t = jnp.dot(piece, rhs_bf16, preferred_element_type=F32)
        else:
            t = lax.dot_general(piece, rhs_bf16, dims, preferred_element_type=F32)
        acc = t if acc is None else acc + t
    return acc


def _softplus(v):
    return jnp.maximum(v, 0.0) + jnp.log1p(jnp.exp(-jnp.abs(v)))


def _causal_conv(ext_ref, rows, cw_ref, cb_ref):
    conv = cb_ref[...]
    for k in range(CONV_W):
        conv = conv + cw_ref[CONV_W - 1 - k:CONV_W - k, :] * ext_ref[8 - k:8 - k + rows, :]
    return _silu(conv)


def _gated_group_norm(y, z, g):
    gated = y * _silu(z)
    parts = []
    for grp in range(SSM_GROUPS):
        seg = gated[:, grp * GROUP_W:(grp + 1) * GROUP_W]
        parts.append(seg * lax.rsqrt(jnp.mean(seg * seg, axis=-1, keepdims=True) + NORM_EPS))
    return jnp.concatenate(parts, axis=1) * g


def _ssd_prompt_kernel(z_ref, xd_ref, cw_ref, cb_ref, dtb_ref, alog_ref, dsk_ref, g_ref,
                       y_ref, hout_ref, ht_ref, ext_ref):
    c = pl.program_id(1)
    q = SSD_CHUNK

    @pl.when(c == 0)
    def _():
        ht_ref[...] = jnp.zeros_like(ht_ref)
        ext_ref[0:8, :] = jnp.zeros((8, CONV_DIM), F32)

    x_raw = xd_ref[:, 0:CONV_DIM]
    ext_ref[8:8 + q, :] = x_raw
    act = _causal_conv(ext_ref, q, cw_ref, cb_ref)
    ext_ref[0:8, :] = x_raw[q - 8:q, :]
    xs = act[:, 0:D_SSM]
    bm = act[:, D_SSM:D_SSM + SSM_GROUPS * D_STATE]
    cm = act[:, D_SSM + SSM_GROUPS * D_STATE:CONV_DIM]

    dt = _softplus(xd_ref[:, CONV_DIM:CONV_DIM + 128] + dtb_ref[...])
    da = dt * (-jnp.exp(alog_ref[...]))
    ti = lax.broadcasted_iota(jnp.int32, (q, q), 0)
    si = lax.broadcasted_iota(jnp.int32, (q, q), 1)
    causal = ti >= si
    a_cum = None
    for piece in _split_bf16(da, 3):
        t = jnp.dot(causal.astype(BF16), piece, preferred_element_type=F32)
        a_cum = t if a_cum is None else a_cum + t
    a_cum_t = a_cum.T
    expand = _head_lane_mask(128).astype(BF16)
    eac = jnp.exp(a_cum)
    dt_x = _dot_split(dt, expand, 2)
    eac_x = _dot_split(eac, expand, 2)
    dend_x = _dot_split(jnp.exp(a_cum[q - 1:q, :] - a_cum), expand, 2)
    xdt = xs * dt_x
    xdd = (xdt * dend_x).astype(BF16)
    xdt_b = xdt.astype(BF16)
    lane = lax.broadcasted_iota(jnp.int32, (q, 2 * SSM_HEAD_DIM), 1)
    lo = lane < SSM_HEAD_DIM
    heads_per_group = H_SSM // SSM_GROUPS
    for grp in range(SSM_GROUPS):
        gs = slice(grp * GROUP_W, (grp + 1) * GROUP_W)
        bm_g = bm[:, grp * D_STATE:(grp + 1) * D_STATE]
        cm_b = cm[:, grp * D_STATE:(grp + 1) * D_STATE].astype(BF16)
        cb_mat = lax.dot_general(cm_b, bm_g.astype(BF16), NT_DIMS, preferred_element_type=F32)
        h_in = ht_ref[:, gs]
        y_off = jnp.dot(cm_b, h_in.astype(BF16), preferred_element_type=F32) * eac_x[:, gs]
        states = jnp.dot(bm_g.T.astype(BF16), xdd[:, gs], preferred_element_type=F32)
        ht_ref[:, gs] = h_in * eac_x[q - 1:q, gs] + states
        for hp in range(heads_per_group // 2):
            h0 = grp * heads_per_group + 2 * hp
            ps = slice(h0 * SSM_HEAD_DIM, (h0 + 2) * SSM_HEAD_DIM)
            x_pair = xdt_b[:, ps]
            halves = []
            for half in range(2):
                h = h0 + half
                seg = a_cum[:, h:h + 1] - a_cum_t[h:h + 1, :]
                decay = jnp.exp(jnp.where(causal, seg, -jnp.inf))
                halves.append(jnp.dot((cb_mat * decay).astype(BF16), x_pair, preferred_element_type=F32))
            y_diag = jnp.where(lo, halves[0], halves[1])
            off = slice(2 * hp * SSM_HEAD_DIM, (2 * hp + 2) * SSM_HEAD_DIM)
            y_ref[:, ps] = y_diag + y_off[:, off] + dsk_ref[:, ps] * xs[:, ps]
    y_ref[...] = _gated_group_norm(y_ref[...], z_ref[...], g_ref[...])

    @pl.when(c == pl.num_programs(1) - 1)
    def _():
        hout_ref[...] = ht_ref[...].T


def _ssd_prompt_call(proj, sp):
    b, s_len, _ = proj.shape
    q = SSD_CHUNK
    const = lambda shape: pl.BlockSpec(shape, lambda bi, c: (0,) * len(shape))
    y, h_last = pl.pallas_call(
        _ssd_prompt_kernel,
        out_shape=(jax.ShapeDtypeStruct((b, s_len, D_SSM), F32),
                   jax.ShapeDtypeStruct((b, D_SSM, D_STATE), F32)),
        grid=(b, s_len // q),
        in_specs=[pl.BlockSpec((None, q, D_SSM), lambda bi, c: (bi, c, 3)),
                  pl.BlockSpec((None, q, 2048), lambda bi, c: (bi, c, 2)),
                  const((CONV_W, CONV_DIM)), const((1, CONV_DIM)), const((1, 128)), const((1, 128)),
                  const((1, D_SSM)), const((1, D_SSM))],
        out_specs=(pl.BlockSpec((None, q, D_SSM), lambda bi, c: (bi, c, 0)),
                   pl.BlockSpec((None, D_SSM, D_STATE), lambda bi, c: (bi, 0, 0))),
        scratch_shapes=[pltpu.VMEM((D_STATE, D_SSM), F32), pltpu.VMEM((8 + q, CONV_DIM), F32)],
        compiler_params=pltpu.CompilerParams(
            dimension_semantics=("parallel", "arbitrary"), vmem_limit_bytes=VMEM_LIMIT_BYTES),
        name="ssd_prompt",
    )(proj, proj, sp['conv_w'], sp['conv_b'], sp['dt_bias'], sp['a_log'], sp['d_skip_x'], sp['g_ssm_out'])
    return y.reshape(b * s_len, D_SSM), h_last.reshape(b, H_SSM, SSM_HEAD_DIM, D_STATE)


def _ssd_sample_kernel(z_ref, xd_ref, buf_ref, h0_ref, cw_ref, cb_ref, dtb_ref, alog_ref, dsk_ref, g_ref,
                       y_ref, hout_ref, ext_ref):
    n_seq, n_tok, _ = z_ref.shape
    expand = _head_lane_mask(128).astype(BF16)
    ti = lax.broadcasted_iota(jnp.int32, (8, 8), 0)
    si = lax.broadcasted_iota(jnp.int32, (8, 8), 1)
    causal = (ti >= si) & (si < n_tok)
    real = lax.broadcasted_iota(jnp.int32, (8, 1), 0) < n_tok
    ones = jnp.ones((8, D_STATE), BF16)
    ext_ref[...] = jnp.zeros_like(ext_ref)

    def one_sequence(j, ext_ref):
        ext_ref[8 - (CONV_W - 1):8, :] = buf_ref[j]
        ext_ref[8:8 + n_tok, :] = xd_ref[j][:, 0:CONV_DIM]
        act = _causal_conv(ext_ref, 8, cw_ref, cb_ref)
        xs = act[:, 0:D_SSM]
        bm = act[:, D_SSM:D_SSM + SSM_GROUPS * D_STATE]
        cm = act[:, D_SSM + SSM_GROUPS * D_STATE:CONV_DIM]
        pad = jnp.zeros((8 - n_tok, 128), F32)
        dt = _softplus(jnp.concatenate([xd_ref[j][:, CONV_DIM:CONV_DIM + 128], pad], axis=0) + dtb_ref[...])
        da = jnp.where(real, dt * (-jnp.exp(alog_ref[...])), 0.0)
        a_cum = None
        for piece in _split_bf16(da, 3):
            t = jnp.dot((ti >= si).astype(BF16), piece, preferred_element_type=F32)
            a_cum = t if a_cum is None else a_cum + t
        a_last = a_cum[n_tok - 1:n_tok, :]
        acum_x = _dot_split(a_cum, expand, 3)
        eac_x = jnp.exp(acum_x)
        dt_x = _dot_split(dt, expand, 2)
        xdt = jnp.where(real, xs * dt_x, 0.0)
        xdd = (xdt * jnp.exp(acum_x[n_tok - 1:n_tok, :] - acum_x)).astype(BF16)
        last_x = jnp.where(lax.broadcasted_iota(jnp.int32, (8, 1), 0) == n_tok - 1, eac_x, 0.0)
        keep = _dot_split(last_x, ones, 2, dims=TN_DIMS)
        y_parts, new_state = [], []
        for grp in range(SSM_GROUPS):
            gs = slice(grp * GROUP_W, (grp + 1) * GROUP_W)
            bm_b = bm[:, grp * D_STATE:(grp + 1) * D_STATE].astype(BF16)
            cm_b = cm[:, grp * D_STATE:(grp + 1) * D_STATE].astype(BF16)
            cb_mat = jnp.where(causal, lax.dot_general(cm_b, bm_b, NT_DIMS, preferred_element_type=F32), 0.0)
            h0_g = h0_ref[j, gs, :]
            y_g = lax.dot_general(cm_b, h0_g.astype(BF16), NT_DIMS, preferred_element_type=F32) * eac_x[:, gs]
            for s in range(n_tok):
                coef = cb_mat[:, s:s + 1] * jnp.exp(jnp.where(ti[:, 0:1] >= s, acum_x[:, gs] - acum_x[s:s + 1, gs], -jnp.inf))
                y_g = y_g + coef * xdt[s:s + 1, gs]
            y_parts.append(y_g)
            states = lax.dot_general(xdd[:, gs], bm_b, TN_DIMS, preferred_element_type=F32)
            hout_ref[j, gs, :] = h0_g * keep[gs, :] + states
        y = jnp.concatenate(y_parts, axis=1) + dsk_ref[...] * xs
        zj = jnp.concatenate([z_ref[j], jnp.zeros((8 - n_tok, D_SSM), F32)], axis=0)
        y_ref[j] = _gated_group_norm(y, zj, g_ref[...])[0:n_tok, :]

    def body(jj, carry):
        for slot in range(2):
            one_sequence(2 * jj + slot, ext_ref.at[slot])
        return carry

    lax.fori_loop(0, n_seq // 2, body, 0)


def _ssd_sample_call(proj, conv_buf, h0, sp):
    b, n_tok, _ = proj.shape
    nseq = 8
    const = lambda shape: pl.BlockSpec(shape, lambda i: (0,) * len(shape))
    y, h_last = pl.pallas_call(
        _ssd_sample_kernel,
        out_shape=(jax.ShapeDtypeStruct((b, n_tok, D_SSM), F32),
                   jax.ShapeDtypeStruct((b, D_SSM, D_STATE), F32)),
        grid=(b // nseq,),
        in_specs=[pl.BlockSpec((nseq, n_tok, D_SSM), lambda i: (i, 0, 3)),
                  pl.BlockSpec((nseq, n_tok, 2048), lambda i: (i, 0, 2)),
                  pl.BlockSpec((nseq, CONV_W - 1, CONV_DIM), lambda i: (i, 0, 0)),
                  pl.BlockSpec((nseq, D_SSM, D_STATE), lambda i: (i, 0, 0)),
                  const((CONV_W, CONV_DIM)), const((1, CONV_DIM)), const((1, 128)), const((1, 128)),
                  const((1, D_SSM)), const((1, D_SSM))],
        out_specs=(pl.BlockSpec((nseq, n_tok, D_SSM), lambda i: (i, 0, 0)),
                   pl.BlockSpec((nseq, D_SSM, D_STATE), lambda i: (i, 0, 0))),
        scratch_shapes=[pltpu.VMEM((2, 16, CONV_DIM), F32)],
        compiler_params=pltpu.CompilerParams(
            dimension_semantics=("parallel",), vmem_limit_bytes=VMEM_LIMIT_BYTES),
        name="ssd_sample",
    )(proj, proj, conv_buf, h0.reshape(b, D_SSM, D_STATE), sp['conv_w'], sp['conv_b'], sp['dt_bias'],
      sp['a_log'], sp['d_skip_x'], sp['g_ssm_out'])
    return y.reshape(b * n_tok, D_SSM), h_last.reshape(b, H_SSM, SSM_HEAD_DIM, D_STATE)


def _split_proj(proj, b, L):
    proj = proj.reshape(b, L, D_IN_PAD)
    k = proj[..., D_ATT:2 * D_ATT].reshape(b, L, H_ATT, HEAD_DIM)
    v = proj[..., 2 * D_ATT:3 * D_ATT].reshape(b, L, H_ATT, HEAD_DIM)
    o = 3 * D_ATT + D_SSM
    return proj, k, v, proj[:, L - (CONV_W - 1):, o:o + CONV_DIM]


def kernel(x_prompt, x_sample, cache_k, cache_v, state_ssm, state_conv, c_prompt, c_sample,
           rpb_table, w_ada, b_ada, g_ffn1, w_gate1, w_up1, w_down1, g_mix, w_in, conv_w, conv_b,
           dt_bias, a_log, d_skip, g_ssm_out, g_attn_out, w_out, g_ffn2, w_gate2, w_up2, w_down2, g_final):
    bp, sp, _ = x_prompt.shape
    bs, ls, _ = x_sample.shape
    wbp = min(WINDOW_MAX, sp)
    layer = 0
    w_in_pad = jnp.pad(w_in[layer].astype(BF16), ((0, 0), (0, D_IN_PAD - D_IN)))
    p = {'g_ffn1': g_ffn1[layer], 'w_gate1': w_gate1[layer].astype(BF16), 'w_up1': w_up1[layer].astype(BF16),
         'w_down1': w_down1[layer].astype(BF16), 'g_mix': g_mix[layer], 'w_in_pad': w_in_pad,
         'conv_w': conv_w[layer], 'conv_b': conv_b[layer].reshape(1, CONV_DIM),
         'dt_bias': jnp.pad(dt_bias[layer], (0, 128 - H_SSM)).reshape(1, 128),
         'a_log': jnp.pad(a_log[layer], (0, 128 - H_SSM)).reshape(1, 128),
         'd_skip_x': jnp.repeat(d_skip[layer], SSM_HEAD_DIM).reshape(1, D_SSM),
         'g_ssm_out': g_ssm_out[layer].reshape(1, D_SSM), 'g_attn_out': g_attn_out[layer],
         'w_out_att': w_out[layer, :D_ATT].astype(BF16), 'w_out_ssm': w_out[layer, D_ATT:].astype(BF16),
         'g_ffn2': g_ffn2[layer], 'w_gate2': w_gate2[layer].astype(BF16), 'w_up2': w_up2[layer].astype(BF16),
         'w_down2': w_down2[layer].astype(BF16)}

    n_c = bp + bs
    c_rows = -(-n_c // 16) * 16
    c_all = jnp.pad(jnp.concatenate([c_prompt, c_sample], axis=0), ((0, c_rows - n_c), (0, 0)))
    mod = _mod_call(c_all, w_ada[layer], b_ada[layer])
    mods_p = [mod[:bp, i * D_MODEL:(i + 1) * D_MODEL].reshape(bp, 1, D_MODEL) for i in range(N_MOD)]
    mods_s = [jnp.repeat(mod[bp:n_c, i * D_MODEL:(i + 1) * D_MODEL], ls, axis=0) for i in range(N_MOD)]

    lb_c, lb_n = _sample_log_bias(rpb_table, ls, cache_k.shape[2])
    cache_kt = jnp.transpose(cache_k[layer], (0, 2, 3, 1))
    cache_vt = jnp.transpose(cache_v[layer], (0, 2, 3, 1))
    dils = tuple(d for _, d in DILATED_PATTERNS)
    sh1, sc1, g1, sh2, sc2, g2, sh3, sc3, g3 = mods_p
    th1, tc1, t1, th2, tc2, t2, th3, tc3, t3 = mods_s
    ffn1 = (p['g_ffn1'], p['w_gate1'], p['w_up1'], p['w_down1'])
    ffn2 = (p['g_ffn2'], p['w_gate2'], p['w_up2'], p['w_down2'])
    kw_p = dict(tm=1024, rows_per_mod=sp)
    kw_s = dict(tm=bs * ls, rows_per_mod=ls)

    xs = _ffn_call(x_sample.reshape(bs * ls, D_MODEL), ffn1[0], th1, tc1, t1, *ffn1[1:], g_final,
                   final_norm=False, **kw_s)
    (proj_s,) = _inproj_call(xs, p['g_mix'], th2, tc2, p['w_in_pad'], dilations=(), **kw_s)
    proj_s, kn, vn, cs = _split_proj(proj_s, bs, ls)
    rider = lambda first, n: (proj_s, cache_kt, cache_vt, lb_c, lb_n, first, n)

    xp, att_a = _ffn_call(x_prompt.reshape(bp * sp, D_MODEL), ffn1[0], sh1, sc1, g1, *ffn1[1:], g_final,
                          final_norm=False, rider=rider(0, bs // 2), **kw_p)
    proj_p, *qkv = _inproj_call(xp, p['g_mix'], sh2, sc2, p['w_in_pad'], dilations=dils,
                                tm=512, rows_per_mod=sp)
    proj_p, kp, vp, cp = _split_proj(proj_p, bp, sp)
    outs, lses = [], []
    for (window, dil), qkv_d in zip(DILATED_PATTERNS, qkv):
        o, lse = _attn_prompt_call(qkv_d, _prompt_bias(rpb_table, window, dil), dil, bp)
        outs.append(o)
        lses.append(lse)
    ssd_p, hp = _ssd_prompt_call(proj_p, p)
    xp = _outproj_call(xp, outs + lses, ssd_p, g2, p['w_out_att'], p['w_out_ssm'], p['g_attn_out'],
                       tm=256, rows_per_mod=sp, dilations=dils)
    yp, att_b = _ffn_call(xp, ffn2[0], sh3, sc3, g3, *ffn2[1:], g_final,
                          final_norm=True, rider=rider(bs // 2, bs - bs // 2), **kw_p)

    att_s = jnp.concatenate([att_a, att_b], axis=0).reshape(bs * ls, D_ATT)
    ssd_s, hs = _ssd_sample_call(proj_s, state_conv[layer], state_ssm[layer], p)
    xs = _outproj_call(xs, [att_s], ssd_s, t2, p['w_out_att'], p['w_out_ssm'], p['g_attn_out'],
                       tm=256, rows_per_mod=ls, dilations=())
    ys = _ffn_call(xs, ffn2[0], th3, tc3, t3, *ffn2[1:], g_final, final_norm=True, **kw_s)
    return (yp.reshape(bp, sp, D_MODEL), ys.reshape(bs, ls, D_MODEL), kp[None, :, -wbp:], vp[None, :, -wbp:],
            kn[None], vn[None], hp[None], hs[None], cp[None], cs[None])
```

```python
import functools
import math

import jax
import jax.numpy as jnp
from jax import lax
from jax.experimental import pallas as pl
from jax.experimental.pallas import tpu as pltpu

D_MODEL = 2048
HEAD_DIM = 64
D_ATT = 1024
D_SSM = 1024
H_ATT = 16
SSM_HEAD_DIM = 64
H_SSM = 16
SSM_GROUPS = 2
D_STATE = 128
CONV_W = 4
SSD_CHUNK = 128
CONV_DIM = D_SSM + 2 * SSM_GROUPS * D_STATE
D_FF = 5632
DILATED_PATTERNS = ((128, 1), (512, 4), (2048, 16))
WINDOW_MAX = 2048
ATT_BLOCK = 128
RPB_BUCKETS = 32
RPB_MAX_DIST = 2048
NORM_EPS = 1e-6
N_MOD = 9
PROJ_SIZES = (D_ATT, D_ATT, D_ATT, D_SSM, CONV_DIM, H_SSM)
D_IN = sum(PROJ_SIZES)
D_IN_PAD = 6144

VMEM_LIMIT_BYTES = 56 * 1024 * 1024

BF16 = jnp.bfloat16
F32 = jnp.float32


def _silu(v):
    return v * jax.nn.sigmoid(v)


def _norm_mod(x, g, shift, scale):
    y = x * lax.rsqrt(jnp.mean(x * x, axis=-1, keepdims=True) + NORM_EPS)
    return (y * g) * (1.0 + scale) + shift


def _mod_kernel(c_ref, w_ref, b_ref, o_ref, a_ref):
    @pl.when(pl.program_id(0) == 0)
    def _():
        a_ref[...] = _silu(c_ref[...]).astype(BF16)

    o_ref[...] = jnp.dot(a_ref[...], w_ref[...].astype(BF16),
                         preferred_element_type=F32) + b_ref[...]


def _mod_call(c_all, w_ada, b_ada):
    rows = c_all.shape[0]
    n = w_ada.shape[1]
    tn = 1024
    return pl.pallas_call(
        _mod_kernel,
        out_shape=jax.ShapeDtypeStruct((rows, n), F32),
        grid=(n // tn,),
        in_specs=[
            pl.BlockSpec((rows, D_MODEL), lambda j: (0, 0)),
            pl.BlockSpec((D_MODEL, tn), lambda j: (0, j)),
            pl.BlockSpec((1, tn), lambda j: (0, j)),
        ],
        out_specs=pl.BlockSpec((rows, tn), lambda j: (0, j)),
        scratch_shapes=[pltpu.VMEM((rows, D_MODEL), BF16)],
        compiler_params=pltpu.CompilerParams(
            dimension_semantics=("arbitrary",), vmem_limit_bytes=VMEM_LIMIT_BYTES),
        name="adaln_mod",
    )(c_all, w_ada, b_ada.reshape(1, n))


NT_DIMS = (((1,), (1,)), ((), ()))
RIDER_HEADS = 8


def _sample_attn_heads(q_ref, kn_ref, vn_ref, kt_ref, vt_ref, lbc_ref, lbn_ref, o_ref):
    n_q, width = q_ref.shape
    pad = jnp.zeros((8 - n_q, width), F32)
    q8 = jnp.concatenate([q_ref[...] * (HEAD_DIM ** -0.5), pad], axis=0).astype(BF16)
    kn8 = jnp.concatenate([kn_ref[...], pad], axis=0).astype(BF16)
    vn8 = jnp.concatenate([vn_ref[...], pad], axis=0).astype(BF16)
    heads = width // HEAD_DIM
    wb = kt_ref.shape[2]
    hs = lambda h: slice(h * HEAD_DIM, (h + 1) * HEAD_DIM)
    rows = lambda h: slice(h * 8, (h + 1) * 8)
    t_c = jnp.concatenate([jnp.dot(q8[:, hs(h)], kt_ref[h].astype(BF16), preferred_element_type=F32)
                           for h in range(heads)], axis=0) + lbc_ref[...].reshape(heads * 8, wb)
    t_n = jnp.concatenate([lax.dot_general(q8[:, hs(h)], kn8[:, hs(h)], NT_DIMS, preferred_element_type=F32)
                           for h in range(heads)], axis=0) + lbn_ref[...].reshape(heads * 8, 8)
    m = jnp.maximum(jnp.max(t_c, axis=-1, keepdims=True), jnp.max(t_n, axis=-1, keepdims=True))
    p_c = jnp.exp(t_c - m)
    p_n = jnp.exp(t_n - m)
    inv_l = 1.0 / (jnp.sum(p_c, axis=-1, keepdims=True) + jnp.sum(p_n, axis=-1, keepdims=True))
    for h in range(heads):
        acc = lax.dot_general(p_c[rows(h)].astype(BF16), vt_ref[h].astype(BF16), NT_DIMS,
                              preferred_element_type=F32)
        acc += jnp.dot(p_n[rows(h)].astype(BF16), vn8[:, hs(h)], preferred_element_type=F32)
        o_ref[:, hs(h)] = (acc * inv_l[rows(h)])[0:n_q, :]


def _ffn_kernel(*refs, final_norm, rider):
    x_ref, g_ref, sh_ref, sc_ref, gt_ref, wg_ref, wu_ref, wd_ref, gf_ref = refs[:9]
    if rider:
        rider_in, (o_ref, att_ref, h_ref) = refs[9:16], refs[16:]
    else:
        o_ref, h_ref = refs[9:]
    j = pl.program_id(1)

    @pl.when(j == 0)
    def _():
        h_ref[...] = _norm_mod(x_ref[...], g_ref[...], sh_ref[...], sc_ref[...]).astype(BF16)
        o_ref[...] = jnp.zeros_like(o_ref)

    h = h_ref[...]
    a = jnp.dot(h, wg_ref[...], preferred_element_type=F32)
    b = jnp.dot(h, wu_ref[...], preferred_element_type=F32)
    s = (_silu(a) * b).astype(BF16)
    o_ref[...] += jnp.dot(s, wd_ref[...], preferred_element_type=F32)
    if rider:
        @pl.when(pl.program_id(0) * pl.num_programs(1) + j < rider)
        def _():
            _sample_attn_heads(*rider_in, att_ref)

    @pl.when(j == pl.num_programs(1) - 1)
    def _():
        y = x_ref[...] + 0.5 * gt_ref[...] * o_ref[...]
        if final_norm:
            y = y * lax.rsqrt(jnp.mean(y * y, axis=-1, keepdims=True) + NORM_EPS) * gf_ref[...]
        o_ref[...] = y


def _mod_spec(mod, tm, rows_per_mod):
    if mod.ndim == 3:
        tiles_per_row = rows_per_mod // tm
        return pl.BlockSpec((None, 1, D_MODEL), lambda i, j: (i // tiles_per_row, 0, 0))
    return pl.BlockSpec((tm, D_MODEL), lambda i, j: (i, 0))


def _ffn_call(x, g, sh, sc, gt, wg, wu, wd, g_final, *, tm, rows_per_mod, final_norm, rider=None):
    m = x.shape[0]
    tf = 256 if rider else 512
    nj = D_FF // tf
    row = lambda i, j: (i, 0)
    const = lambda i, j: (0, 0)
    in_specs = [
        pl.BlockSpec((tm, D_MODEL), row, pipeline_mode=pl.Buffered(1)),
        pl.BlockSpec((1, D_MODEL), const),
        _mod_spec(sh, tm, rows_per_mod),
        _mod_spec(sc, tm, rows_per_mod),
        _mod_spec(gt, tm, rows_per_mod),
        pl.BlockSpec((D_MODEL, tf), lambda i, j: (0, j)),
        pl.BlockSpec((D_MODEL, tf), lambda i, j: (0, j)),
        pl.BlockSpec((tf, D_MODEL), lambda i, j: (j, 0)),
        pl.BlockSpec((1, D_MODEL), const),
    ]
    args = [x, g.reshape(1, D_MODEL), sh, sc, gt, wg, wu, wd, g_final.reshape(1, D_MODEL)]
    out_shape = jax.ShapeDtypeStruct((m, D_MODEL), F32)
    out_specs = pl.BlockSpec((tm, D_MODEL), row)
    if rider:
        proj_s, cache_kt, cache_vt, lb_c, lb_n, first_seq, n_seq = rider
        n_q = proj_s.shape[1]
        wb = cache_kt.shape[3]
        halves = H_ATT // RIDER_HEADS
        n_units = n_seq * halves
        assert n_units <= (m // tm) * nj
        rw = RIDER_HEADS * HEAD_DIM
        unit = lambda i, j: jnp.minimum(i * nj + j, n_units - 1)
        seq = lambda i, j: first_seq + unit(i, j) // halves
        half = lambda i, j: unit(i, j) % halves
        tiles = D_ATT // rw
        qspec = lambda t: pl.BlockSpec((None, n_q, rw), lambda i, j: (seq(i, j), 0, t * tiles + half(i, j)))
        cspec = pl.BlockSpec((None, RIDER_HEADS, HEAD_DIM, wb), lambda i, j: (seq(i, j), half(i, j), 0, 0))
        in_specs += [qspec(0), qspec(1), qspec(2), cspec, cspec,
                     pl.BlockSpec((RIDER_HEADS, 8, wb), lambda i, j: (half(i, j), 0, 0)),
                     pl.BlockSpec((RIDER_HEADS, 8, 8), lambda i, j: (half(i, j), 0, 0))]
        args += [proj_s, proj_s, proj_s, cache_kt, cache_vt, lb_c, lb_n]
        out_shape = (out_shape, jax.ShapeDtypeStruct((n_seq, n_q, D_ATT), F32))
        out_specs = (pl.BlockSpec((tm, D_MODEL), row, pipeline_mode=pl.Buffered(1)),
                     pl.BlockSpec((None, n_q, rw), lambda i, j: (seq(i, j) - first_seq, 0, half(i, j))))
    return pl.pallas_call(
        functools.partial(_ffn_kernel, final_norm=final_norm, rider=n_units if rider else 0),
        out_shape=out_shape,
        grid=(m // tm, nj),
        in_specs=in_specs,
        out_specs=out_specs,
        scratch_shapes=[pltpu.VMEM((tm, D_MODEL), BF16)],
        compiler_params=pltpu.CompilerParams(
            dimension_semantics=("arbitrary", "arbitrary") if rider else ("parallel", "arbitrary"),
            vmem_limit_bytes=VMEM_LIMIT_BYTES),
        name=("ffn_final" if final_norm else "ffn") + ("_rider" if rider else ""),
    )(*args)


QKV_TILES = 3


def _inproj_kernel(x_ref, g_ref, sh_ref, sc_ref, w_ref, o_ref, *rest, dilations):
    qkv_refs, h_ref = rest[:-1], rest[-1]
    j = pl.program_id(1)
    tm = o_ref.shape[0]

    @pl.when(j == 0)
    def _():
        h_ref[...] = _norm_mod(x_ref[...], g_ref[...], sh_ref[...], sc_ref[...]).astype(BF16)

    res = lax.dot_general(h_ref[...], w_ref[...], NT_DIMS, preferred_element_type=F32)
    o_ref[...] = res

    @pl.when(j < QKV_TILES)
    def _():
        res_b = res.astype(BF16)
        out_row = lax.broadcasted_iota(jnp.int32, (tm, tm), 0)
        in_row = lax.broadcasted_iota(jnp.int32, (tm, tm), 1)
        for d, ref in zip(dilations, qkv_refs):
            if d == 1:
                ref[...] = res_b
                continue
            n = tm // d
            perm = (in_row == (out_row % n) * d + out_row // n).astype(BF16)
            grouped = jnp.dot(perm, res_b, preferred_element_type=F32).astype(BF16)
            for r in range(d):
                ref[:, r * D_ATT:(r + 1) * D_ATT] = grouped[r * n:(r + 1) * n, :]


def _inproj_call(x, g, sh, sc, w_pad_t, *, tm, rows_per_mod, dilations):
    m = x.shape[0]
    tn = D_ATT
    qkv_j = lambda j: jnp.minimum(j, QKV_TILES - 1)
    return pl.pallas_call(
        functools.partial(_inproj_kernel, dilations=dilations),
        out_shape=(jax.ShapeDtypeStruct((m, D_IN_PAD), F32),)
        + tuple(jax.ShapeDtypeStruct((m // d, QKV_TILES * d * tn), BF16) for d in dilations),
        grid=(m // tm, D_IN_PAD // tn),
        in_specs=[
            pl.BlockSpec((tm, D_MODEL), lambda i, j: (i, 0), pipeline_mode=pl.Buffered(1)),
            pl.BlockSpec((1, D_MODEL), lambda i, j: (0, 0)),
            _mod_spec(sh, tm, rows_per_mod),
            _mod_spec(sc, tm, rows_per_mod),
            pl.BlockSpec((tn, D_MODEL), lambda i, j: (j, 0)),
        ],
        out_specs=(pl.BlockSpec((tm, tn), lambda i, j: (i, j)),)
        + tuple(pl.BlockSpec((tm // d, d * tn), lambda i, j: (i, qkv_j(j))) for d in dilations),
        scratch_shapes=[pltpu.VMEM((tm, D_MODEL), BF16)],
        compiler_params=pltpu.CompilerParams(
            dimension_semantics=("parallel", "arbitrary"), vmem_limit_bytes=VMEM_LIMIT_BYTES),
        name="in_proj",
    )(x, g.reshape(1, D_MODEL), sh, sc, w_pad_t)


def _outproj_kernel(*refs, dilations):
    x_ref = refs[0]
    n_pat = len(dilations)
    if n_pat:
        o_refs = refs[1:1 + n_pat]
        l_refs = refs[1 + n_pat:1 + 2 * n_pat]
        ga_ref, ssd_ref, gt_ref, wa_ref, ws_ref, out_ref, o_scr, l_scr = refs[1 + 2 * n_pat:]
        tm = x_ref.shape[0]
        n_chunks = D_ATT // 128
        outs, lses = [], []
        for p, (d, o_ref, l_ref) in enumerate(zip(dilations, o_refs, l_refs)):
            if d == 1:
                outs.append(o_ref[...].astype(F32))
                lses.append(l_ref[...])
                continue
            for r in range(d):
                l_scr[p, pl.ds(r, tm // d, stride=d), :] = l_ref[:, r * 128:(r + 1) * 128]
                for c in range(n_chunks):
                    lanes = slice(r * D_ATT + c * 128, r * D_ATT + (c + 1) * 128)
                    o_scr[p, c, pl.ds(r, tm // d, stride=d), :] = o_ref[:, lanes].astype(F32)
            outs.append(jnp.concatenate([o_scr[p, c] for c in range(n_chunks)], axis=1))
            lses.append(l_scr[p])
        mx = functools.reduce(jnp.maximum, lses)
        ws = [jnp.exp(v - mx) for v in lses]
        wsum = functools.reduce(lambda a, b: a + b, ws)
        expand = _head_lane_mask(128).astype(BF16)
        att = None
        for w, o in zip(ws, outs):
            wn = w / wsum
            w_hi = wn.astype(BF16)
            w_lo = (wn - w_hi.astype(F32)).astype(BF16)
            wx = (jnp.dot(w_hi, expand, preferred_element_type=F32)
                  + jnp.dot(w_lo, expand, preferred_element_type=F32))
            att = wx * o if att is None else att + wx * o
    else:
        att_ref, ga_ref, ssd_ref, gt_ref, wa_ref, ws_ref, out_ref = refs[1:]
        att = att_ref[...]
    att = att * lax.rsqrt(jnp.mean(att * att, axis=-1, keepdims=True) + NORM_EPS) * ga_ref[...]
    mix = jnp.dot(att.astype(BF16), wa_ref[...], preferred_element_type=F32)
    mix += jnp.dot(ssd_ref[...].astype(BF16), ws_ref[...], preferred_element_type=F32)
    out_ref[...] = x_ref[...] + gt_ref[...] * mix


def _outproj_call(x, att_parts, ssd, gt, w_att, w_ssd, g_attn, *, tm, rows_per_mod, dilations):
    m = x.shape[0]
    n_pat = len(dilations)
    if gt.ndim == 3:
        tiles_per_row = rows_per_mod // tm
        gt_spec = pl.BlockSpec((None, 1, D_MODEL), lambda i: (i // tiles_per_row, 0, 0))
    else:
        gt_spec = pl.BlockSpec((tm, D_MODEL), lambda i: (i, 0))
    row = lambda width, d=1: pl.BlockSpec((tm // d, d * width), lambda i: (i, 0))
    scratch = []
    if n_pat:
        att_specs = ([row(D_ATT, d) for d in dilations] + [row(128, d) for d in dilations]
                     + [pl.BlockSpec((1, D_ATT), lambda i: (0, 0))])
        att_args = list(att_parts) + [g_attn.reshape(1, D_ATT)]
        scratch = [pltpu.VMEM((n_pat, D_ATT // 128, tm, 128), F32), pltpu.VMEM((n_pat, tm, 128), F32)]
    else:
        att_specs = [row(D_ATT), pl.BlockSpec((1, D_ATT), lambda i: (0, 0))]
        att_args = list(att_parts) + [g_attn.reshape(1, D_ATT)]
    weight = lambda k: pl.BlockSpec((k, D_MODEL), lambda i: (0, 0), pipeline_mode=pl.Buffered(1))
    return pl.pallas_call(
        functools.partial(_outproj_kernel, dilations=dilations),
        out_shape=jax.ShapeDtypeStruct((m, D_MODEL), F32),
        grid=(m // tm,),
        in_specs=[row(D_MODEL)] + att_specs + [row(D_SSM), gt_spec, weight(D_ATT), weight(D_SSM)],
        out_specs=row(D_MODEL),
        scratch_shapes=scratch,
        compiler_params=pltpu.CompilerParams(
            dimension_semantics=("parallel",), vmem_limit_bytes=VMEM_LIMIT_BYTES),
        name="out_proj_merge" if n_pat else "out_proj",
    )(x, *att_args, ssd, gt, w_att, w_ssd)


def _rel_bucket(dist):
    max_exact = RPB_BUCKETS // 2
    df = jnp.maximum(dist, 1).astype(jnp.float32)
    large = max_exact + (jnp.log(df / max_exact) / math.log(RPB_MAX_DIST / max_exact)
                         * (RPB_BUCKETS - max_exact)).astype(jnp.int32)
    large = jnp.minimum(large, RPB_BUCKETS - 1)
    return jnp.where(dist < max_exact, dist, large)


def _head_lane_mask(rows=H_ATT):
    r = lax.broadcasted_iota(jnp.int32, (rows, D_ATT), 0)
    c = lax.broadcasted_iota(jnp.int32, (rows, D_ATT), 1)
    return (c // HEAD_DIM == r).astype(F32)


def _attn_prompt_kernel(q_ref, kp_ref, kc_ref, vp_ref, vc_ref, bias_ref, o_ref, lse_ref):
    nb = pl.program_id(2)
    blk = ATT_BLOCK
    col = lax.broadcasted_iota(jnp.int32, (blk, 2 * blk), 1)
    has_prev = (col >= blk) | (nb > 0)
    lane = lax.broadcasted_iota(jnp.int32, (blk, 2 * HEAD_DIM), 1)
    lo = lane < HEAD_DIM
    lse_lane = lax.broadcasted_iota(jnp.int32, (blk, 128), 1)
    lse_tile = jnp.zeros((blk, 128), F32)
    scale = HEAD_DIM ** -0.5
    for hp in range(H_ATT // 2):
        sl = slice(hp * 2 * HEAD_DIM, (hp + 1) * 2 * HEAD_DIM)
        q2 = q_ref[:, sl] * scale
        k2 = jnp.concatenate([kp_ref[:, sl], kc_ref[:, sl]], axis=0)
        v2 = jnp.concatenate([vp_ref[:, sl], vc_ref[:, sl]], axis=0)
        outs = []
        for half in range(2):
            h = 2 * hp + half
            qh = jnp.where(lo if half == 0 else ~lo, q2, jnp.zeros_like(q2))
            s = lax.dot_general(qh, k2, NT_DIMS, preferred_element_type=F32) + bias_ref[h]
            s = jnp.where(has_prev, s, -jnp.inf)
            m = jnp.max(s, axis=-1, keepdims=True)
            p = jnp.exp(s - m)
            l = jnp.sum(p, axis=-1, keepdims=True)
            pv = jnp.dot(p.astype(BF16), v2, preferred_element_type=F32)
            outs.append(pv / l)
            lse_tile = jnp.where(lse_lane == h, m + jnp.log(l), lse_tile)
        o_ref[:, sl] = jnp.where(lo, outs[0], outs[1]).astype(o_ref.dtype)
    lse_ref[...] = lse_tile


def _attn_prompt_call(qkv, bias, dil, b):
    blk = ATT_BLOCK
    n_sub = qkv.shape[0] // b
    s_len = n_sub * dil
    nbk = n_sub // blk
    pv = qkv.reshape(b, n_sub, qkv.shape[1])
    cur = lambda t: (lambda bi, r, nb: (bi, nb, t * dil + r))
    prev = lambda t: (lambda bi, r, nb: (bi, jnp.maximum(nb - 1, 0), t * dil + r))
    spec = lambda f: pl.BlockSpec((None, blk, D_ATT), f)
    o, lse = pl.pallas_call(
        _attn_prompt_kernel,
        out_shape=(jax.ShapeDtypeStruct((b, n_sub, dil * D_ATT), BF16),
                   jax.ShapeDtypeStruct((b, n_sub, dil * 128), F32)),
        grid=(b, dil, nbk),
        in_specs=[spec(cur(0)), spec(prev(1)), spec(cur(1)), spec(prev(2)), spec(cur(2)),
                  pl.BlockSpec((H_ATT, blk, 2 * blk), lambda bi, r, nb: (0, 0, 0))],
        out_specs=(pl.BlockSpec((None, blk, D_ATT), lambda bi, r, nb: (bi, nb, r)),
                   pl.BlockSpec((None, blk, 128), lambda bi, r, nb: (bi, nb, r))),
        compiler_params=pltpu.CompilerParams(
            dimension_semantics=("parallel", "parallel", "arbitrary"),
            vmem_limit_bytes=VMEM_LIMIT_BYTES),
        name=f"attn_prompt_d{dil}",
    )(pv, pv, pv, pv, pv, bias)
    return o.reshape(b * n_sub, dil * D_ATT), lse.reshape(b * n_sub, dil * 128)


def _prompt_bias(rpb, window, dil):
    span = window // dil
    blk = ATT_BLOCK
    assert span == blk
    table = rpb[_rel_bucket(jnp.arange(span + 1) * dil)].T.astype(F32)
    fill = jnp.full((H_ATT, blk), -jnp.inf, F32)
    row = jnp.concatenate([fill, table[:, ::-1], fill], axis=1)
    wrapped = jnp.tile(row, (1, blk))[:, :blk * 3 * blk].reshape(H_ATT, blk, 3 * blk)
    return wrapped[:, :, blk:]


def _sample_log_bias(rpb, n_q, wb):
    reach = max(w for w, _ in DILATED_PATTERNS) + 2 * 8
    assert wb + n_q <= reach
    terms = []
    for window, dil in DILATED_PATTERNS:
        span = window // dil
        table = rpb[_rel_bucket(jnp.arange(span + 1) * dil)].T.astype(F32)
        gaps = jnp.full((H_ATT, span + 1, dil - 1), -jnp.inf, F32)
        by_dist = jnp.concatenate([table[:, :, None], gaps], axis=2).reshape(H_ATT, (span + 1) * dil)
        by_dist = by_dist[:, :reach]
        terms.append(jnp.pad(by_dist, ((0, 0), (0, reach - by_dist.shape[1])), constant_values=-jnp.inf))
    lb = jax.nn.logsumexp(jnp.stack(terms), axis=0)
    neg = jnp.full((H_ATT,), -jnp.inf, F32)
    rows_c, rows_n = [], []
    for r in range(8):
        n = r % n_q
        rows_c.append(lb[:, n + 1:n + 1 + wb][:, ::-1])
        rows_n.append(jnp.stack([lb[:, n - m] if (m < n_q and m <= n) else neg for m in range(8)], axis=1))
    return jnp.stack(rows_c, axis=1), jnp.stack(rows_n, axis=1)


TN_DIMS = (((0,), (0,)), ((), ()))
GROUP_W = D_SSM // SSM_GROUPS


def _split_bf16(v, pieces):
    out = []
    for _ in range(pieces):
        hi = v.astype(BF16)
        out.append(hi)
        v = v - hi.astype(F32)
    return out


def _dot_split(v, rhs_bf16, pieces, dims=None):
    acc = None
    for piece in _split_bf16(v, pieces):
        if dims is None:
            t = jnp.dot(piece, rhs_bf16, preferred_element_type=F32)
        else:
            t = lax.dot_general(piece, rhs_bf16, dims, preferred_element_type=F32)
        acc = t if acc is None else acc + t
    return acc


def _softplus(v):
    return jnp.maximum(v, 0.0) + jnp.log1p(jnp.exp(-jnp.abs(v)))


def _causal_conv(ext_ref, rows, cw_ref, cb_ref):
    conv = cb_ref[...]
    for k in range(CONV_W):
        conv = conv + cw_ref[CONV_W - 1 - k:CONV_W - k, :] * ext_ref[8 - k:8 - k + rows, :]
    return _silu(conv)


def _gated_group_norm(y, z, g):
    gated = y * _silu(z)
    parts = []
    for grp in range(SSM_GROUPS):
        seg = gated[:, grp * GROUP_W:(grp + 1) * GROUP_W]
        parts.append(seg * lax.rsqrt(jnp.mean(seg * seg, axis=-1, keepdims=True) + NORM_EPS))
    return jnp.concatenate(parts, axis=1) * g


def _ssd_prompt_kernel(z_ref, xd_ref, cw_ref, cb_ref, dtb_ref, alog_ref, dsk_ref, g_ref,
                       y_ref, hout_ref, ht_ref, ext_ref):
    c = pl.program_id(1)
    q = SSD_CHUNK

    @pl.when(c == 0)
    def _():
        ht_ref[...] = jnp.zeros_like(ht_ref)
        ext_ref[0:8, :] = jnp.zeros((8, CONV_DIM), F32)

    x_raw = xd_ref[:, 0:CONV_DIM]
    ext_ref[8:8 + q, :] = x_raw
    act = _causal_conv(ext_ref, q, cw_ref, cb_ref)
    ext_ref[0:8, :] = x_raw[q - 8:q, :]
    xs = act[:, 0:D_SSM]
    bm = act[:, D_SSM:D_SSM + SSM_GROUPS * D_STATE]
    cm = act[:, D_SSM + SSM_GROUPS * D_STATE:CONV_DIM]

    dt = _softplus(xd_ref[:, CONV_DIM:CONV_DIM + 128] + dtb_ref[...])
    da = dt * (-jnp.exp(alog_ref[...]))
    ti = lax.broadcasted_iota(jnp.int32, (q, q), 0)
    si = lax.broadcasted_iota(jnp.int32, (q, q), 1)
    causal = ti >= si
    a_cum = None
    for piece in _split_bf16(da, 3):
        t = jnp.dot(causal.astype(BF16), piece, preferred_element_type=F32)
        a_cum = t if a_cum is None else a_cum + t
    a_cum_t = a_cum.T
    expand = _head_lane_mask(128).astype(BF16)
    eac = jnp.exp(a_cum)
    dt_x = _dot_split(dt, expand, 2)
    eac_x = _dot_split(eac, expand, 2)
    dend_x = _dot_split(jnp.exp(a_cum[q - 1:q, :] - a_cum), expand, 2)
    xdt = xs * dt_x
    xdd = (xdt * dend_x).astype(BF16)
    xdt_b = xdt.astype(BF16)
    lane = lax.broadcasted_iota(jnp.int32, (q, 2 * SSM_HEAD_DIM), 1)
    lo = lane < SSM_HEAD_DIM
    heads_per_group = H_SSM // SSM_GROUPS
    for grp in range(SSM_GROUPS):
        gs = slice(grp * GROUP_W, (grp + 1) * GROUP_W)
        bm_g = bm[:, grp * D_STATE:(grp + 1) * D_STATE]
        cm_b = cm[:, grp * D_STATE:(grp + 1) * D_STATE].astype(BF16)
        cb_mat = lax.dot_general(cm_b, bm_g.astype(BF16), NT_DIMS, preferred_element_type=F32)
        h_in = ht_ref[:, gs]
        y_off = jnp.dot(cm_b, h_in.astype(BF16), preferred_element_type=F32) * eac_x[:, gs]
        states = jnp.dot(bm_g.T.astype(BF16), xdd[:, gs], preferred_element_type=F32)
        ht_ref[:, gs] = h_in * eac_x[q - 1:q, gs] + states
        for hp in range(heads_per_group // 2):
            h0 = grp * heads_per_group + 2 * hp
            ps = slice(h0 * SSM_HEAD_DIM, (h0 + 2) * SSM_HEAD_DIM)
            x_pair = xdt_b[:, ps]
            halves = []
            for half in range(2):
                h = h0 + half
                seg = a_cum[:, h:h + 1] - a_cum_t[h:h + 1, :]
                decay = jnp.exp(jnp.where(causal, seg, -jnp.inf))
                halves.append(jnp.dot((cb_mat * decay).astype(BF16), x_pair, preferred_element_type=F32))
            y_diag = jnp.where(lo, halves[0], halves[1])
            off = slice(2 * hp * SSM_HEAD_DIM, (2 * hp + 2) * SSM_HEAD_DIM)
            y_ref[:, ps] = y_diag + y_off[:, off] + dsk_ref[:, ps] * xs[:, ps]
    y_ref[...] = _gated_group_norm(y_ref[...], z_ref[...], g_ref[...])

    @pl.when(c == pl.num_programs(1) - 1)
    def _():
        hout_ref[...] = ht_ref[...].T


def _ssd_prompt_call(proj, sp):
    b, s_len, _ = proj.shape
    q = SSD_CHUNK
    const = lambda shape: pl.BlockSpec(shape, lambda bi, c: (0,) * len(shape))
    y, h_last = pl.pallas_call(
        _ssd_prompt_kernel,
        out_shape=(jax.ShapeDtypeStruct((b, s_len, D_SSM), F32),
                   jax.ShapeDtypeStruct((b, D_SSM, D_STATE), F32)),
        grid=(b, s_len // q),
        in_specs=[pl.BlockSpec((None, q, D_SSM), lambda bi, c: (bi, c, 3)),
                  pl.BlockSpec((None, q, 2048), lambda bi, c: (bi, c, 2)),
                  const((CONV_W, CONV_DIM)), const((1, CONV_DIM)), const((1, 128)), const((1, 128)),
                  const((1, D_SSM)), const((1, D_SSM))],
        out_specs=(pl.BlockSpec((None, q, D_SSM), lambda bi, c: (bi, c, 0)),
                   pl.BlockSpec((None, D_SSM, D_STATE), lambda bi, c: (bi, 0, 0))),
        scratch_shapes=[pltpu.VMEM((D_STATE, D_SSM), F32), pltpu.VMEM((8 + q, CONV_DIM), F32)],
        compiler_params=pltpu.CompilerParams(
            dimension_semantics=("parallel", "arbitrary"), vmem_limit_bytes=VMEM_LIMIT_BYTES),
        name="ssd_prompt",
    )(proj, proj, sp['conv_w'], sp['conv_b'], sp['dt_bias'], sp['a_log'], sp['d_skip_x'], sp['g_ssm_out'])
    return y.reshape(b * s_len, D_SSM), h_last.reshape(b, H_SSM, SSM_HEAD_DIM, D_STATE)


def _ssd_sample_kernel(z_ref, xd_ref, buf_ref, h0_ref, cw_ref, cb_ref, dtb_ref, alog_ref, dsk_ref, g_ref,
                       y_ref, hout_ref, ext_ref):
    n_seq, n_tok, _ = z_ref.shape
    expand = _head_lane_mask(128).astype(BF16)
    ti = lax.broadcasted_iota(jnp.int32, (8, 8), 0)
    si = lax.broadcasted_iota(jnp.int32, (8, 8), 1)
    causal = (ti >= si) & (si < n_tok)
    real = lax.broadcasted_iota(jnp.int32, (8, 1), 0) < n_tok
    ones = jnp.ones((8, D_STATE), BF16)
    ext_ref[...] = jnp.zeros_like(ext_ref)

    def one_sequence(j, ext_ref):
        ext_ref[8 - (CONV_W - 1):8, :] = buf_ref[j]
        ext_ref[8:8 + n_tok, :] = xd_ref[j][:, 0:CONV_DIM]
        act = _causal_conv(ext_ref, 8, cw_ref, cb_ref)
        xs = act[:, 0:D_SSM]
        bm = act[:, D_SSM:D_SSM + SSM_GROUPS * D_STATE]
        cm = act[:, D_SSM + SSM_GROUPS * D_STATE:CONV_DIM]
        pad = jnp.zeros((8 - n_tok, 128), F32)
        dt = _softplus(jnp.concatenate([xd_ref[j][:, CONV_DIM:CONV_DIM + 128], pad], axis=0) + dtb_ref[...])
        da = jnp.where(real, dt * (-jnp.exp(alog_ref[...])), 0.0)
        a_cum = None
        for piece in _split_bf16(da, 3):
            t = jnp.dot((ti >= si).astype(BF16), piece, preferred_element_type=F32)
            a_cum = t if a_cum is None else a_cum + t
        a_last = a_cum[n_tok - 1:n_tok, :]
        acum_x = _dot_split(a_cum, expand, 3)
        eac_x = jnp.exp(acum_x)
        dt_x = _dot_split(dt, expand, 2)
        xdt = jnp.where(real, xs * dt_x, 0.0)
        xdd = (xdt * jnp.exp(acum_x[n_tok - 1:n_tok, :] - acum_x)).astype(BF16)
        last_x = jnp.where(lax.broadcasted_iota(jnp.int32, (8, 1), 0) == n_tok - 1, eac_x, 0.0)
        keep = _dot_split(last_x, ones, 2, dims=TN_DIMS)
        y_parts, new_state = [], []
        for grp in range(SSM_GROUPS):
            gs = slice(grp * GROUP_W, (grp + 1) * GROUP_W)
            bm_b = bm[:, grp * D_STATE:(grp + 1) * D_STATE].astype(BF16)
            cm_b = cm[:, grp * D_STATE:(grp + 1) * D_STATE].astype(BF16)
            cb_mat = jnp.where(causal, lax.dot_general(cm_b, bm_b, NT_DIMS, preferred_element_type=F32), 0.0)
            h0_g = h0_ref[j, gs, :]
            y_g = lax.dot_general(cm_b, h0_g.astype(BF16), NT_DIMS, preferred_element_type=F32) * eac_x[:, gs]
            for s in range(n_tok):
                coef = cb_mat[:, s:s + 1] * jnp.exp(jnp.where(ti[:, 0:1] >= s, acum_x[:, gs] - acum_x[s:s + 1, gs], -jnp.inf))
                y_g = y_g + coef * xdt[s:s + 1, gs]
            y_parts.append(y_g)
            states = lax.dot_general(xdd[:, gs], bm_b, TN_DIMS, preferred_element_type=F32)
            hout_ref[j, gs, :] = h0_g * keep[gs, :] + states
        y = jnp.concatenate(y_parts, axis=1) + dsk_ref[...] * xs
        zj = jnp.concatenate([z_ref[j], jnp.zeros((8 - n_tok, D_SSM), F32)], axis=0)
        y_ref[j] = _gated_group_norm(y, zj, g_ref[...])[0:n_tok, :]

    def body(jj, carry):
        for slot in range(2):
            one_sequence(2 * jj + slot, ext_ref.at[slot])
        return carry

    lax.fori_loop(0, n_seq // 2, body, 0)


def _ssd_sample_call(proj, conv_buf, h0, sp):
    b, n_tok, _ = proj.shape
    nseq = 8
    const = lambda shape: pl.BlockSpec(shape, lambda i: (0,) * len(shape))
    y, h_last = pl.pallas_call(
        _ssd_sample_kernel,
        out_shape=(jax.ShapeDtypeStruct((b, n_tok, D_SSM), F32),
                   jax.ShapeDtypeStruct((b, D_SSM, D_STATE), F32)),
        grid=(b // nseq,),
        in_specs=[pl.BlockSpec((nseq, n_tok, D_SSM), lambda i: (i, 0, 3)),
                  pl.BlockSpec((nseq, n_tok, 2048), lambda i: (i, 0, 2)),
                  pl.BlockSpec((nseq, CONV_W - 1, CONV_DIM), lambda i: (i, 0, 0)),
                  pl.BlockSpec((nseq, D_SSM, D_STATE), lambda i: (i, 0, 0)),
                  const((CONV_W, CONV_DIM)), const((1, CONV_DIM)), const((1, 128)), const((1, 128)),
                  const((1, D_SSM)), const((1, D_SSM))],
        out_specs=(pl.BlockSpec((nseq, n_tok, D_SSM), lambda i: (i, 0, 0)),
                   pl.BlockSpec((nseq, D_SSM, D_STATE), lambda i: (i, 0, 0))),
        scratch_shapes=[pltpu.VMEM((2, 16, CONV_DIM), F32)],
        compiler_params=pltpu.CompilerParams(
            dimension_semantics=("parallel",), vmem_limit_bytes=VMEM_LIMIT_BYTES),
        name="ssd_sample",
    )(proj, proj, conv_buf, h0.reshape(b, D_SSM, D_STATE), sp['conv_w'], sp['conv_b'], sp['dt_bias'],
      sp['a_log'], sp['d_skip_x'], sp['g_ssm_out'])
    return y.reshape(b * n_tok, D_SSM), h_last.reshape(b, H_SSM, SSM_HEAD_DIM, D_STATE)


def _split_proj(proj, b, L, keep):
    proj = proj.reshape(b, L, D_IN_PAD)
    k = proj[:, L - keep:, D_ATT:2 * D_ATT].reshape(b, keep, H_ATT, HEAD_DIM)
    v = proj[:, L - keep:, 2 * D_ATT:3 * D_ATT].reshape(b, keep, H_ATT, HEAD_DIM)
    o = 3 * D_ATT + D_SSM
    return proj, k, v, proj[:, L - (CONV_W - 1):, o:o + CONV_DIM]


def kernel(x_prompt, x_sample, cache_k, cache_v, state_ssm, state_conv, c_prompt, c_sample,
           rpb_table, w_ada, b_ada, g_ffn1, w_gate1, w_up1, w_down1, g_mix, w_in, conv_w, conv_b,
           dt_bias, a_log, d_skip, g_ssm_out, g_attn_out, w_out, g_ffn2, w_gate2, w_up2, w_down2, g_final):
    bp, sp, _ = x_prompt.shape
    bs, ls, _ = x_sample.shape
    wbp = min(WINDOW_MAX, sp)
    layer = 0
    w_in_pad = jnp.pad(jnp.transpose(w_in[layer]).astype(BF16), ((0, D_IN_PAD - D_IN), (0, 0)))
    p = {'g_ffn1': g_ffn1[layer], 'w_gate1': w_gate1[layer].astype(BF16), 'w_up1': w_up1[layer].astype(BF16),
         'w_down1': w_down1[layer].astype(BF16), 'g_mix': g_mix[layer], 'w_in_pad': w_in_pad,
         'conv_w': conv_w[layer], 'conv_b': conv_b[layer].reshape(1, CONV_DIM),
         'dt_bias': jnp.pad(dt_bias[layer], (0, 128 - H_SSM)).reshape(1, 128),
         'a_log': jnp.pad(a_log[layer], (0, 128 - H_SSM)).reshape(1, 128),
         'd_skip_x': jnp.repeat(d_skip[layer], SSM_HEAD_DIM).reshape(1, D_SSM),
         'g_ssm_out': g_ssm_out[layer].reshape(1, D_SSM), 'g_attn_out': g_attn_out[layer],
         'w_out_att': w_out[layer, :D_ATT].astype(BF16), 'w_out_ssm': w_out[layer, D_ATT:].astype(BF16),
         'g_ffn2': g_ffn2[layer], 'w_gate2': w_gate2[layer].astype(BF16), 'w_up2': w_up2[layer].astype(BF16),
         'w_down2': w_down2[layer].astype(BF16)}

    n_c = bp + bs
    c_rows = -(-n_c // 16) * 16
    c_all = jnp.pad(jnp.concatenate([c_prompt, c_sample], axis=0), ((0, c_rows - n_c), (0, 0)))
    mod = _mod_call(c_all, w_ada[layer], b_ada[layer])
    mods_p = [mod[:bp, i * D_MODEL:(i + 1) * D_MODEL].reshape(bp, 1, D_MODEL) for i in range(N_MOD)]
    mods_s = [jnp.repeat(mod[bp:n_c, i * D_MODEL:(i + 1) * D_MODEL], ls, axis=0) for i in range(N_MOD)]

    lb_c, lb_n = _sample_log_bias(rpb_table, ls, cache_k.shape[2])
    cache_kt = jnp.transpose(cache_k[layer], (0, 2, 3, 1))
    cache_vt = jnp.transpose(cache_v[layer], (0, 2, 3, 1))
    dils = tuple(d for _, d in DILATED_PATTERNS)
    sh1, sc1, g1, sh2, sc2, g2, sh3, sc3, g3 = mods_p
    th1, tc1, t1, th2, tc2, t2, th3, tc3, t3 = mods_s
    ffn1 = (p['g_ffn1'], p['w_gate1'], p['w_up1'], p['w_down1'])
    ffn2 = (p['g_ffn2'], p['w_gate2'], p['w_up2'], p['w_down2'])
    kw_p = dict(tm=1024, rows_per_mod=sp)
    kw_s = dict(tm=bs * ls, rows_per_mod=ls)

    xs = _ffn_call(x_sample.reshape(bs * ls, D_MODEL), ffn1[0], th1, tc1, t1, *ffn1[1:], g_final,
                   final_norm=False, **kw_s)
    (proj_s,) = _inproj_call(xs, p['g_mix'], th2, tc2, p['w_in_pad'], dilations=(), **kw_s)
    proj_s, kn, vn, cs = _split_proj(proj_s, bs, ls, ls)
    rider = lambda first, n: (proj_s, cache_kt, cache_vt, lb_c, lb_n, first, n)

    xp, att_a = _ffn_call(x_prompt.reshape(bp * sp, D_MODEL), ffn1[0], sh1, sc1, g1, *ffn1[1:], g_final,
                          final_norm=False, rider=rider(0, bs // 2), **kw_p)
    proj_p, *qkv = _inproj_call(xp, p['g_mix'], sh2, sc2, p['w_in_pad'], dilations=dils,
                                tm=512, rows_per_mod=sp)
    proj_p, kp, vp, cp = _split_proj(proj_p, bp, sp, wbp)
    outs, lses = [], []
    for (window, dil), qkv_d in zip(DILATED_PATTERNS, qkv):
        o, lse = _attn_prompt_call(qkv_d, _prompt_bias(rpb_table, window, dil), dil, bp)
        outs.append(o)
        lses.append(lse)
    ssd_p, hp = _ssd_prompt_call(proj_p, p)
    xp = _outproj_call(xp, outs + lses, ssd_p, g2, p['w_out_att'], p['w_out_ssm'], p['g_attn_out'],
                       tm=256, rows_per_mod=sp, dilations=dils)
    yp, att_b = _ffn_call(xp, ffn2[0], sh3, sc3, g3, *ffn2[1:], g_final,
                          final_norm=True, rider=rider(bs // 2, bs - bs // 2), **kw_p)

    att_s = jnp.concatenate([att_a, att_b], axis=0).reshape(bs * ls, D_ATT)
    ssd_s, hs = _ssd_sample_call(proj_s, state_conv[layer], state_ssm[layer], p)
    xs = _outproj_call(xs, [att_s], ssd_s, t2, p['w_out_att'], p['w_out_ssm'], p['g_attn_out'],
                       tm=256, rows_per_mod=ls, dilations=())
    ys = _ffn_call(xs, ffn2[0], th3, tc3, t3, *ffn2[1:], g_final, final_norm=True, **kw_s)
    return (yp.reshape(bp, sp, D_MODEL), ys.reshape(bs, ls, D_MODEL), kp[None], vp[None],
            kn[None], vn[None], hp[None], hs[None], cp[None], cs[None])
```

```python
import functools
import math

import jax
import jax.numpy as jnp
from jax import lax
from jax.experimental import pallas as pl
from jax.experimental.pallas import tpu as pltpu

D_MODEL = 2048
HEAD_DIM = 64
D_ATT = 1024
D_SSM = 1024
H_ATT = 16
SSM_HEAD_DIM = 64
H_SSM = 16
SSM_GROUPS = 2
D_STATE = 128
CONV_W = 4
SSD_CHUNK = 128
CONV_DIM = D_SSM + 2 * SSM_GROUPS * D_STATE
D_FF = 5632
DILATED_PATTERNS = ((128, 1), (512, 4), (2048, 16))
WINDOW_MAX = 2048
ATT_BLOCK = 128
RPB_BUCKETS = 32
RPB_MAX_DIST = 2048
NORM_EPS = 1e-6
N_MOD = 9
PROJ_SIZES = (D_ATT, D_ATT, D_ATT, D_SSM, CONV_DIM, H_SSM)
D_IN = sum(PROJ_SIZES)
D_IN_PAD = 6144

VMEM_LIMIT_BYTES = 56 * 1024 * 1024

BF16 = jnp.bfloat16
F32 = jnp.float32


def _silu(v):
    return v * jax.nn.sigmoid(v)


def _norm_mod(x, g, shift, scale):
    y = x * lax.rsqrt(jnp.mean(x * x, axis=-1, keepdims=True) + NORM_EPS)
    return (y * g) * (1.0 + scale) + shift


def _mod_kernel(c_ref, w_ref, b_ref, o_ref, a_ref):
    @pl.when(pl.program_id(0) == 0)
    def _():
        a_ref[...] = _silu(c_ref[...]).astype(BF16)

    o_ref[...] = jnp.dot(a_ref[...], w_ref[...].astype(BF16),
                         preferred_element_type=F32) + b_ref[...]


def _mod_call(c_all, w_ada, b_ada):
    rows = c_all.shape[0]
    n = w_ada.shape[1]
    tn = 1024
    return pl.pallas_call(
        _mod_kernel,
        out_shape=jax.ShapeDtypeStruct((rows, n), F32),
        grid=(n // tn,),
        in_specs=[
            pl.BlockSpec((rows, D_MODEL), lambda j: (0, 0)),
            pl.BlockSpec((D_MODEL, tn), lambda j: (0, j)),
            pl.BlockSpec((1, tn), lambda j: (0, j)),
        ],
        out_specs=pl.BlockSpec((rows, tn), lambda j: (0, j)),
        scratch_shapes=[pltpu.VMEM((rows, D_MODEL), BF16)],
        compiler_params=pltpu.CompilerParams(
            dimension_semantics=("arbitrary",), vmem_limit_bytes=VMEM_LIMIT_BYTES),
        name="adaln_mod",
    )(c_all, w_ada, b_ada.reshape(1, n))


NT_DIMS = (((1,), (1,)), ((), ()))
RIDER_HEADS = 8
RIDER_GROUP = 8


def _sample_attn_heads(q_ref, kn_ref, vn_ref, kt_ref, vt_ref, lbc_ref, lbn_ref, o_ref):
    n_q, width = q_ref.shape
    pad = jnp.zeros((8 - n_q, width), F32)
    q8 = jnp.concatenate([q_ref[...] * (HEAD_DIM ** -0.5), pad], axis=0).astype(BF16)
    kn8 = jnp.concatenate([kn_ref[...], pad], axis=0).astype(BF16)
    vn8 = jnp.concatenate([vn_ref[...], pad], axis=0).astype(BF16)
    heads = width // HEAD_DIM
    wb = kt_ref.shape[2]
    hs = lambda h: slice(h * HEAD_DIM, (h + 1) * HEAD_DIM)
    group = RIDER_GROUP
    for g0 in range(0, heads, group):
        hh = range(g0, g0 + group)
        t_c = jnp.concatenate([jnp.dot(q8[:, hs(h)], kt_ref[h].astype(BF16), preferred_element_type=F32)
                               for h in hh], axis=0) + lbc_ref[g0:g0 + group].reshape(group * 8, wb)
        t_n = jnp.concatenate([lax.dot_general(q8[:, hs(h)], kn8[:, hs(h)], NT_DIMS,
                                               preferred_element_type=F32)
                               for h in hh], axis=0) + lbn_ref[g0:g0 + group].reshape(group * 8, 8)
        m = jnp.maximum(jnp.max(t_c, axis=-1, keepdims=True), jnp.max(t_n, axis=-1, keepdims=True))
        p_c = jnp.exp(t_c - m)
        p_n = jnp.exp(t_n - m)
        inv_l = 1.0 / (jnp.sum(p_c, axis=-1, keepdims=True) + jnp.sum(p_n, axis=-1, keepdims=True))
        for i, h in enumerate(hh):
            rows = slice(i * 8, (i + 1) * 8)
            acc = lax.dot_general(p_c[rows].astype(BF16), vt_ref[h].astype(BF16), NT_DIMS,
                                  preferred_element_type=F32)
            acc += jnp.dot(p_n[rows].astype(BF16), vn8[:, hs(h)], preferred_element_type=F32)
            o_ref[:, hs(h)] = (acc * inv_l[rows])[0:n_q, :]


def _ffn_kernel(*refs, final_norm, rider):
    x_ref, g_ref, sh_ref, sc_ref, gt_ref, wg_ref, wu_ref, wd_ref, gf_ref = refs[:9]
    if rider:
        rider_in, (o_ref, att_ref, h_ref) = refs[9:16], refs[16:]
    else:
        o_ref, h_ref = refs[9:]
    j = pl.program_id(1)

    @pl.when(j == 0)
    def _():
        h_ref[...] = _norm_mod(x_ref[...], g_ref[...], sh_ref[...], sc_ref[...]).astype(BF16)
        o_ref[...] = jnp.zeros_like(o_ref)

    h = h_ref[...]
    a = jnp.dot(h, wg_ref[...], preferred_element_type=F32)
    b = jnp.dot(h, wu_ref[...], preferred_element_type=F32)
    s = (_silu(a) * b).astype(BF16)
    o_ref[...] += jnp.dot(s, wd_ref[...], preferred_element_type=F32)
    if rider:
        @pl.when(pl.program_id(0) * pl.num_programs(1) + j < rider)
        def _():
            _sample_attn_heads(*rider_in, att_ref)

    @pl.when(j == pl.num_programs(1) - 1)
    def _():
        y = x_ref[...] + 0.5 * gt_ref[...] * o_ref[...]
        if final_norm:
            y = y * lax.rsqrt(jnp.mean(y * y, axis=-1, keepdims=True) + NORM_EPS) * gf_ref[...]
        o_ref[...] = y


def _mod_spec(mod, tm, rows_per_mod):
    if mod.ndim == 3:
        tiles_per_row = rows_per_mod // tm
        return pl.BlockSpec((None, 1, D_MODEL), lambda i, j: (i // tiles_per_row, 0, 0))
    return pl.BlockSpec((tm, D_MODEL), lambda i, j: (i, 0))


def _ffn_call(x, g, sh, sc, gt, wg, wu, wd, g_final, *, tm, rows_per_mod, final_norm, rider=None):
    m = x.shape[0]
    tf = 256 if rider else 512
    nj = D_FF // tf
    row = lambda i, j: (i, 0)
    const = lambda i, j: (0, 0)
    in_specs = [
        pl.BlockSpec((tm, D_MODEL), row, pipeline_mode=pl.Buffered(1)),
        pl.BlockSpec((1, D_MODEL), const),
        _mod_spec(sh, tm, rows_per_mod),
        _mod_spec(sc, tm, rows_per_mod),
        _mod_spec(gt, tm, rows_per_mod),
        pl.BlockSpec((D_MODEL, tf), lambda i, j: (0, j)),
        pl.BlockSpec((D_MODEL, tf), lambda i, j: (0, j)),
        pl.BlockSpec((tf, D_MODEL), lambda i, j: (j, 0)),
        pl.BlockSpec((1, D_MODEL), const),
    ]
    args = [x, g.reshape(1, D_MODEL), sh, sc, gt, wg, wu, wd, g_final.reshape(1, D_MODEL)]
    out_shape = jax.ShapeDtypeStruct((m, D_MODEL), F32)
    out_specs = pl.BlockSpec((tm, D_MODEL), row)
    if rider:
        proj_s, cache_kt, cache_vt, lb_c, lb_n, first_seq, n_seq = rider
        n_q = proj_s.shape[1]
        wb = cache_kt.shape[3]
        halves = H_ATT // RIDER_HEADS
        n_units = n_seq * halves
        assert n_units <= (m // tm) * nj
        rw = RIDER_HEADS * HEAD_DIM
        unit = lambda i, j: jnp.minimum(i * nj + j, n_units - 1)
        seq = lambda i, j: first_seq + unit(i, j) // halves
        half = lambda i, j: unit(i, j) % halves
        tiles = D_ATT // rw
        qspec = lambda t: pl.BlockSpec((None, n_q, rw), lambda i, j: (seq(i, j), 0, t * tiles + half(i, j)))
        cspec = pl.BlockSpec((None, RIDER_HEADS, HEAD_DIM, wb), lambda i, j: (seq(i, j), half(i, j), 0, 0))
        in_specs += [qspec(0), qspec(1), qspec(2), cspec, cspec,
                     pl.BlockSpec((RIDER_HEADS, 8, wb), lambda i, j: (half(i, j), 0, 0)),
                     pl.BlockSpec((RIDER_HEADS, 8, 8), lambda i, j: (half(i, j), 0, 0))]
        args += [proj_s, proj_s, proj_s, cache_kt, cache_vt, lb_c, lb_n]
        out_shape = (out_shape, jax.ShapeDtypeStruct((n_seq, n_q, D_ATT), F32))
        out_specs = (pl.BlockSpec((tm, D_MODEL), row, pipeline_mode=pl.Buffered(1)),
                     pl.BlockSpec((None, n_q, rw), lambda i, j: (seq(i, j) - first_seq, 0, half(i, j))))
    return pl.pallas_call(
        functools.partial(_ffn_kernel, final_norm=final_norm, rider=n_units if rider else 0),
        out_shape=out_shape,
        grid=(m // tm, nj),
        in_specs=in_specs,
        out_specs=out_specs,
        scratch_shapes=[pltpu.VMEM((tm, D_MODEL), BF16)],
        compiler_params=pltpu.CompilerParams(
            dimension_semantics=("arbitrary", "arbitrary") if rider else ("parallel", "arbitrary"),
            vmem_limit_bytes=VMEM_LIMIT_BYTES),
        name=("ffn_final" if final_norm else "ffn") + ("_rider" if rider else ""),
    )(*args)


QKV_TILES = 3


def _inproj_kernel(x_ref, g_ref, sh_ref, sc_ref, w_ref, o_ref, *rest, dilations):
    qkv_refs, h_ref = rest[:-1], rest[-1]
    j = pl.program_id(1)
    tm = o_ref.shape[0]

    @pl.when(j == 0)
    def _():
        h_ref[...] = _norm_mod(x_ref[...], g_ref[...], sh_ref[...], sc_ref[...]).astype(BF16)

    res = lax.dot_general(h_ref[...], w_ref[...], NT_DIMS, preferred_element_type=F32)
    o_ref[...] = res

    @pl.when(j < QKV_TILES)
    def _():
        res_b = res.astype(BF16)
        part = min(tm, 512)
        out_row = lax.broadcasted_iota(jnp.int32, (part, part), 0)
        in_row = lax.broadcasted_iota(jnp.int32, (part, part), 1)
        for d, ref in zip(dilations, qkv_refs):
            if d == 1:
                ref[...] = res_b
                continue
            n = part // d
            perm = (in_row == (out_row % n) * d + out_row // n).astype(BF16)
            for s in range(tm // part):
                grouped = jnp.dot(perm, res_b[s * part:(s + 1) * part, :], preferred_element_type=F32)
                grouped = grouped.astype(BF16)
                for r in range(d):
                    ref[s * n:(s + 1) * n, r * D_ATT:(r + 1) * D_ATT] = grouped[r * n:(r + 1) * n, :]


def _inproj_call(x, g, sh, sc, w_pad_t, *, tm, rows_per_mod, dilations):
    m = x.shape[0]
    tn = D_ATT
    qkv_j = lambda j: jnp.minimum(j, QKV_TILES - 1)
    return pl.pallas_call(
        functools.partial(_inproj_kernel, dilations=dilations),
        out_shape=(jax.ShapeDtypeStruct((m, D_IN_PAD), F32),)
        + tuple(jax.ShapeDtypeStruct((m // d, QKV_TILES * d * tn), BF16) for d in dilations),
        grid=(m // tm, D_IN_PAD // tn),
        in_specs=[
            pl.BlockSpec((tm, D_MODEL), lambda i, j: (i, 0), pipeline_mode=pl.Buffered(1)),
            pl.BlockSpec((1, D_MODEL), lambda i, j: (0, 0)),
            _mod_spec(sh, tm, rows_per_mod),
            _mod_spec(sc, tm, rows_per_mod),
            pl.BlockSpec((tn, D_MODEL), lambda i, j: (j, 0)),
        ],
        out_specs=(pl.BlockSpec((tm, tn), lambda i, j: (i, j)),)
        + tuple(pl.BlockSpec((tm // d, d * tn), lambda i, j: (i, qkv_j(j))) for d in dilations),
        scratch_shapes=[pltpu.VMEM((tm, D_MODEL), BF16)],
        compiler_params=pltpu.CompilerParams(
            dimension_semantics=("parallel", "arbitrary"), vmem_limit_bytes=VMEM_LIMIT_BYTES),
        name="in_proj",
    )(x, g.reshape(1, D_MODEL), sh, sc, w_pad_t)


def _outproj_kernel(*refs, dilations):
    x_ref = refs[0]
    n_pat = len(dilations)
    if n_pat:
        o_refs = refs[1:1 + n_pat]
        l_refs = refs[1 + n_pat:1 + 2 * n_pat]
        ga_ref, ssd_ref, gt_ref, wa_ref, ws_ref, out_ref, o_scr, l_scr = refs[1 + 2 * n_pat:]
        tm = x_ref.shape[0]
        n_chunks = D_ATT // 128
        outs, lses = [], []
        for p, (d, o_ref, l_ref) in enumerate(zip(dilations, o_refs, l_refs)):
            if d == 1:
                outs.append(o_ref[...].astype(F32))
                lses.append(l_ref[...])
                continue
            for r in range(d):
                l_scr[p, pl.ds(r, tm // d, stride=d), :] = l_ref[:, r * 128:(r + 1) * 128]
                for c in range(n_chunks):
                    lanes = slice(r * D_ATT + c * 128, r * D_ATT + (c + 1) * 128)
                    o_scr[p, c, pl.ds(r, tm // d, stride=d), :] = o_ref[:, lanes].astype(F32)
            outs.append(jnp.concatenate([o_scr[p, c] for c in range(n_chunks)], axis=1))
            lses.append(l_scr[p])
        mx = functools.reduce(jnp.maximum, lses)
        ws = [jnp.exp(v - mx) for v in lses]
        wsum = functools.reduce(lambda a, b: a + b, ws)
        expand = _head_lane_mask(128).astype(BF16)
        att = None
        for w, o in zip(ws, outs):
            wn = w / wsum
            w_hi = wn.astype(BF16)
            w_lo = (wn - w_hi.astype(F32)).astype(BF16)
            wx = (jnp.dot(w_hi, expand, preferred_element_type=F32)
                  + jnp.dot(w_lo, expand, preferred_element_type=F32))
            att = wx * o if att is None else att + wx * o
    else:
        att_ref, ga_ref, ssd_ref, gt_ref, wa_ref, ws_ref, out_ref = refs[1:]
        att = att_ref[...]
    att = att * lax.rsqrt(jnp.mean(att * att, axis=-1, keepdims=True) + NORM_EPS) * ga_ref[...]
    mix = jnp.dot(att.astype(BF16), wa_ref[...], preferred_element_type=F32)
    mix += jnp.dot(ssd_ref[...].astype(BF16), ws_ref[...], preferred_element_type=F32)
    out_ref[...] = x_ref[...] + gt_ref[...] * mix


def _outproj_call(x, att_parts, ssd, gt, w_att, w_ssd, g_attn, *, tm, rows_per_mod, dilations):
    m = x.shape[0]
    n_pat = len(dilations)
    if gt.ndim == 3:
        tiles_per_row = rows_per_mod // tm
        gt_spec = pl.BlockSpec((None, 1, D_MODEL), lambda i: (i // tiles_per_row, 0, 0))
    else:
        gt_spec = pl.BlockSpec((tm, D_MODEL), lambda i: (i, 0))
    row = lambda width, d=1: pl.BlockSpec((tm // d, d * width), lambda i: (i, 0))
    scratch = []
    if n_pat:
        att_specs = ([row(D_ATT, d) for d in dilations] + [row(128, d) for d in dilations]
                     + [pl.BlockSpec((1, D_ATT), lambda i: (0, 0))])
        att_args = list(att_parts) + [g_attn.reshape(1, D_ATT)]
        scratch = [pltpu.VMEM((n_pat, D_ATT // 128, tm, 128), F32), pltpu.VMEM((n_pat, tm, 128), F32)]
    else:
        att_specs = [row(D_ATT), pl.BlockSpec((1, D_ATT), lambda i: (0, 0))]
        att_args = list(att_parts) + [g_attn.reshape(1, D_ATT)]
    weight = lambda k: pl.BlockSpec((k, D_MODEL), lambda i: (0, 0), pipeline_mode=pl.Buffered(1))
    return pl.pallas_call(
        functools.partial(_outproj_kernel, dilations=dilations),
        out_shape=jax.ShapeDtypeStruct((m, D_MODEL), F32),
        grid=(m // tm,),
        in_specs=[row(D_MODEL)] + att_specs + [row(D_SSM), gt_spec, weight(D_ATT), weight(D_SSM)],
        out_specs=row(D_MODEL),
        scratch_shapes=scratch,
        compiler_params=pltpu.CompilerParams(
            dimension_semantics=("parallel",), vmem_limit_bytes=VMEM_LIMIT_BYTES),
        name="out_proj_merge" if n_pat else "out_proj",
    )(x, *att_args, ssd, gt, w_att, w_ssd)


def _rel_bucket(dist):
    max_exact = RPB_BUCKETS // 2
    df = jnp.maximum(dist, 1).astype(jnp.float32)
    large = max_exact + (jnp.log(df / max_exact) / math.log(RPB_MAX_DIST / max_exact)
                         * (RPB_BUCKETS - max_exact)).astype(jnp.int32)
    large = jnp.minimum(large, RPB_BUCKETS - 1)
    return jnp.where(dist < max_exact, dist, large)


def _head_lane_mask(rows=H_ATT):
    r = lax.broadcasted_iota(jnp.int32, (rows, D_ATT), 0)
    c = lax.broadcasted_iota(jnp.int32, (rows, D_ATT), 1)
    return (c // HEAD_DIM == r).astype(F32)


def _attn_prompt_kernel(q_ref, kp_ref, kc_ref, vp_ref, vc_ref, bias_ref, o_ref, lse_ref):
    nb = pl.program_id(2)
    blk = ATT_BLOCK
    col = lax.broadcasted_iota(jnp.int32, (blk, 2 * blk), 1)
    has_prev = (col >= blk) | (nb > 0)
    lane = lax.broadcasted_iota(jnp.int32, (blk, 2 * HEAD_DIM), 1)
    lo = lane < HEAD_DIM
    lse_lane = lax.broadcasted_iota(jnp.int32, (blk, 128), 1)
    lse_tile = jnp.zeros((blk, 128), F32)
    scale = HEAD_DIM ** -0.5
    for hp in range(H_ATT // 2):
        sl = slice(hp * 2 * HEAD_DIM, (hp + 1) * 2 * HEAD_DIM)
        q2 = q_ref[:, sl] * scale
        k2 = jnp.concatenate([kp_ref[:, sl], kc_ref[:, sl]], axis=0)
        v2 = jnp.concatenate([vp_ref[:, sl], vc_ref[:, sl]], axis=0)
        outs = []
        for half in range(2):
            h = 2 * hp + half
            qh = jnp.where(lo if half == 0 else ~lo, q2, jnp.zeros_like(q2))
            s = lax.dot_general(qh, k2, NT_DIMS, preferred_element_type=F32) + bias_ref[h]
            s = jnp.where(has_prev, s, -jnp.inf)
            m = jnp.max(s, axis=-1, keepdims=True)
            p = jnp.exp(s - m)
            l = jnp.sum(p, axis=-1, keepdims=True)
            pv = jnp.dot(p.astype(BF16), v2, preferred_element_type=F32)
            outs.append(pv / l)
            lse_tile = jnp.where(lse_lane == h, m + jnp.log(l), lse_tile)
        o_ref[:, sl] = jnp.where(lo, outs[0], outs[1]).astype(o_ref.dtype)
    lse_ref[...] = lse_tile


def _attn_prompt_call(qkv, bias, dil, b):
    blk = ATT_BLOCK
    n_sub = qkv.shape[0] // b
    s_len = n_sub * dil
    nbk = n_sub // blk
    pv = qkv.reshape(b, n_sub, qkv.shape[1])
    cur = lambda t: (lambda bi, r, nb: (bi, nb, t * dil + r))
    prev = lambda t: (lambda bi, r, nb: (bi, jnp.maximum(nb - 1, 0), t * dil + r))
    spec = lambda f: pl.BlockSpec((None, blk, D_ATT), f)
    o, lse = pl.pallas_call(
        _attn_prompt_kernel,
        out_shape=(jax.ShapeDtypeStruct((b, n_sub, dil * D_ATT), BF16),
                   jax.ShapeDtypeStruct((b, n_sub, dil * 128), F32)),
        grid=(b, dil, nbk),
        in_specs=[spec(cur(0)), spec(prev(1)), spec(cur(1)), spec(prev(2)), spec(cur(2)),
                  pl.BlockSpec((H_ATT, blk, 2 * blk), lambda bi, r, nb: (0, 0, 0))],
        out_specs=(pl.BlockSpec((None, blk, D_ATT), lambda bi, r, nb: (bi, nb, r)),
                   pl.BlockSpec((None, blk, 128), lambda bi, r, nb: (bi, nb, r))),
        compiler_params=pltpu.CompilerParams(
            dimension_semantics=("parallel", "parallel", "arbitrary"),
            vmem_limit_bytes=VMEM_LIMIT_BYTES),
        name=f"attn_prompt_d{dil}",
    )(pv, pv, pv, pv, pv, bias)
    return o.reshape(b * n_sub, dil * D_ATT), lse.reshape(b * n_sub, dil * 128)


def _prompt_bias(rpb, window, dil):
    span = window // dil
    blk = ATT_BLOCK
    assert span == blk
    table = rpb[_rel_bucket(jnp.arange(span + 1) * dil)].T.astype(F32)
    fill = jnp.full((H_ATT, blk), -jnp.inf, F32)
    row = jnp.concatenate([fill, table[:, ::-1], fill], axis=1)
    wrapped = jnp.tile(row, (1, blk))[:, :blk * 3 * blk].reshape(H_ATT, blk, 3 * blk)
    return wrapped[:, :, blk:]


def _sample_log_bias(rpb, n_q, wb):
    reach = max(w for w, _ in DILATED_PATTERNS) + 2 * 8
    assert wb + n_q <= reach
    terms = []
    for window, dil in DILATED_PATTERNS:
        span = window // dil
        table = rpb[_rel_bucket(jnp.arange(span + 1) * dil)].T.astype(F32)
        gaps = jnp.full((H_ATT, span + 1, dil - 1), -jnp.inf, F32)
        by_dist = jnp.concatenate([table[:, :, None], gaps], axis=2).reshape(H_ATT, (span + 1) * dil)
        by_dist = by_dist[:, :reach]
        terms.append(jnp.pad(by_dist, ((0, 0), (0, reach - by_dist.shape[1])), constant_values=-jnp.inf))
    lb = jax.nn.logsumexp(jnp.stack(terms), axis=0)
    neg = jnp.full((H_ATT,), -jnp.inf, F32)
    rows_c, rows_n = [], []
    for r in range(8):
        n = r % n_q
        rows_c.append(lb[:, n + 1:n + 1 + wb][:, ::-1])
        rows_n.append(jnp.stack([lb[:, n - m] if (m < n_q and m <= n) else neg for m in range(8)], axis=1))
    return jnp.stack(rows_c, axis=1), jnp.stack(rows_n, axis=1)


TN_DIMS = (((0,), (0,)), ((), ()))
GROUP_W = D_SSM // SSM_GROUPS


def _split_bf16(v, pieces):
    out = []
    for _ in range(pieces):
        hi = v.astype(BF16)
        out.append(hi)
        v = v - hi.astype(F32)
    return out


def _dot_split(v, rhs_bf16, pieces, dims=None):
    acc = None
    for piece in _split_bf16(v, pieces):
        if dims is None:
            t = jnp.dot(piece, rhs_bf16, preferred_element_type=F32)
        else:
            t = lax.dot_general(piece, rhs_bf16, dims, preferred_element_type=F32)
        acc = t if acc is None else acc + t
    return acc


def _softplus(v):
    return jnp.maximum(v, 0.0) + jnp.log1p(jnp.exp(-jnp.abs(v)))


def _causal_conv(ext_ref, rows, cw_ref, cb_ref):
    conv = cb_ref[...]
    for k in range(CONV_W):
        conv = conv + cw_ref[CONV_W - 1 - k:CONV_W - k, :] * ext_ref[8 - k:8 - k + rows, :]
    return _silu(conv)


def _gated_group_norm(y, z, g):
    gated = y * _silu(z)
    parts = []
    for grp in range(SSM_GROUPS):
        seg = gated[:, grp * GROUP_W:(grp + 1) * GROUP_W]
        parts.append(seg * lax.rsqrt(jnp.mean(seg * seg, axis=-1, keepdims=True) + NORM_EPS))
    return jnp.concatenate(parts, axis=1) * g


def _ssd_prompt_kernel(z_ref, xd_ref, cw_ref, cb_ref, dtb_ref, alog_ref, dsk_ref, g_ref,
                       y_ref, hout_ref, ht_ref, ext_ref):
    c = pl.program_id(1)
    q = SSD_CHUNK

    @pl.when(c == 0)
    def _():
        ht_ref[...] = jnp.zeros_like(ht_ref)
        ext_ref[0:8, :] = jnp.zeros((8, CONV_DIM), F32)

    x_raw = xd_ref[:, 0:CONV_DIM]
    ext_ref[8:8 + q, :] = x_raw
    act = _causal_conv(ext_ref, q, cw_ref, cb_ref)
    ext_ref[0:8, :] = x_raw[q - 8:q, :]
    xs = act[:, 0:D_SSM]
    bm = act[:, D_SSM:D_SSM + SSM_GROUPS * D_STATE]
    cm = act[:, D_SSM + SSM_GROUPS * D_STATE:CONV_DIM]

    dt = _softplus(xd_ref[:, CONV_DIM:CONV_DIM + 128] + dtb_ref[...])
    da = dt * (-jnp.exp(alog_ref[...]))
    ti = lax.broadcasted_iota(jnp.int32, (q, q), 0)
    si = lax.broadcasted_iota(jnp.int32, (q, q), 1)
    causal = ti >= si
    a_cum = None
    for piece in _split_bf16(da, 3):
        t = jnp.dot(causal.astype(BF16), piece, preferred_element_type=F32)
        a_cum = t if a_cum is None else a_cum + t
    a_cum_t = a_cum.T
    expand = _head_lane_mask(128).astype(BF16)
    eac = jnp.exp(a_cum)
    dt_x = _dot_split(dt, expand, 2)
    eac_x = _dot_split(eac, expand, 2)
    dend_x = _dot_split(jnp.exp(a_cum[q - 1:q, :] - a_cum), expand, 2)
    xdt = xs * dt_x
    xdd = (xdt * dend_x).astype(BF16)
    xdt_b = xdt.astype(BF16)
    lane = lax.broadcasted_iota(jnp.int32, (q, 2 * SSM_HEAD_DIM), 1)
    lo = lane < SSM_HEAD_DIM
    heads_per_group = H_SSM // SSM_GROUPS
    for grp in range(SSM_GROUPS):
        gs = slice(grp * GROUP_W, (grp + 1) * GROUP_W)
        bm_g = bm[:, grp * D_STATE:(grp + 1) * D_STATE]
        cm_b = cm[:, grp * D_STATE:(grp + 1) * D_STATE].astype(BF16)
        cb_mat = lax.dot_general(cm_b, bm_g.astype(BF16), NT_DIMS, preferred_element_type=F32)
        h_in = ht_ref[:, gs]
        y_off = jnp.dot(cm_b, h_in.astype(BF16), preferred_element_type=F32) * eac_x[:, gs]
        states = jnp.dot(bm_g.T.astype(BF16), xdd[:, gs], preferred_element_type=F32)
        ht_ref[:, gs] = h_in * eac_x[q - 1:q, gs] + states
        for hp in range(heads_per_group // 2):
            h0 = grp * heads_per_group + 2 * hp
            ps = slice(h0 * SSM_HEAD_DIM, (h0 + 2) * SSM_HEAD_DIM)
            x_pair = xdt_b[:, ps]
            halves = []
            for half in range(2):
                h = h0 + half
                seg = a_cum[:, h:h + 1] - a_cum_t[h:h + 1, :]
                decay = jnp.exp(jnp.where(causal, seg, -jnp.inf))
                halves.append(jnp.dot((cb_mat * decay).astype(BF16), x_pair, preferred_element_type=F32))
            y_diag = jnp.where(lo, halves[0], halves[1])
            off = slice(2 * hp * SSM_HEAD_DIM, (2 * hp + 2) * SSM_HEAD_DIM)
            y_ref[:, ps] = y_diag + y_off[:, off] + dsk_ref[:, ps] * xs[:, ps]
    y_ref[...] = _gated_group_norm(y_ref[...], z_ref[...], g_ref[...])

    @pl.when(c == pl.num_programs(1) - 1)
    def _():
        hout_ref[...] = ht_ref[...].T


def _ssd_prompt_call(proj, sp):
    b, s_len, _ = proj.shape
    q = SSD_CHUNK
    const = lambda shape: pl.BlockSpec(shape, lambda bi, c: (0,) * len(shape))
    y, h_last = pl.pallas_call(
        _ssd_prompt_kernel,
        out_shape=(jax.ShapeDtypeStruct((b, s_len, D_SSM), F32),
                   jax.ShapeDtypeStruct((b, D_SSM, D_STATE), F32)),
        grid=(b, s_len // q),
        in_specs=[pl.BlockSpec((None, q, D_SSM), lambda bi, c: (bi, c, 3)),
                  pl.BlockSpec((None, q, 2048), lambda bi, c: (bi, c, 2)),
                  const((CONV_W, CONV_DIM)), const((1, CONV_DIM)), const((1, 128)), const((1, 128)),
                  const((1, D_SSM)), const((1, D_SSM))],
        out_specs=(pl.BlockSpec((None, q, D_SSM), lambda bi, c: (bi, c, 0)),
                   pl.BlockSpec((None, D_SSM, D_STATE), lambda bi, c: (bi, 0, 0))),
        scratch_shapes=[pltpu.VMEM((D_STATE, D_SSM), F32), pltpu.VMEM((8 + q, CONV_DIM), F32)],
        compiler_params=pltpu.CompilerParams(
            dimension_semantics=("parallel", "arbitrary"), vmem_limit_bytes=VMEM_LIMIT_BYTES),
        name="ssd_prompt",
    )(proj, proj, sp['conv_w'], sp['conv_b'], sp['dt_bias'], sp['a_log'], sp['d_skip_x'], sp['g_ssm_out'])
    return y.reshape(b * s_len, D_SSM), h_last.reshape(b, H_SSM, SSM_HEAD_DIM, D_STATE)


def _ssd_sample_kernel(z_ref, xd_ref, buf_ref, h0_ref, cw_ref, cb_ref, dtb_ref, alog_ref, dsk_ref, g_ref,
                       y_ref, hout_ref, ext_ref):
    n_seq, n_tok, _ = z_ref.shape
    expand = _head_lane_mask(128).astype(BF16)
    ti = lax.broadcasted_iota(jnp.int32, (8, 8), 0)
    si = lax.broadcasted_iota(jnp.int32, (8, 8), 1)
    causal = (ti >= si) & (si < n_tok)
    real = lax.broadcasted_iota(jnp.int32, (8, 1), 0) < n_tok
    ones = jnp.ones((8, D_STATE), BF16)
    ext_ref[...] = jnp.zeros_like(ext_ref)

    def one_sequence(j, ext_ref):
        ext_ref[8 - (CONV_W - 1):8, :] = buf_ref[j]
        ext_ref[8:8 + n_tok, :] = xd_ref[j][:, 0:CONV_DIM]
        act = _causal_conv(ext_ref, 8, cw_ref, cb_ref)
        xs = act[:, 0:D_SSM]
        bm = act[:, D_SSM:D_SSM + SSM_GROUPS * D_STATE]
        cm = act[:, D_SSM + SSM_GROUPS * D_STATE:CONV_DIM]
        pad = jnp.zeros((8 - n_tok, 128), F32)
        dt = _softplus(jnp.concatenate([xd_ref[j][:, CONV_DIM:CONV_DIM + 128], pad], axis=0) + dtb_ref[...])
        da = jnp.where(real, dt * (-jnp.exp(alog_ref[...])), 0.0)
        a_cum = None
        for piece in _split_bf16(da, 3):
            t = jnp.dot((ti >= si).astype(BF16), piece, preferred_element_type=F32)
            a_cum = t if a_cum is None else a_cum + t
        a_last = a_cum[n_tok - 1:n_tok, :]
        acum_x = _dot_split(a_cum, expand, 3)
        eac_x = jnp.exp(acum_x)
        dt_x = _dot_split(dt, expand, 2)
        xdt = jnp.where(real, xs * dt_x, 0.0)
        xdd = (xdt * jnp.exp(acum_x[n_tok - 1:n_tok, :] - acum_x)).astype(BF16)
        last_x = jnp.where(lax.broadcasted_iota(jnp.int32, (8, 1), 0) == n_tok - 1, eac_x, 0.0)
        keep = _dot_split(last_x, ones, 2, dims=TN_DIMS)
        y_parts, new_state = [], []
        for grp in range(SSM_GROUPS):
            gs = slice(grp * GROUP_W, (grp + 1) * GROUP_W)
            bm_b = bm[:, grp * D_STATE:(grp + 1) * D_STATE].astype(BF16)
            cm_b = cm[:, grp * D_STATE:(grp + 1) * D_STATE].astype(BF16)
            cb_mat = jnp.where(causal, lax.dot_general(cm_b, bm_b, NT_DIMS, preferred_element_type=F32), 0.0)
            h0_g = h0_ref[j, gs, :]
            y_g = lax.dot_general(cm_b, h0_g.astype(BF16), NT_DIMS, preferred_element_type=F32) * eac_x[:, gs]
            for s in range(n_tok):
                coef = cb_mat[:, s:s + 1] * jnp.exp(jnp.where(ti[:, 0:1] >= s, acum_x[:, gs] - acum_x[s:s + 1, gs], -jnp.inf))
                y_g = y_g + coef * xdt[s:s + 1, gs]
            y_parts.append(y_g)
            states = lax.dot_general(xdd[:, gs], bm_b, TN_DIMS, preferred_element_type=F32)
            hout_ref[j, gs, :] = h0_g * keep[gs, :] + states
        y = jnp.concatenate(y_parts, axis=1) + dsk_ref[...] * xs
        zj = jnp.concatenate([z_ref[j], jnp.zeros((8 - n_tok, D_SSM), F32)], axis=0)
        y_ref[j] = _gated_group_norm(y, zj, g_ref[...])[0:n_tok, :]

    def body(jj, carry):
        for slot in range(2):
            one_sequence(2 * jj + slot, ext_ref.at[slot])
        return carry

    lax.fori_loop(0, n_seq // 2, body, 0)


def _ssd_sample_call(proj, conv_buf, h0, sp):
    b, n_tok, _ = proj.shape
    nseq = 8
    const = lambda shape: pl.BlockSpec(shape, lambda i: (0,) * len(shape))
    y, h_last = pl.pallas_call(
        _ssd_sample_kernel,
        out_shape=(jax.ShapeDtypeStruct((b, n_tok, D_SSM), F32),
                   jax.ShapeDtypeStruct((b, D_SSM, D_STATE), F32)),
        grid=(b // nseq,),
        in_specs=[pl.BlockSpec((nseq, n_tok, D_SSM), lambda i: (i, 0, 3)),
                  pl.BlockSpec((nseq, n_tok, 2048), lambda i: (i, 0, 2)),
                  pl.BlockSpec((nseq, CONV_W - 1, CONV_DIM), lambda i: (i, 0, 0)),
                  pl.BlockSpec((nseq, D_SSM, D_STATE), lambda i: (i, 0, 0)),
                  const((CONV_W, CONV_DIM)), const((1, CONV_DIM)), const((1, 128)), const((1, 128)),
                  const((1, D_SSM)), const((1, D_SSM))],
        out_specs=(pl.BlockSpec((nseq, n_tok, D_SSM), lambda i: (i, 0, 0)),
                   pl.BlockSpec((nseq, D_SSM, D_STATE), lambda i: (i, 0, 0))),
        scratch_shapes=[pltpu.VMEM((2, 16, CONV_DIM), F32)],
        compiler_params=pltpu.CompilerParams(
            dimension_semantics=("parallel",), vmem_limit_bytes=VMEM_LIMIT_BYTES),
        name="ssd_sample",
    )(proj, proj, conv_buf, h0.reshape(b, D_SSM, D_STATE), sp['conv_w'], sp['conv_b'], sp['dt_bias'],
      sp['a_log'], sp['d_skip_x'], sp['g_ssm_out'])
    return y.reshape(b * n_tok, D_SSM), h_last.reshape(b, H_SSM, SSM_HEAD_DIM, D_STATE)


def _split_proj(proj, b, L, keep):
    proj = proj.reshape(b, L, D_IN_PAD)
    k = proj[:, L - keep:, D_ATT:2 * D_ATT].reshape(b, keep, H_ATT, HEAD_DIM)
    v = proj[:, L - keep:, 2 * D_ATT:3 * D_ATT].reshape(b, keep, H_ATT, HEAD_DIM)
    o = 3 * D_ATT + D_SSM
    return proj, k, v, proj[:, L - (CONV_W - 1):, o:o + CONV_DIM]


def kernel(x_prompt, x_sample, cache_k, cache_v, state_ssm, state_conv, c_prompt, c_sample,
           rpb_table, w_ada, b_ada, g_ffn1, w_gate1, w_up1, w_down1, g_mix, w_in, conv_w, conv_b,
           dt_bias, a_log, d_skip, g_ssm_out, g_attn_out, w_out, g_ffn2, w_gate2, w_up2, w_down2, g_final):
    bp, sp, _ = x_prompt.shape
    bs, ls, _ = x_sample.shape
    wbp = min(WINDOW_MAX, sp)
    layer = 0
    w_in_pad = jnp.pad(jnp.transpose(w_in[layer]).astype(BF16), ((0, D_IN_PAD - D_IN), (0, 0)))
    p = {'g_ffn1': g_ffn1[layer], 'w_gate1': w_gate1[layer].astype(BF16), 'w_up1': w_up1[layer].astype(BF16),
         'w_down1': w_down1[layer].astype(BF16), 'g_mix': g_mix[layer], 'w_in_pad': w_in_pad,
         'conv_w': conv_w[layer], 'conv_b': conv_b[layer].reshape(1, CONV_DIM),
         'dt_bias': jnp.pad(dt_bias[layer], (0, 128 - H_SSM)).reshape(1, 128),
         'a_log': jnp.pad(a_log[layer], (0, 128 - H_SSM)).reshape(1, 128),
         'd_skip_x': jnp.repeat(d_skip[layer], SSM_HEAD_DIM).reshape(1, D_SSM),
         'g_ssm_out': g_ssm_out[layer].reshape(1, D_SSM), 'g_attn_out': g_attn_out[layer],
         'w_out_att': w_out[layer, :D_ATT].astype(BF16), 'w_out_ssm': w_out[layer, D_ATT:].astype(BF16),
         'g_ffn2': g_ffn2[layer], 'w_gate2': w_gate2[layer].astype(BF16), 'w_up2': w_up2[layer].astype(BF16),
         'w_down2': w_down2[layer].astype(BF16)}

    n_c = bp + bs
    c_rows = -(-n_c // 16) * 16
    c_all = jnp.pad(jnp.concatenate([c_prompt, c_sample], axis=0), ((0, c_rows - n_c), (0, 0)))
    mod = _mod_call(c_all, w_ada[layer], b_ada[layer])
    mods_p = [mod[:bp, i * D_MODEL:(i + 1) * D_MODEL].reshape(bp, 1, D_MODEL) for i in range(N_MOD)]
    mods_s = [jnp.repeat(mod[bp:n_c, i * D_MODEL:(i + 1) * D_MODEL], ls, axis=0) for i in range(N_MOD)]

    lb_c, lb_n = _sample_log_bias(rpb_table, ls, cache_k.shape[2])
    cache_kt = jnp.transpose(cache_k[layer], (0, 2, 3, 1))
    cache_vt = jnp.transpose(cache_v[layer], (0, 2, 3, 1))
    dils = tuple(d for _, d in DILATED_PATTERNS)
    sh1, sc1, g1, sh2, sc2, g2, sh3, sc3, g3 = mods_p
    th1, tc1, t1, th2, tc2, t2, th3, tc3, t3 = mods_s
    ffn1 = (p['g_ffn1'], p['w_gate1'], p['w_up1'], p['w_down1'])
    ffn2 = (p['g_ffn2'], p['w_gate2'], p['w_up2'], p['w_down2'])
    kw_p = dict(tm=1024, rows_per_mod=sp)
    kw_s = dict(tm=bs * ls, rows_per_mod=ls)

    xs = _ffn_call(x_sample.reshape(bs * ls, D_MODEL), ffn1[0], th1, tc1, t1, *ffn1[1:], g_final,
                   final_norm=False, **kw_s)
    (proj_s,) = _inproj_call(xs, p['g_mix'], th2, tc2, p['w_in_pad'], dilations=(), **kw_s)
    proj_s, kn, vn, cs = _split_proj(proj_s, bs, ls, ls)
    rider = lambda first, n: (proj_s, cache_kt, cache_vt, lb_c, lb_n, first, n)

    xp, att_a = _ffn_call(x_prompt.reshape(bp * sp, D_MODEL), ffn1[0], sh1, sc1, g1, *ffn1[1:], g_final,
                          final_norm=False, rider=rider(0, bs // 2), **kw_p)
    proj_p, *qkv = _inproj_call(xp, p['g_mix'], sh2, sc2, p['w_in_pad'], dilations=dils, **kw_p)
    proj_p, kp, vp, cp = _split_proj(proj_p, bp, sp, wbp)
    outs, lses = [], []
    for (window, dil), qkv_d in zip(DILATED_PATTERNS, qkv):
        o, lse = _attn_prompt_call(qkv_d, _prompt_bias(rpb_table, window, dil), dil, bp)
        outs.append(o)
        lses.append(lse)
    ssd_p, hp = _ssd_prompt_call(proj_p, p)
    xp = _outproj_call(xp, outs + lses, ssd_p, g2, p['w_out_att'], p['w_out_ssm'], p['g_attn_out'],
                       tm=256, rows_per_mod=sp, dilations=dils)
    yp, att_b = _ffn_call(xp, ffn2[0], sh3, sc3, g3, *ffn2[1:], g_final,
                          final_norm=True, rider=rider(bs // 2, bs - bs // 2), **kw_p)

    att_s = jnp.concatenate([att_a, att_b], axis=0).reshape(bs * ls, D_ATT)
    ssd_s, hs = _ssd_sample_call(proj_s, state_conv[layer], state_ssm[layer], p)
    xs = _outproj_call(xs, [att_s], ssd_s, t2, p['w_out_att'], p['w_out_ssm'], p['g_attn_out'],
                       tm=256, rows_per_mod=ls, dilations=())
    ys = _ffn_call(xs, ffn2[0], th3, tc3, t3, *ffn2[1:], g_final, final_norm=True, **kw_s)
    return (yp.reshape(bp, sp, D_MODEL), ys.reshape(bs, ls, D_MODEL), kp[None], vp[None],
            kn[None], vn[None], hp[None], hs[None], cp[None], cs[None])
```

```python
import functools
import math

import jax
import jax.numpy as jnp
from jax import lax
from jax.experimental import pallas as pl
from jax.experimental.pallas import tpu as pltpu

D_MODEL = 2048
HEAD_DIM = 64
D_ATT = 1024
D_SSM = 1024
H_ATT = 16
SSM_HEAD_DIM = 64
H_SSM = 16
SSM_GROUPS = 2
D_STATE = 128
CONV_W = 4
SSD_CHUNK = 128
CONV_DIM = D_SSM + 2 * SSM_GROUPS * D_STATE
D_FF = 5632
DILATED_PATTERNS = ((128, 1), (512, 4), (2048, 16))
WINDOW_MAX = 2048
ATT_BLOCK = 128
RPB_BUCKETS = 32
RPB_MAX_DIST = 2048
NORM_EPS = 1e-6
N_MOD = 9
PROJ_SIZES = (D_ATT, D_ATT, D_ATT, D_SSM, CONV_DIM, H_SSM)
D_IN = sum(PROJ_SIZES)
D_IN_PAD = 6144

VMEM_LIMIT_BYTES = 56 * 1024 * 1024

BF16 = jnp.bfloat16
F32 = jnp.float32


def _silu(v):
    return v * jax.nn.sigmoid(v)


def _norm_mod(x, g, shift, scale):
    y = x * lax.rsqrt(jnp.mean(x * x, axis=-1, keepdims=True) + NORM_EPS)
    return (y * g) * (1.0 + scale) + shift


def _mod_kernel(c_ref, w_ref, b_ref, o_ref, a_ref):
    @pl.when(pl.program_id(0) == 0)
    def _():
        a_ref[...] = _silu(c_ref[...]).astype(BF16)

    o_ref[...] = jnp.dot(a_ref[...], w_ref[...].astype(BF16),
                         preferred_element_type=F32) + b_ref[...]


def _mod_call(c_all, w_ada, b_ada):
    rows = c_all.shape[0]
    n = w_ada.shape[1]
    tn = 1024
    return pl.pallas_call(
        _mod_kernel,
        out_shape=jax.ShapeDtypeStruct((rows, n), F32),
        grid=(n // tn,),
        in_specs=[
            pl.BlockSpec((rows, D_MODEL), lambda j: (0, 0)),
            pl.BlockSpec((D_MODEL, tn), lambda j: (0, j)),
            pl.BlockSpec((1, tn), lambda j: (0, j)),
        ],
        out_specs=pl.BlockSpec((rows, tn), lambda j: (0, j)),
        scratch_shapes=[pltpu.VMEM((rows, D_MODEL), BF16)],
        compiler_params=pltpu.CompilerParams(
            dimension_semantics=("arbitrary",), vmem_limit_bytes=VMEM_LIMIT_BYTES),
        name="adaln_mod",
    )(c_all, w_ada, b_ada.reshape(1, n))


NT_DIMS = (((1,), (1,)), ((), ()))
RIDER_HEADS = 8
RIDER_GROUP = 8


def _sample_attn_heads(q_ref, kn_ref, vn_ref, kt_ref, vt_ref, lbc_ref, lbn_ref, o_ref):
    n_q, width = q_ref.shape
    pad = jnp.zeros((8 - n_q, width), F32)
    q8 = jnp.concatenate([q_ref[...] * (HEAD_DIM ** -0.5), pad], axis=0).astype(BF16)
    kn8 = jnp.concatenate([kn_ref[...], pad], axis=0).astype(BF16)
    vn8 = jnp.concatenate([vn_ref[...], pad], axis=0).astype(BF16)
    heads = width // HEAD_DIM
    wb = kt_ref.shape[2]
    hs = lambda h: slice(h * HEAD_DIM, (h + 1) * HEAD_DIM)
    group = RIDER_GROUP
    for g0 in range(0, heads, group):
        hh = range(g0, g0 + group)
        t_c = jnp.concatenate([jnp.dot(q8[:, hs(h)], kt_ref[h].astype(BF16), preferred_element_type=F32)
                               for h in hh], axis=0) + lbc_ref[g0:g0 + group].reshape(group * 8, wb)
        t_n = jnp.concatenate([lax.dot_general(q8[:, hs(h)], kn8[:, hs(h)], NT_DIMS,
                                               preferred_element_type=F32)
                               for h in hh], axis=0) + lbn_ref[g0:g0 + group].reshape(group * 8, 8)
        m = jnp.maximum(jnp.max(t_c, axis=-1, keepdims=True), jnp.max(t_n, axis=-1, keepdims=True))
        p_c = jnp.exp(t_c - m)
        p_n = jnp.exp(t_n - m)
        inv_l = 1.0 / (jnp.sum(p_c, axis=-1, keepdims=True) + jnp.sum(p_n, axis=-1, keepdims=True))
        for i, h in enumerate(hh):
            rows = slice(i * 8, (i + 1) * 8)
            acc = lax.dot_general(p_c[rows].astype(BF16), vt_ref[h].astype(BF16), NT_DIMS,
                                  preferred_element_type=F32)
            acc += jnp.dot(p_n[rows].astype(BF16), vn8[:, hs(h)], preferred_element_type=F32)
            o_ref[:, hs(h)] = (acc * inv_l[rows])[0:n_q, :]


def _ffn_kernel(*refs, final_norm, rider, emit_weights):
    x_ref, g_ref, sh_ref, sc_ref, gt_ref, wg_ref, wu_ref, wd_ref, gf_ref = refs[:9]
    if rider:
        rider_in, (o_ref, att_ref, h_ref) = refs[9:16], refs[16:]
    elif emit_weights:
        o_ref, wgb_ref, wub_ref, wdb_ref, h_ref = refs[9:]
    else:
        o_ref, h_ref = refs[9:]
    j = pl.program_id(1)

    @pl.when(j == 0)
    def _():
        h_ref[...] = _norm_mod(x_ref[...], g_ref[...], sh_ref[...], sc_ref[...]).astype(BF16)
        o_ref[...] = jnp.zeros_like(o_ref)

    wg, wu, wd = wg_ref[...], wu_ref[...], wd_ref[...]
    if emit_weights:
        wg, wu, wd = wg.astype(BF16), wu.astype(BF16), wd.astype(BF16)
        wgb_ref[...] = wg
        wub_ref[...] = wu
        wdb_ref[...] = wd
    h = h_ref[...]
    a = jnp.dot(h, wg, preferred_element_type=F32)
    b = jnp.dot(h, wu, preferred_element_type=F32)
    s = (_silu(a) * b).astype(BF16)
    o_ref[...] += jnp.dot(s, wd, preferred_element_type=F32)
    if rider:
        @pl.when(pl.program_id(0) * pl.num_programs(1) + j < rider)
        def _():
            _sample_attn_heads(*rider_in, att_ref)

    @pl.when(j == pl.num_programs(1) - 1)
    def _():
        y = x_ref[...] + 0.5 * gt_ref[...] * o_ref[...]
        if final_norm:
            y = y * lax.rsqrt(jnp.mean(y * y, axis=-1, keepdims=True) + NORM_EPS) * gf_ref[...]
        o_ref[...] = y


def _mod_spec(mod, tm, rows_per_mod):
    if mod.ndim == 3:
        tiles_per_row = rows_per_mod // tm
        return pl.BlockSpec((None, 1, D_MODEL), lambda i, j: (i // tiles_per_row, 0, 0))
    return pl.BlockSpec((tm, D_MODEL), lambda i, j: (i, 0))


def _ffn_call(x, g, sh, sc, gt, wg, wu, wd, g_final, *, tm, rows_per_mod, final_norm, rider=None,
              emit_weights=False):
    m = x.shape[0]
    assert not (rider and emit_weights) and (not emit_weights or m == tm)
    tf = 256 if (rider or emit_weights) else 512
    nj = D_FF // tf
    row = lambda i, j: (i, 0)
    const = lambda i, j: (0, 0)
    in_specs = [
        pl.BlockSpec((tm, D_MODEL), row, pipeline_mode=pl.Buffered(1)),
        pl.BlockSpec((1, D_MODEL), const),
        _mod_spec(sh, tm, rows_per_mod),
        _mod_spec(sc, tm, rows_per_mod),
        _mod_spec(gt, tm, rows_per_mod),
        pl.BlockSpec((D_MODEL, tf), lambda i, j: (0, j)),
        pl.BlockSpec((D_MODEL, tf), lambda i, j: (0, j)),
        pl.BlockSpec((tf, D_MODEL), lambda i, j: (j, 0)),
        pl.BlockSpec((1, D_MODEL), const),
    ]
    args = [x, g.reshape(1, D_MODEL), sh, sc, gt, wg, wu, wd, g_final.reshape(1, D_MODEL)]
    out_shape = jax.ShapeDtypeStruct((m, D_MODEL), F32)
    out_specs = pl.BlockSpec((tm, D_MODEL), row)
    if rider:
        proj_s, cache_kt, cache_vt, lb_c, lb_n, first_seq, n_seq = rider
        n_q = proj_s.shape[1]
        wb = cache_kt.shape[3]
        halves = H_ATT // RIDER_HEADS
        n_units = n_seq * halves
        assert n_units <= (m // tm) * nj
        rw = RIDER_HEADS * HEAD_DIM
        unit = lambda i, j: jnp.minimum(i * nj + j, n_units - 1)
        seq = lambda i, j: first_seq + unit(i, j) // halves
        half = lambda i, j: unit(i, j) % halves
        tiles = D_ATT // rw
        qspec = lambda t: pl.BlockSpec((None, n_q, rw), lambda i, j: (seq(i, j), 0, t * tiles + half(i, j)))
        cspec = pl.BlockSpec((None, RIDER_HEADS, HEAD_DIM, wb), lambda i, j: (seq(i, j), half(i, j), 0, 0))
        in_specs += [qspec(0), qspec(1), qspec(2), cspec, cspec,
                     pl.BlockSpec((RIDER_HEADS, 8, wb), lambda i, j: (half(i, j), 0, 0)),
                     pl.BlockSpec((RIDER_HEADS, 8, 8), lambda i, j: (half(i, j), 0, 0))]
        args += [proj_s, proj_s, proj_s, cache_kt, cache_vt, lb_c, lb_n]
        out_shape = (out_shape, jax.ShapeDtypeStruct((n_seq, n_q, D_ATT), F32))
        out_specs = (pl.BlockSpec((tm, D_MODEL), row, pipeline_mode=pl.Buffered(1)),
                     pl.BlockSpec((None, n_q, rw), lambda i, j: (seq(i, j) - first_seq, 0, half(i, j))))
    if emit_weights:
        out_shape = (out_shape, jax.ShapeDtypeStruct(wg.shape, BF16), jax.ShapeDtypeStruct(wu.shape, BF16),
                     jax.ShapeDtypeStruct(wd.shape, BF16))
        out_specs = (out_specs,) + tuple(in_specs[5:8])
    return pl.pallas_call(
        functools.partial(_ffn_kernel, final_norm=final_norm, rider=n_units if rider else 0,
                          emit_weights=emit_weights),
        out_shape=out_shape,
        grid=(m // tm, nj),
        in_specs=in_specs,
        out_specs=out_specs,
        scratch_shapes=[pltpu.VMEM((tm, D_MODEL), BF16)],
        compiler_params=pltpu.CompilerParams(
            dimension_semantics=("arbitrary", "arbitrary") if rider else ("parallel", "arbitrary"),
            vmem_limit_bytes=VMEM_LIMIT_BYTES),
        name=("ffn_final" if final_norm else "ffn") + ("_rider" if rider else ""),
    )(*args)


QKV_TILES = 3


def _inproj_kernel(x_ref, g_ref, sh_ref, sc_ref, w_ref, o_ref, *rest, dilations):
    qkv_refs, h_ref = rest[:-1], rest[-1]
    j = pl.program_id(1)
    tm = o_ref.shape[0]

    @pl.when(j == 0)
    def _():
        h_ref[...] = _norm_mod(x_ref[...], g_ref[...], sh_ref[...], sc_ref[...]).astype(BF16)

    res = lax.dot_general(h_ref[...], w_ref[...], NT_DIMS, preferred_element_type=F32)
    o_ref[...] = res

    @pl.when(j < QKV_TILES)
    def _():
        res_b = res.astype(BF16)
        part = min(tm, 512)
        out_row = lax.broadcasted_iota(jnp.int32, (part, part), 0)
        in_row = lax.broadcasted_iota(jnp.int32, (part, part), 1)
        for d, ref in zip(dilations, qkv_refs):
            if d == 1:
                ref[...] = res_b
                continue
            n = part // d
            perm = (in_row == (out_row % n) * d + out_row // n).astype(BF16)
            for s in range(tm // part):
                grouped = jnp.dot(perm, res_b[s * part:(s + 1) * part, :], preferred_element_type=F32)
                grouped = grouped.astype(BF16)
                for r in range(d):
                    ref[s * n:(s + 1) * n, r * D_ATT:(r + 1) * D_ATT] = grouped[r * n:(r + 1) * n, :]


def _inproj_call(x, g, sh, sc, w_pad_t, *, tm, rows_per_mod, dilations):
    m = x.shape[0]
    tn = D_ATT
    qkv_j = lambda j: jnp.minimum(j, QKV_TILES - 1)
    return pl.pallas_call(
        functools.partial(_inproj_kernel, dilations=dilations),
        out_shape=(jax.ShapeDtypeStruct((m, D_IN_PAD), F32),)
        + tuple(jax.ShapeDtypeStruct((m // d, QKV_TILES * d * tn), BF16) for d in dilations),
        grid=(m // tm, D_IN_PAD // tn),
        in_specs=[
            pl.BlockSpec((tm, D_MODEL), lambda i, j: (i, 0), pipeline_mode=pl.Buffered(1)),
            pl.BlockSpec((1, D_MODEL), lambda i, j: (0, 0)),
            _mod_spec(sh, tm, rows_per_mod),
            _mod_spec(sc, tm, rows_per_mod),
            pl.BlockSpec((tn, D_MODEL), lambda i, j: (j, 0)),
        ],
        out_specs=(pl.BlockSpec((tm, tn), lambda i, j: (i, j)),)
        + tuple(pl.BlockSpec((tm // d, d * tn), lambda i, j: (i, qkv_j(j))) for d in dilations),
        scratch_shapes=[pltpu.VMEM((tm, D_MODEL), BF16)],
        compiler_params=pltpu.CompilerParams(
            dimension_semantics=("parallel", "arbitrary"), vmem_limit_bytes=VMEM_LIMIT_BYTES),
        name="in_proj",
    )(x, g.reshape(1, D_MODEL), sh, sc, w_pad_t)


def _outproj_kernel(*refs, dilations):
    x_ref = refs[0]
    n_pat = len(dilations)
    if n_pat:
        o_refs = refs[1:1 + n_pat]
        l_refs = refs[1 + n_pat:1 + 2 * n_pat]
        ga_ref, ssd_ref, gt_ref, wa_ref, ws_ref, out_ref, o_scr, l_scr = refs[1 + 2 * n_pat:]
        tm = x_ref.shape[0]
        n_chunks = D_ATT // 128
        outs, lses = [], []
        for p, (d, o_ref, l_ref) in enumerate(zip(dilations, o_refs, l_refs)):
            if d == 1:
                outs.append(o_ref[...].astype(F32))
                lses.append(l_ref[...])
                continue
            for r in range(d):
                l_scr[p, pl.ds(r, tm // d, stride=d), :] = l_ref[:, r * 128:(r + 1) * 128]
                for c in range(n_chunks):
                    lanes = slice(r * D_ATT + c * 128, r * D_ATT + (c + 1) * 128)
                    o_scr[p, c, pl.ds(r, tm // d, stride=d), :] = o_ref[:, lanes].astype(F32)
            outs.append(jnp.concatenate([o_scr[p, c] for c in range(n_chunks)], axis=1))
            lses.append(l_scr[p])
        mx = functools.reduce(jnp.maximum, lses)
        ws = [jnp.exp(v - mx) for v in lses]
        wsum = functools.reduce(lambda a, b: a + b, ws)
        expand = _head_lane_mask(128).astype(BF16)
        att = None
        for w, o in zip(ws, outs):
            wn = w / wsum
            w_hi = wn.astype(BF16)
            w_lo = (wn - w_hi.astype(F32)).astype(BF16)
            wx = (jnp.dot(w_hi, expand, preferred_element_type=F32)
                  + jnp.dot(w_lo, expand, preferred_element_type=F32))
            att = wx * o if att is None else att + wx * o
    else:
        att_ref, ga_ref, ssd_ref, gt_ref, wa_ref, ws_ref, out_ref = refs[1:]
        att = att_ref[...]
    att = att * lax.rsqrt(jnp.mean(att * att, axis=-1, keepdims=True) + NORM_EPS) * ga_ref[...]
    mix = jnp.dot(att.astype(BF16), wa_ref[...], preferred_element_type=F32)
    mix += jnp.dot(ssd_ref[...].astype(BF16), ws_ref[...], preferred_element_type=F32)
    out_ref[...] = x_ref[...] + gt_ref[...] * mix


def _outproj_call(x, att_parts, ssd, gt, w_att, w_ssd, g_attn, *, tm, rows_per_mod, dilations):
    m = x.shape[0]
    n_pat = len(dilations)
    if gt.ndim == 3:
        tiles_per_row = rows_per_mod // tm
        gt_spec = pl.BlockSpec((None, 1, D_MODEL), lambda i: (i // tiles_per_row, 0, 0))
    else:
        gt_spec = pl.BlockSpec((tm, D_MODEL), lambda i: (i, 0))
    row = lambda width, d=1: pl.BlockSpec((tm // d, d * width), lambda i: (i, 0))
    scratch = []
    if n_pat:
        att_specs = ([row(D_ATT, d) for d in dilations] + [row(128, d) for d in dilations]
                     + [pl.BlockSpec((1, D_ATT), lambda i: (0, 0))])
        att_args = list(att_parts) + [g_attn.reshape(1, D_ATT)]
        scratch = [pltpu.VMEM((n_pat, D_ATT // 128, tm, 128), F32), pltpu.VMEM((n_pat, tm, 128), F32)]
    else:
        att_specs = [row(D_ATT), pl.BlockSpec((1, D_ATT), lambda i: (0, 0))]
        att_args = list(att_parts) + [g_attn.reshape(1, D_ATT)]
    weight = lambda k: pl.BlockSpec((k, D_MODEL), lambda i: (0, 0), pipeline_mode=pl.Buffered(1))
    return pl.pallas_call(
        functools.partial(_outproj_kernel, dilations=dilations),
        out_shape=jax.ShapeDtypeStruct((m, D_MODEL), F32),
        grid=(m // tm,),
        in_specs=[row(D_MODEL)] + att_specs + [row(D_SSM), gt_spec, weight(D_ATT), weight(D_SSM)],
        out_specs=row(D_MODEL),
        scratch_shapes=scratch,
        compiler_params=pltpu.CompilerParams(
            dimension_semantics=("parallel",), vmem_limit_bytes=VMEM_LIMIT_BYTES),
        name="out_proj_merge" if n_pat else "out_proj",
    )(x, *att_args, ssd, gt, w_att, w_ssd)


def _rel_bucket(dist):
    max_exact = RPB_BUCKETS // 2
    df = jnp.maximum(dist, 1).astype(jnp.float32)
    large = max_exact + (jnp.log(df / max_exact) / math.log(RPB_MAX_DIST / max_exact)
                         * (RPB_BUCKETS - max_exact)).astype(jnp.int32)
    large = jnp.minimum(large, RPB_BUCKETS - 1)
    return jnp.where(dist < max_exact, dist, large)


def _head_lane_mask(rows=H_ATT):
    r = lax.broadcasted_iota(jnp.int32, (rows, D_ATT), 0)
    c = lax.broadcasted_iota(jnp.int32, (rows, D_ATT), 1)
    return (c // HEAD_DIM == r).astype(F32)


def _attn_prompt_kernel(q_ref, kp_ref, kc_ref, vp_ref, vc_ref, bias_ref, o_ref, lse_ref):
    nb = pl.program_id(2)
    blk = ATT_BLOCK
    col = lax.broadcasted_iota(jnp.int32, (blk, 2 * blk), 1)
    has_prev = (col >= blk) | (nb > 0)
    lane = lax.broadcasted_iota(jnp.int32, (blk, 2 * HEAD_DIM), 1)
    lo = lane < HEAD_DIM
    lse_lane = lax.broadcasted_iota(jnp.int32, (blk, 128), 1)
    lse_tile = jnp.zeros((blk, 128), F32)
    scale = HEAD_DIM ** -0.5
    for hp in range(H_ATT // 2):
        sl = slice(hp * 2 * HEAD_DIM, (hp + 1) * 2 * HEAD_DIM)
        q2 = q_ref[:, sl] * scale
        k2 = jnp.concatenate([kp_ref[:, sl], kc_ref[:, sl]], axis=0)
        v2 = jnp.concatenate([vp_ref[:, sl], vc_ref[:, sl]], axis=0)
        outs = []
        for half in range(2):
            h = 2 * hp + half
            qh = jnp.where(lo if half == 0 else ~lo, q2, jnp.zeros_like(q2))
            s = lax.dot_general(qh, k2, NT_DIMS, preferred_element_type=F32) + bias_ref[h]
            s = jnp.where(has_prev, s, -jnp.inf)
            m = jnp.max(s, axis=-1, keepdims=True)
            p = jnp.exp(s - m)
            l = jnp.sum(p, axis=-1, keepdims=True)
            pv = jnp.dot(p.astype(BF16), v2, preferred_element_type=F32)
            outs.append(pv / l)
            lse_tile = jnp.where(lse_lane == h, m + jnp.log(l), lse_tile)
        o_ref[:, sl] = jnp.where(lo, outs[0], outs[1]).astype(o_ref.dtype)
    lse_ref[...] = lse_tile


def _attn_prompt_call(qkv, bias, dil, b):
    blk = ATT_BLOCK
    n_sub = qkv.shape[0] // b
    s_len = n_sub * dil
    nbk = n_sub // blk
    pv = qkv.reshape(b, n_sub, qkv.shape[1])
    cur = lambda t: (lambda bi, r, nb: (bi, nb, t * dil + r))
    prev = lambda t: (lambda bi, r, nb: (bi, jnp.maximum(nb - 1, 0), t * dil + r))
    spec = lambda f: pl.BlockSpec((None, blk, D_ATT), f)
    o, lse = pl.pallas_call(
        _attn_prompt_kernel,
        out_shape=(jax.ShapeDtypeStruct((b, n_sub, dil * D_ATT), BF16),
                   jax.ShapeDtypeStruct((b, n_sub, dil * 128), F32)),
        grid=(b, dil, nbk),
        in_specs=[spec(cur(0)), spec(prev(1)), spec(cur(1)), spec(prev(2)), spec(cur(2)),
                  pl.BlockSpec((H_ATT, blk, 2 * blk), lambda bi, r, nb: (0, 0, 0))],
        out_specs=(pl.BlockSpec((None, blk, D_ATT), lambda bi, r, nb: (bi, nb, r)),
                   pl.BlockSpec((None, blk, 128), lambda bi, r, nb: (bi, nb, r))),
        compiler_params=pltpu.CompilerParams(
            dimension_semantics=("parallel", "parallel", "arbitrary"),
            vmem_limit_bytes=VMEM_LIMIT_BYTES),
        name=f"attn_prompt_d{dil}",
    )(pv, pv, pv, pv, pv, bias)
    return o.reshape(b * n_sub, dil * D_ATT), lse.reshape(b * n_sub, dil * 128)


def _prompt_bias(rpb, window, dil):
    span = window // dil
    blk = ATT_BLOCK
    assert span == blk
    table = rpb[_rel_bucket(jnp.arange(span + 1) * dil)].T.astype(F32)
    fill = jnp.full((H_ATT, blk), -jnp.inf, F32)
    row = jnp.concatenate([fill, table[:, ::-1], fill], axis=1)
    wrapped = jnp.tile(row, (1, blk))[:, :blk * 3 * blk].reshape(H_ATT, blk, 3 * blk)
    return wrapped[:, :, blk:]


def _sample_log_bias(rpb, n_q, wb):
    reach = max(w for w, _ in DILATED_PATTERNS) + 2 * 8
    assert wb + n_q <= reach
    terms = []
    for window, dil in DILATED_PATTERNS:
        span = window // dil
        table = rpb[_rel_bucket(jnp.arange(span + 1) * dil)].T.astype(F32)
        gaps = jnp.full((H_ATT, span + 1, dil - 1), -jnp.inf, F32)
        by_dist = jnp.concatenate([table[:, :, None], gaps], axis=2).reshape(H_ATT, (span + 1) * dil)
        by_dist = by_dist[:, :reach]
        terms.append(jnp.pad(by_dist, ((0, 0), (0, reach - by_dist.shape[1])), constant_values=-jnp.inf))
    lb = jax.nn.logsumexp(jnp.stack(terms), axis=0)
    neg = jnp.full((H_ATT,), -jnp.inf, F32)
    rows_c, rows_n = [], []
    for r in range(8):
        n = r % n_q
        rows_c.append(lb[:, n + 1:n + 1 + wb][:, ::-1])
        rows_n.append(jnp.stack([lb[:, n - m] if (m < n_q and m <= n) else neg for m in range(8)], axis=1))
    return jnp.stack(rows_c, axis=1), jnp.stack(rows_n, axis=1)


TN_DIMS = (((0,), (0,)), ((), ()))
GROUP_W = D_SSM // SSM_GROUPS


def _split_bf16(v, pieces):
    out = []
    for _ in range(pieces):
        hi = v.astype(BF16)
        out.append(hi)
        v = v - hi.astype(F32)
    return out


def _dot_split(v, rhs_bf16, pieces, dims=None):
    acc = None
    for piece in _split_bf16(v, pieces):
        if dims is None:
            t = jnp.dot(piece, rhs_bf16, preferred_element_type=F32)
        else:
            t = lax.dot_general(piece, rhs_bf16, dims, preferred_element_type=F32)
        acc = t if acc is None else acc + t
    return acc


def _softplus(v):
    return jnp.maximum(v, 0.0) + jnp.log1p(jnp.exp(-jnp.abs(v)))


def _causal_conv(ext_ref, rows, cw_ref, cb_ref):
    conv = cb_ref[...]
    for k in range(CONV_W):
        conv = conv + cw_ref[CONV_W - 1 - k:CONV_W - k, :] * ext_ref[8 - k:8 - k + rows, :]
    return _silu(conv)


def _gated_group_norm(y, z, g):
    gated = y * _silu(z)
    parts = []
    for grp in range(SSM_GROUPS):
        seg = gated[:, grp * GROUP_W:(grp + 1) * GROUP_W]
        parts.append(seg * lax.rsqrt(jnp.mean(seg * seg, axis=-1, keepdims=True) + NORM_EPS))
    return jnp.concatenate(parts, axis=1) * g


def _ssd_prompt_kernel(z_ref, xd_ref, cw_ref, cb_ref, dtb_ref, alog_ref, dsk_ref, g_ref,
                       y_ref, hout_ref, ht_ref, ext_ref):
    c = pl.program_id(1)
    q = SSD_CHUNK

    @pl.when(c == 0)
    def _():
        ht_ref[...] = jnp.zeros_like(ht_ref)
        ext_ref[0:8, :] = jnp.zeros((8, CONV_DIM), F32)

    x_raw = xd_ref[:, 0:CONV_DIM]
    ext_ref[8:8 + q, :] = x_raw
    act = _causal_conv(ext_ref, q, cw_ref, cb_ref)
    ext_ref[0:8, :] = x_raw[q - 8:q, :]
    xs = act[:, 0:D_SSM]
    bm = act[:, D_SSM:D_SSM + SSM_GROUPS * D_STATE]
    cm = act[:, D_SSM + SSM_GROUPS * D_STATE:CONV_DIM]

    dt = _softplus(xd_ref[:, CONV_DIM:CONV_DIM + 128] + dtb_ref[...])
    da = dt * (-jnp.exp(alog_ref[...]))
    ti = lax.broadcasted_iota(jnp.int32, (q, q), 0)
    si = lax.broadcasted_iota(jnp.int32, (q, q), 1)
    causal = ti >= si
    a_cum = None
    for piece in _split_bf16(da, 3):
        t = jnp.dot(causal.astype(BF16), piece, preferred_element_type=F32)
        a_cum = t if a_cum is None else a_cum + t
    a_cum_t = a_cum.T
    expand = _head_lane_mask(128).astype(BF16)
    eac = jnp.exp(a_cum)
    dt_x = _dot_split(dt, expand, 2)
    eac_x = _dot_split(eac, expand, 2)
    dend_x = _dot_split(jnp.exp(a_cum[q - 1:q, :] - a_cum), expand, 2)
    xdt = xs * dt_x
    xdd = (xdt * dend_x).astype(BF16)
    xdt_b = xdt.astype(BF16)
    lane = lax.broadcasted_iota(jnp.int32, (q, 2 * SSM_HEAD_DIM), 1)
    lo = lane < SSM_HEAD_DIM
    heads_per_group = H_SSM // SSM_GROUPS
    for grp in range(SSM_GROUPS):
        gs = slice(grp * GROUP_W, (grp + 1) * GROUP_W)
        bm_g = bm[:, grp * D_STATE:(grp + 1) * D_STATE]
        cm_b = cm[:, grp * D_STATE:(grp + 1) * D_STATE].astype(BF16)
        cb_mat = lax.dot_general(cm_b, bm_g.astype(BF16), NT_DIMS, preferred_element_type=F32)
        h_in = ht_ref[:, gs]
        y_off = jnp.dot(cm_b, h_in.astype(BF16), preferred_element_type=F32) * eac_x[:, gs]
        states = jnp.dot(bm_g.T.astype(BF16), xdd[:, gs], preferred_element_type=F32)
        ht_ref[:, gs] = h_in * eac_x[q - 1:q, gs] + states
        for hp in range(heads_per_group // 2):
            h0 = grp * heads_per_group + 2 * hp
            ps = slice(h0 * SSM_HEAD_DIM, (h0 + 2) * SSM_HEAD_DIM)
            x_pair = xdt_b[:, ps]
            halves = []
            for half in range(2):
                h = h0 + half
                seg = a_cum[:, h:h + 1] - a_cum_t[h:h + 1, :]
                decay = jnp.exp(jnp.where(causal, seg, -jnp.inf))
                halves.append(jnp.dot((cb_mat * decay).astype(BF16), x_pair, preferred_element_type=F32))
            y_diag = jnp.where(lo, halves[0], halves[1])
            off = slice(2 * hp * SSM_HEAD_DIM, (2 * hp + 2) * SSM_HEAD_DIM)
            y_ref[:, ps] = y_diag + y_off[:, off] + dsk_ref[:, ps] * xs[:, ps]
    y_ref[...] = _gated_group_norm(y_ref[...], z_ref[...], g_ref[...])

    @pl.when(c == pl.num_programs(1) - 1)
    def _():
        hout_ref[...] = ht_ref[...].T


def _ssd_prompt_call(proj, sp):
    b, s_len, _ = proj.shape
    q = SSD_CHUNK
    const = lambda shape: pl.BlockSpec(shape, lambda bi, c: (0,) * len(shape))
    y, h_last = pl.pallas_call(
        _ssd_prompt_kernel,
        out_shape=(jax.ShapeDtypeStruct((b, s_len, D_SSM), F32),
                   jax.ShapeDtypeStruct((b, D_SSM, D_STATE), F32)),
        grid=(b, s_len // q),
        in_specs=[pl.BlockSpec((None, q, D_SSM), lambda bi, c: (bi, c, 3)),
                  pl.BlockSpec((None, q, 2048), lambda bi, c: (bi, c, 2)),
                  const((CONV_W, CONV_DIM)), const((1, CONV_DIM)), const((1, 128)), const((1, 128)),
                  const((1, D_SSM)), const((1, D_SSM))],
        out_specs=(pl.BlockSpec((None, q, D_SSM), lambda bi, c: (bi, c, 0)),
                   pl.BlockSpec((None, D_SSM, D_STATE), lambda bi, c: (bi, 0, 0))),
        scratch_shapes=[pltpu.VMEM((D_STATE, D_SSM), F32), pltpu.VMEM((8 + q, CONV_DIM), F32)],
        compiler_params=pltpu.CompilerParams(
            dimension_semantics=("parallel", "arbitrary"), vmem_limit_bytes=VMEM_LIMIT_BYTES),
        name="ssd_prompt",
    )(proj, proj, sp['conv_w'], sp['conv_b'], sp['dt_bias'], sp['a_log'], sp['d_skip_x'], sp['g_ssm_out'])
    return y.reshape(b * s_len, D_SSM), h_last.reshape(b, H_SSM, SSM_HEAD_DIM, D_STATE)


def _ssd_sample_kernel(z_ref, xd_ref, buf_ref, h0_ref, cw_ref, cb_ref, dtb_ref, alog_ref, dsk_ref, g_ref,
                       y_ref, hout_ref, ext_ref):
    n_seq, n_tok, _ = z_ref.shape
    expand = _head_lane_mask(128).astype(BF16)
    ti = lax.broadcasted_iota(jnp.int32, (8, 8), 0)
    si = lax.broadcasted_iota(jnp.int32, (8, 8), 1)
    causal = (ti >= si) & (si < n_tok)
    real = lax.broadcasted_iota(jnp.int32, (8, 1), 0) < n_tok
    ones = jnp.ones((8, D_STATE), BF16)
    ext_ref[...] = jnp.zeros_like(ext_ref)

    def one_sequence(j, ext_ref):
        ext_ref[8 - (CONV_W - 1):8, :] = buf_ref[j]
        ext_ref[8:8 + n_tok, :] = xd_ref[j][:, 0:CONV_DIM]
        act = _causal_conv(ext_ref, 8, cw_ref, cb_ref)
        xs = act[:, 0:D_SSM]
        bm = act[:, D_SSM:D_SSM + SSM_GROUPS * D_STATE]
        cm = act[:, D_SSM + SSM_GROUPS * D_STATE:CONV_DIM]
        pad = jnp.zeros((8 - n_tok, 128), F32)
        dt = _softplus(jnp.concatenate([xd_ref[j][:, CONV_DIM:CONV_DIM + 128], pad], axis=0) + dtb_ref[...])
        da = jnp.where(real, dt * (-jnp.exp(alog_ref[...])), 0.0)
        a_cum = None
        for piece in _split_bf16(da, 3):
            t = jnp.dot((ti >= si).astype(BF16), piece, preferred_element_type=F32)
            a_cum = t if a_cum is None else a_cum + t
        a_last = a_cum[n_tok - 1:n_tok, :]
        acum_x = _dot_split(a_cum, expand, 3)
        eac_x = jnp.exp(acum_x)
        dt_x = _dot_split(dt, expand, 2)
        xdt = jnp.where(real, xs * dt_x, 0.0)
        xdd = (xdt * jnp.exp(acum_x[n_tok - 1:n_tok, :] - acum_x)).astype(BF16)
        last_x = jnp.where(lax.broadcasted_iota(jnp.int32, (8, 1), 0) == n_tok - 1, eac_x, 0.0)
        keep = _dot_split(last_x, ones, 2, dims=TN_DIMS)
        y_parts, new_state = [], []
        for grp in range(SSM_GROUPS):
            gs = slice(grp * GROUP_W, (grp + 1) * GROUP_W)
            bm_b = bm[:, grp * D_STATE:(grp + 1) * D_STATE].astype(BF16)
            cm_b = cm[:, grp * D_STATE:(grp + 1) * D_STATE].astype(BF16)
            cb_mat = jnp.where(causal, lax.dot_general(cm_b, bm_b, NT_DIMS, preferred_element_type=F32), 0.0)
            h0_g = h0_ref[j, gs, :]
            y_g = lax.dot_general(cm_b, h0_g.astype(BF16), NT_DIMS, preferred_element_type=F32) * eac_x[:, gs]
            for s in range(n_tok):
                coef = cb_mat[:, s:s + 1] * jnp.exp(jnp.where(ti[:, 0:1] >= s, acum_x[:, gs] - acum_x[s:s + 1, gs], -jnp.inf))
                y_g = y_g + coef * xdt[s:s + 1, gs]
            y_parts.append(y_g)
            states = lax.dot_general(xdd[:, gs], bm_b, TN_DIMS, preferred_element_type=F32)
            hout_ref[j, gs, :] = h0_g * keep[gs, :] + states
        y = jnp.concatenate(y_parts, axis=1) + dsk_ref[...] * xs
        zj = jnp.concatenate([z_ref[j], jnp.zeros((8 - n_tok, D_SSM), F32)], axis=0)
        y_ref[j] = _gated_group_norm(y, zj, g_ref[...])[0:n_tok, :]

    def body(jj, carry):
        for slot in range(2):
            one_sequence(2 * jj + slot, ext_ref.at[slot])
        return carry

    lax.fori_loop(0, n_seq // 2, body, 0)


def _ssd_sample_call(proj, conv_buf, h0, sp):
    b, n_tok, _ = proj.shape
    nseq = 8
    const = lambda shape: pl.BlockSpec(shape, lambda i: (0,) * len(shape))
    y, h_last = pl.pallas_call(
        _ssd_sample_kernel,
        out_shape=(jax.ShapeDtypeStruct((b, n_tok, D_SSM), F32),
                   jax.ShapeDtypeStruct((b, D_SSM, D_STATE), F32)),
        grid=(b // nseq,),
        in_specs=[pl.BlockSpec((nseq, n_tok, D_SSM), lambda i: (i, 0, 3)),
                  pl.BlockSpec((nseq, n_tok, 2048), lambda i: (i, 0, 2)),
                  pl.BlockSpec((nseq, CONV_W - 1, CONV_DIM), lambda i: (i, 0, 0)),
                  pl.BlockSpec((nseq, D_SSM, D_STATE), lambda i: (i, 0, 0)),
                  const((CONV_W, CONV_DIM)), const((1, CONV_DIM)), const((1, 128)), const((1, 128)),
                  const((1, D_SSM)), const((1, D_SSM))],
        out_specs=(pl.BlockSpec((nseq, n_tok, D_SSM), lambda i: (i, 0, 0)),
                   pl.BlockSpec((nseq, D_SSM, D_STATE), lambda i: (i, 0, 0))),
        scratch_shapes=[pltpu.VMEM((2, 16, CONV_DIM), F32)],
        compiler_params=pltpu.CompilerParams(
            dimension_semantics=("parallel",), vmem_limit_bytes=VMEM_LIMIT_BYTES),
        name="ssd_sample",
    )(proj, proj, conv_buf, h0.reshape(b, D_SSM, D_STATE), sp['conv_w'], sp['conv_b'], sp['dt_bias'],
      sp['a_log'], sp['d_skip_x'], sp['g_ssm_out'])
    return y.reshape(b * n_tok, D_SSM), h_last.reshape(b, H_SSM, SSM_HEAD_DIM, D_STATE)


def _split_proj(proj, b, L, keep):
    proj = proj.reshape(b, L, D_IN_PAD)
    k = proj[:, L - keep:, D_ATT:2 * D_ATT].reshape(b, keep, H_ATT, HEAD_DIM)
    v = proj[:, L - keep:, 2 * D_ATT:3 * D_ATT].reshape(b, keep, H_ATT, HEAD_DIM)
    o = 3 * D_ATT + D_SSM
    return proj, k, v, proj[:, L - (CONV_W - 1):, o:o + CONV_DIM]


def kernel(x_prompt, x_sample, cache_k, cache_v, state_ssm, state_conv, c_prompt, c_sample,
           rpb_table, w_ada, b_ada, g_ffn1, w_gate1, w_up1, w_down1, g_mix, w_in, conv_w, conv_b,
           dt_bias, a_log, d_skip, g_ssm_out, g_attn_out, w_out, g_ffn2, w_gate2, w_up2, w_down2, g_final):
    bp, sp, _ = x_prompt.shape
    bs, ls, _ = x_sample.shape
    wbp = min(WINDOW_MAX, sp)
    layer = 0
    w_in_pad = jnp.pad(jnp.transpose(w_in[layer]).astype(BF16), ((0, D_IN_PAD - D_IN), (0, 0)))
    p = {'g_ffn1': g_ffn1[layer], 'g_mix': g_mix[layer], 'w_in_pad': w_in_pad,
         'conv_w': conv_w[layer], 'conv_b': conv_b[layer].reshape(1, CONV_DIM),
         'dt_bias': jnp.pad(dt_bias[layer], (0, 128 - H_SSM)).reshape(1, 128),
         'a_log': jnp.pad(a_log[layer], (0, 128 - H_SSM)).reshape(1, 128),
         'd_skip_x': jnp.repeat(d_skip[layer], SSM_HEAD_DIM).reshape(1, D_SSM),
         'g_ssm_out': g_ssm_out[layer].reshape(1, D_SSM), 'g_attn_out': g_attn_out[layer],
         'w_out_att': w_out[layer, :D_ATT].astype(BF16), 'w_out_ssm': w_out[layer, D_ATT:].astype(BF16),
         'g_ffn2': g_ffn2[layer], 'w_gate2': w_gate2[layer].astype(BF16), 'w_up2': w_up2[layer].astype(BF16),
         'w_down2': w_down2[layer].astype(BF16)}

    n_c = bp + bs
    c_rows = -(-n_c // 16) * 16
    c_all = jnp.pad(jnp.concatenate([c_prompt, c_sample], axis=0), ((0, c_rows - n_c), (0, 0)))
    mod = _mod_call(c_all, w_ada[layer], b_ada[layer])
    mods_p = [mod[:bp, i * D_MODEL:(i + 1) * D_MODEL].reshape(bp, 1, D_MODEL) for i in range(N_MOD)]
    mods_s = [jnp.repeat(mod[bp:n_c, i * D_MODEL:(i + 1) * D_MODEL], ls, axis=0) for i in range(N_MOD)]

    lb_c, lb_n = _sample_log_bias(rpb_table, ls, cache_k.shape[2])
    cache_kt = jnp.transpose(cache_k[layer], (0, 2, 3, 1))
    cache_vt = jnp.transpose(cache_v[layer], (0, 2, 3, 1))
    dils = tuple(d for _, d in DILATED_PATTERNS)
    sh1, sc1, g1, sh2, sc2, g2, sh3, sc3, g3 = mods_p
    th1, tc1, t1, th2, tc2, t2, th3, tc3, t3 = mods_s
    ffn2 = (p['g_ffn2'], p['w_gate2'], p['w_up2'], p['w_down2'])
    kw_p = dict(tm=1024, rows_per_mod=sp)
    kw_s = dict(tm=bs * ls, rows_per_mod=ls)

    xs, *w1 = _ffn_call(x_sample.reshape(bs * ls, D_MODEL), p['g_ffn1'], th1, tc1, t1,
                        w_gate1[layer], w_up1[layer], w_down1[layer], g_final,
                        final_norm=False, emit_weights=True, **kw_s)
    ffn1 = (p['g_ffn1'], *w1)
    (proj_s,) = _inproj_call(xs, p['g_mix'], th2, tc2, p['w_in_pad'], dilations=(), **kw_s)
    proj_s, kn, vn, cs = _split_proj(proj_s, bs, ls, ls)
    rider = lambda first, n: (proj_s, cache_kt, cache_vt, lb_c, lb_n, first, n)

    xp, att_a = _ffn_call(x_prompt.reshape(bp * sp, D_MODEL), ffn1[0], sh1, sc1, g1, *ffn1[1:], g_final,
                          final_norm=False, rider=rider(0, bs // 2), **kw_p)
    proj_p, *qkv = _inproj_call(xp, p['g_mix'], sh2, sc2, p['w_in_pad'], dilations=dils, **kw_p)
    proj_p, kp, vp, cp = _split_proj(proj_p, bp, sp, wbp)
    outs, lses = [], []
    for (window, dil), qkv_d in zip(DILATED_PATTERNS, qkv):
        o, lse = _attn_prompt_call(qkv_d, _prompt_bias(rpb_table, window, dil), dil, bp)
        outs.append(o)
        lses.append(lse)
    ssd_p, hp = _ssd_prompt_call(proj_p, p)
    xp = _outproj_call(xp, outs + lses, ssd_p, g2, p['w_out_att'], p['w_out_ssm'], p['g_attn_out'],
                       tm=256, rows_per_mod=sp, dilations=dils)
    yp, att_b = _ffn_call(xp, ffn2[0], sh3, sc3, g3, *ffn2[1:], g_final,
                          final_norm=True, rider=rider(bs // 2, bs - bs // 2), **kw_p)

    att_s = jnp.concatenate([att_a, att_b], axis=0).reshape(bs * ls, D_ATT)
    ssd_s, hs = _ssd_sample_call(proj_s, state_conv[layer], state_ssm[layer], p)
    xs = _outproj_call(xs, [att_s], ssd_s, t2, p['w_out_att'], p['w_out_ssm'], p['g_attn_out'],
                       tm=256, rows_per_mod=ls, dilations=())
    ys = _ffn_call(xs, ffn2[0], th3, tc3, t3, *ffn2[1:], g_final, final_norm=True, **kw_s)
    return (yp.reshape(bp, sp, D_MODEL), ys.reshape(bs, ls, D_MODEL), kp[None], vp[None],
            kn[None], vn[None], hp[None], hs[None], cp[None], cs[None])
```

```python
import functools
import math

import jax
import jax.numpy as jnp
from jax import lax
from jax.experimental import pallas as pl
from jax.experimental.pallas import tpu as pltpu

D_MODEL = 2048
HEAD_DIM = 64
D_ATT = 1024
D_SSM = 1024
H_ATT = 16
SSM_HEAD_DIM = 64
H_SSM = 16
SSM_GROUPS = 2
D_STATE = 128
CONV_W = 4
SSD_CHUNK = 128
CONV_DIM = D_SSM + 2 * SSM_GROUPS * D_STATE
D_FF = 5632
DILATED_PATTERNS = ((128, 1), (512, 4), (2048, 16))
WINDOW_MAX = 2048
ATT_BLOCK = 128
RPB_BUCKETS = 32
RPB_MAX_DIST = 2048
NORM_EPS = 1e-6
N_MOD = 9
PROJ_SIZES = (D_ATT, D_ATT, D_ATT, D_SSM, CONV_DIM, H_SSM)
D_IN = sum(PROJ_SIZES)
D_IN_PAD = 6144

VMEM_LIMIT_BYTES = 56 * 1024 * 1024

BF16 = jnp.bfloat16
F32 = jnp.float32


def _silu(v):
    return v * jax.nn.sigmoid(v)


def _norm_mod(x, g, shift, scale):
    y = x * lax.rsqrt(jnp.mean(x * x, axis=-1, keepdims=True) + NORM_EPS)
    return (y * g) * (1.0 + scale) + shift


def _mod_kernel(c_ref, w_ref, b_ref, o_ref, a_ref):
    @pl.when(pl.program_id(0) == 0)
    def _():
        a_ref[...] = _silu(c_ref[...]).astype(BF16)

    o_ref[...] = jnp.dot(a_ref[...], w_ref[...].astype(BF16),
                         preferred_element_type=F32) + b_ref[...]


def _mod_call(c_all, w_ada, b_ada):
    rows = c_all.shape[0]
    n = w_ada.shape[1]
    tn = 1024
    return pl.pallas_call(
        _mod_kernel,
        out_shape=jax.ShapeDtypeStruct((rows, n), F32),
        grid=(n // tn,),
        in_specs=[
            pl.BlockSpec((rows, D_MODEL), lambda j: (0, 0)),
            pl.BlockSpec((D_MODEL, tn), lambda j: (0, j)),
            pl.BlockSpec((1, tn), lambda j: (0, j)),
        ],
        out_specs=pl.BlockSpec((rows, tn), lambda j: (0, j)),
        scratch_shapes=[pltpu.VMEM((rows, D_MODEL), BF16)],
        compiler_params=pltpu.CompilerParams(
            dimension_semantics=("arbitrary",), vmem_limit_bytes=VMEM_LIMIT_BYTES),
        name="adaln_mod",
    )(c_all, w_ada, b_ada.reshape(1, n))


NT_DIMS = (((1,), (1,)), ((), ()))
RIDER_HEADS = 8
RIDER_GROUP = 8


def _sample_attn_heads(q_ref, kn_ref, vn_ref, kt_ref, vt_ref, lbc_ref, lbn_ref, o_ref):
    n_q, width = q_ref.shape
    pad = jnp.zeros((8 - n_q, width), F32)
    q8 = jnp.concatenate([q_ref[...] * (HEAD_DIM ** -0.5), pad], axis=0).astype(BF16)
    kn8 = jnp.concatenate([kn_ref[...], pad], axis=0).astype(BF16)
    vn8 = jnp.concatenate([vn_ref[...], pad], axis=0).astype(BF16)
    heads = width // HEAD_DIM
    wb = kt_ref.shape[2]
    hs = lambda h: slice(h * HEAD_DIM, (h + 1) * HEAD_DIM)
    group = RIDER_GROUP
    for g0 in range(0, heads, group):
        hh = range(g0, g0 + group)
        t_c = jnp.concatenate([jnp.dot(q8[:, hs(h)], kt_ref[h].astype(BF16), preferred_element_type=F32)
                               for h in hh], axis=0) + lbc_ref[g0:g0 + group].reshape(group * 8, wb)
        t_n = jnp.concatenate([lax.dot_general(q8[:, hs(h)], kn8[:, hs(h)], NT_DIMS,
                                               preferred_element_type=F32)
                               for h in hh], axis=0) + lbn_ref[g0:g0 + group].reshape(group * 8, 8)
        m = jnp.maximum(jnp.max(t_c, axis=-1, keepdims=True), jnp.max(t_n, axis=-1, keepdims=True))
        p_c = jnp.exp(t_c - m)
        p_n = jnp.exp(t_n - m)
        inv_l = 1.0 / (jnp.sum(p_c, axis=-1, keepdims=True) + jnp.sum(p_n, axis=-1, keepdims=True))
        for i, h in enumerate(hh):
            rows = slice(i * 8, (i + 1) * 8)
            acc = lax.dot_general(p_c[rows].astype(BF16), vt_ref[h].astype(BF16), NT_DIMS,
                                  preferred_element_type=F32)
            acc += jnp.dot(p_n[rows].astype(BF16), vn8[:, hs(h)], preferred_element_type=F32)
            o_ref[:, hs(h)] = (acc * inv_l[rows])[0:n_q, :]


def _ffn_kernel(*refs, final_norm, rider, emit_weights):
    x_ref, g_ref, sh_ref, sc_ref, gt_ref, wg_ref, wu_ref, wd_ref, gf_ref = refs[:9]
    if rider:
        rider_in, (o_ref, att_ref, h_ref) = refs[9:16], refs[16:]
    elif emit_weights:
        o_ref, wgb_ref, wub_ref, wdb_ref, h_ref = refs[9:]
    else:
        o_ref, h_ref = refs[9:]
    j = pl.program_id(1)

    @pl.when(j == 0)
    def _():
        h_ref[...] = _norm_mod(x_ref[...], g_ref[...], sh_ref[...], sc_ref[...]).astype(BF16)
        o_ref[...] = jnp.zeros_like(o_ref)

    wg, wu, wd = wg_ref[...], wu_ref[...], wd_ref[...]
    if emit_weights:
        wg, wu, wd = wg.astype(BF16), wu.astype(BF16), wd.astype(BF16)
        wgb_ref[...] = wg
        wub_ref[...] = wu
        wdb_ref[...] = wd
    h = h_ref[...]
    a = jnp.dot(h, wg, preferred_element_type=F32)
    b = jnp.dot(h, wu, preferred_element_type=F32)
    s = (_silu(a) * b).astype(BF16)
    o_ref[...] += jnp.dot(s, wd, preferred_element_type=F32)
    if rider:
        @pl.when(pl.program_id(0) * pl.num_programs(1) + j < rider)
        def _():
            _sample_attn_heads(*rider_in, att_ref)

    @pl.when(j == pl.num_programs(1) - 1)
    def _():
        y = x_ref[...] + 0.5 * gt_ref[...] * o_ref[...]
        if final_norm:
            y = y * lax.rsqrt(jnp.mean(y * y, axis=-1, keepdims=True) + NORM_EPS) * gf_ref[...]
        o_ref[...] = y


def _mod_spec(mod, tm, rows_per_mod):
    if mod.ndim == 3:
        tiles_per_row = rows_per_mod // tm
        return pl.BlockSpec((None, 1, D_MODEL), lambda i, j: (i // tiles_per_row, 0, 0))
    return pl.BlockSpec((tm, D_MODEL), lambda i, j: (i, 0))


def _ffn_call(x, g, sh, sc, gt, wg, wu, wd, g_final, *, tm, rows_per_mod, final_norm, rider=None,
              emit_weights=False):
    m = x.shape[0]
    assert not (rider and emit_weights) and (not emit_weights or m == tm)
    tf = 256 if (rider or emit_weights) else 512
    nj = D_FF // tf
    row = lambda i, j: (i, 0)
    const = lambda i, j: (0, 0)
    in_specs = [
        pl.BlockSpec((tm, D_MODEL), row, pipeline_mode=pl.Buffered(1)),
        pl.BlockSpec((1, D_MODEL), const),
        _mod_spec(sh, tm, rows_per_mod),
        _mod_spec(sc, tm, rows_per_mod),
        _mod_spec(gt, tm, rows_per_mod),
        pl.BlockSpec((D_MODEL, tf), lambda i, j: (0, j)),
        pl.BlockSpec((D_MODEL, tf), lambda i, j: (0, j)),
        pl.BlockSpec((tf, D_MODEL), lambda i, j: (j, 0)),
        pl.BlockSpec((1, D_MODEL), const),
    ]
    args = [x, g.reshape(1, D_MODEL), sh, sc, gt, wg, wu, wd, g_final.reshape(1, D_MODEL)]
    out_shape = jax.ShapeDtypeStruct((m, D_MODEL), F32)
    out_specs = pl.BlockSpec((tm, D_MODEL), row)
    if rider:
        proj_s, cache_kt, cache_vt, lb_c, lb_n, first_seq, n_seq = rider
        n_q = proj_s.shape[1]
        wb = cache_kt.shape[3]
        halves = H_ATT // RIDER_HEADS
        n_units = n_seq * halves
        assert n_units <= (m // tm) * nj
        rw = RIDER_HEADS * HEAD_DIM
        unit = lambda i, j: jnp.minimum(i * nj + j, n_units - 1)
        seq = lambda i, j: first_seq + unit(i, j) // halves
        half = lambda i, j: unit(i, j) % halves
        tiles = D_ATT // rw
        qspec = lambda t: pl.BlockSpec((None, n_q, rw), lambda i, j: (seq(i, j), 0, t * tiles + half(i, j)))
        cspec = pl.BlockSpec((None, RIDER_HEADS, HEAD_DIM, wb), lambda i, j: (seq(i, j), half(i, j), 0, 0))
        in_specs += [qspec(0), qspec(1), qspec(2), cspec, cspec,
                     pl.BlockSpec((RIDER_HEADS, 8, wb), lambda i, j: (half(i, j), 0, 0)),
                     pl.BlockSpec((RIDER_HEADS, 8, 8), lambda i, j: (half(i, j), 0, 0))]
        args += [proj_s, proj_s, proj_s, cache_kt, cache_vt, lb_c, lb_n]
        out_shape = (out_shape, jax.ShapeDtypeStruct((n_seq, n_q, D_ATT), F32))
        out_specs = (pl.BlockSpec((tm, D_MODEL), row, pipeline_mode=pl.Buffered(1)),
                     pl.BlockSpec((None, n_q, rw), lambda i, j: (seq(i, j) - first_seq, 0, half(i, j))))
    if emit_weights:
        out_shape = (out_shape, jax.ShapeDtypeStruct(wg.shape, BF16), jax.ShapeDtypeStruct(wu.shape, BF16),
                     jax.ShapeDtypeStruct(wd.shape, BF16))
        out_specs = (out_specs,) + tuple(in_specs[5:8])
    return pl.pallas_call(
        functools.partial(_ffn_kernel, final_norm=final_norm, rider=n_units if rider else 0,
                          emit_weights=emit_weights),
        out_shape=out_shape,
        grid=(m // tm, nj),
        in_specs=in_specs,
        out_specs=out_specs,
        scratch_shapes=[pltpu.VMEM((tm, D_MODEL), BF16)],
        compiler_params=pltpu.CompilerParams(
            dimension_semantics=("arbitrary", "arbitrary") if rider else ("parallel", "arbitrary"),
            vmem_limit_bytes=VMEM_LIMIT_BYTES),
        name=("ffn_final" if final_norm else "ffn") + ("_rider" if rider else ""),
    )(*args)


QKV_TILES = 3


def _inproj_kernel(x_ref, g_ref, sh_ref, sc_ref, w_ref, o_ref, *rest, dilations):
    qkv_refs, h_ref = rest[:-1], rest[-1]
    j = pl.program_id(1)
    tm = o_ref.shape[0]

    @pl.when(j == 0)
    def _():
        h_ref[...] = _norm_mod(x_ref[...], g_ref[...], sh_ref[...], sc_ref[...]).astype(BF16)

    res = lax.dot_general(h_ref[...], w_ref[...], NT_DIMS, preferred_element_type=F32)
    o_ref[...] = res

    @pl.when(j < QKV_TILES)
    def _():
        res_b = res.astype(BF16)
        part = min(tm, 512)
        out_row = lax.broadcasted_iota(jnp.int32, (part, part), 0)
        in_row = lax.broadcasted_iota(jnp.int32, (part, part), 1)
        for d, ref in zip(dilations, qkv_refs):
            if d == 1:
                ref[...] = res_b
                continue
            n = part // d
            perm = (in_row == (out_row % n) * d + out_row // n).astype(BF16)
            for s in range(tm // part):
                grouped = jnp.dot(perm, res_b[s * part:(s + 1) * part, :], preferred_element_type=F32)
                grouped = grouped.astype(BF16)
                for r in range(d):
                    ref[s * n:(s + 1) * n, r * D_ATT:(r + 1) * D_ATT] = grouped[r * n:(r + 1) * n, :]


def _inproj_call(x, g, sh, sc, w_pad_t, *, tm, rows_per_mod, dilations):
    m = x.shape[0]
    tn = D_ATT
    qkv_j = lambda j: jnp.minimum(j, QKV_TILES - 1)
    return pl.pallas_call(
        functools.partial(_inproj_kernel, dilations=dilations),
        out_shape=(jax.ShapeDtypeStruct((m, D_IN_PAD), F32),)
        + tuple(jax.ShapeDtypeStruct((m // d, QKV_TILES * d * tn), BF16) for d in dilations),
        grid=(m // tm, D_IN_PAD // tn),
        in_specs=[
            pl.BlockSpec((tm, D_MODEL), lambda i, j: (i, 0), pipeline_mode=pl.Buffered(1)),
            pl.BlockSpec((1, D_MODEL), lambda i, j: (0, 0)),
            _mod_spec(sh, tm, rows_per_mod),
            _mod_spec(sc, tm, rows_per_mod),
            pl.BlockSpec((tn, D_MODEL), lambda i, j: (j, 0)),
        ],
        out_specs=(pl.BlockSpec((tm, tn), lambda i, j: (i, j)),)
        + tuple(pl.BlockSpec((tm // d, d * tn), lambda i, j: (i, qkv_j(j))) for d in dilations),
        scratch_shapes=[pltpu.VMEM((tm, D_MODEL), BF16)],
        compiler_params=pltpu.CompilerParams(
            dimension_semantics=("parallel", "arbitrary"), vmem_limit_bytes=VMEM_LIMIT_BYTES),
        name="in_proj",
    )(x, g.reshape(1, D_MODEL), sh, sc, w_pad_t)


def _outproj_kernel(*refs, dilations):
    x_ref = refs[0]
    n_pat = len(dilations)
    if n_pat:
        o_refs = refs[1:1 + n_pat]
        l_refs = refs[1 + n_pat:1 + 2 * n_pat]
        ga_ref, ssd_ref, gt_ref, wa_ref, ws_ref, out_ref, o_scr, l_scr = refs[1 + 2 * n_pat:]
        tm = x_ref.shape[0]
        n_chunks = D_ATT // 128
        outs, lses = [], []
        for p, (d, o_ref, l_ref) in enumerate(zip(dilations, o_refs, l_refs)):
            if d == 1:
                outs.append(o_ref[...].astype(F32))
                lses.append(l_ref[...])
                continue
            for r in range(d):
                l_scr[p, pl.ds(r, tm // d, stride=d), :] = l_ref[:, r * 128:(r + 1) * 128]
                for c in range(n_chunks):
                    lanes = slice(r * D_ATT + c * 128, r * D_ATT + (c + 1) * 128)
                    o_scr[p, c, pl.ds(r, tm // d, stride=d), :] = o_ref[:, lanes].astype(F32)
            outs.append(jnp.concatenate([o_scr[p, c] for c in range(n_chunks)], axis=1))
            lses.append(l_scr[p])
        mx = functools.reduce(jnp.maximum, lses)
        ws = [jnp.exp(v - mx) for v in lses]
        wsum = functools.reduce(lambda a, b: a + b, ws)
        expand = _head_lane_mask(128).astype(BF16)
        att = None
        for w, o in zip(ws, outs):
            wn = w / wsum
            w_hi = wn.astype(BF16)
            w_lo = (wn - w_hi.astype(F32)).astype(BF16)
            wx = (jnp.dot(w_hi, expand, preferred_element_type=F32)
                  + jnp.dot(w_lo, expand, preferred_element_type=F32))
            att = wx * o if att is None else att + wx * o
    else:
        att_ref, ga_ref, ssd_ref, gt_ref, wa_ref, ws_ref, out_ref = refs[1:]
        att = att_ref[...]
    att = att * lax.rsqrt(jnp.mean(att * att, axis=-1, keepdims=True) + NORM_EPS) * ga_ref[...]
    mix = jnp.dot(att.astype(BF16), wa_ref[...], preferred_element_type=F32)
    mix += jnp.dot(ssd_ref[...].astype(BF16), ws_ref[...], preferred_element_type=F32)
    out_ref[...] = x_ref[...] + gt_ref[...] * mix


def _outproj_call(x, att_parts, ssd, gt, w_att, w_ssd, g_attn, *, tm, rows_per_mod, dilations):
    m = x.shape[0]
    n_pat = len(dilations)
    if gt.ndim == 3:
        tiles_per_row = rows_per_mod // tm
        gt_spec = pl.BlockSpec((None, 1, D_MODEL), lambda i: (i // tiles_per_row, 0, 0))
    else:
        gt_spec = pl.BlockSpec((tm, D_MODEL), lambda i: (i, 0))
    row = lambda width, d=1: pl.BlockSpec((tm // d, d * width), lambda i: (i, 0))
    scratch = []
    if n_pat:
        att_specs = ([row(D_ATT, d) for d in dilations] + [row(128, d) for d in dilations]
                     + [pl.BlockSpec((1, D_ATT), lambda i: (0, 0))])
        att_args = list(att_parts) + [g_attn.reshape(1, D_ATT)]
        scratch = [pltpu.VMEM((n_pat, D_ATT // 128, tm, 128), F32), pltpu.VMEM((n_pat, tm, 128), F32)]
    else:
        att_specs = [row(D_ATT), pl.BlockSpec((1, D_ATT), lambda i: (0, 0))]
        att_args = list(att_parts) + [g_attn.reshape(1, D_ATT)]
    weight = lambda k: pl.BlockSpec((k, D_MODEL), lambda i: (0, 0), pipeline_mode=pl.Buffered(1))
    return pl.pallas_call(
        functools.partial(_outproj_kernel, dilations=dilations),
        out_shape=jax.ShapeDtypeStruct((m, D_MODEL), F32),
        grid=(m // tm,),
        in_specs=[row(D_MODEL)] + att_specs + [row(D_SSM), gt_spec, weight(D_ATT), weight(D_SSM)],
        out_specs=row(D_MODEL),
        scratch_shapes=scratch,
        compiler_params=pltpu.CompilerParams(
            dimension_semantics=("parallel",), vmem_limit_bytes=VMEM_LIMIT_BYTES),
        name="out_proj_merge" if n_pat else "out_proj",
    )(x, *att_args, ssd, gt, w_att, w_ssd)


def _rel_bucket(dist):
    max_exact = RPB_BUCKETS // 2
    df = jnp.maximum(dist, 1).astype(jnp.float32)
    large = max_exact + (jnp.log(df / max_exact) / math.log(RPB_MAX_DIST / max_exact)
                         * (RPB_BUCKETS - max_exact)).astype(jnp.int32)
    large = jnp.minimum(large, RPB_BUCKETS - 1)
    return jnp.where(dist < max_exact, dist, large)


def _head_lane_mask(rows=H_ATT):
    r = lax.broadcasted_iota(jnp.int32, (rows, D_ATT), 0)
    c = lax.broadcasted_iota(jnp.int32, (rows, D_ATT), 1)
    return (c // HEAD_DIM == r).astype(F32)


def _attn_prompt_kernel(q_ref, kp_ref, kc_ref, vp_ref, vc_ref, bias_ref, o_ref, lse_ref):
    nb = pl.program_id(2)
    blk = ATT_BLOCK
    col = lax.broadcasted_iota(jnp.int32, (blk, 2 * blk), 1)
    has_prev = (col >= blk) | (nb > 0)
    lane = lax.broadcasted_iota(jnp.int32, (blk, 2 * HEAD_DIM), 1)
    lo = lane < HEAD_DIM
    lse_lane = lax.broadcasted_iota(jnp.int32, (blk, 128), 1)
    lse_tile = jnp.zeros((blk, 128), F32)
    scale = HEAD_DIM ** -0.5
    for hp in range(H_ATT // 2):
        sl = slice(hp * 2 * HEAD_DIM, (hp + 1) * 2 * HEAD_DIM)
        q2 = q_ref[:, sl] * scale
        k2 = jnp.concatenate([kp_ref[:, sl], kc_ref[:, sl]], axis=0)
        v2 = jnp.concatenate([vp_ref[:, sl], vc_ref[:, sl]], axis=0)
        outs = []
        for half in range(2):
            h = 2 * hp + half
            qh = jnp.where(lo if half == 0 else ~lo, q2, jnp.zeros_like(q2))
            s = lax.dot_general(qh, k2, NT_DIMS, preferred_element_type=F32) + bias_ref[h]
            s = jnp.where(has_prev, s, -jnp.inf)
            m = jnp.max(s, axis=-1, keepdims=True)
            p = jnp.exp(s - m)
            l = jnp.sum(p, axis=-1, keepdims=True)
            pv = jnp.dot(p.astype(BF16), v2, preferred_element_type=F32)
            outs.append(pv / l)
            lse_tile = jnp.where(lse_lane == h, m + jnp.log(l), lse_tile)
        o_ref[:, sl] = jnp.where(lo, outs[0], outs[1]).astype(o_ref.dtype)
    lse_ref[...] = lse_tile


def _attn_prompt_cast_kernel(*refs):
    w_ref, wb_ref = refs[6], refs[9]
    wb_ref[...] = w_ref[...].astype(BF16)
    _attn_prompt_kernel(*refs[:6], *refs[7:9])


def _attn_prompt_call(qkv, bias, dil, b, w=None):
    blk = ATT_BLOCK
    n_sub = qkv.shape[0] // b
    nbk = n_sub // blk
    pv = qkv.reshape(b, n_sub, qkv.shape[1])
    cur = lambda t: (lambda bi, r, nb: (bi, nb, t * dil + r))
    prev = lambda t: (lambda bi, r, nb: (bi, jnp.maximum(nb - 1, 0), t * dil + r))
    spec = lambda f: pl.BlockSpec((None, blk, D_ATT), f)
    in_specs = [spec(cur(0)), spec(prev(1)), spec(cur(1)), spec(prev(2)), spec(cur(2)),
                pl.BlockSpec((H_ATT, blk, 2 * blk), lambda bi, r, nb: (0, 0, 0))]
    out_shape = [jax.ShapeDtypeStruct((b, n_sub, dil * D_ATT), BF16),
                 jax.ShapeDtypeStruct((b, n_sub, dil * 128), F32)]
    out_specs = [pl.BlockSpec((None, blk, D_ATT), lambda bi, r, nb: (bi, nb, r)),
                 pl.BlockSpec((None, blk, 128), lambda bi, r, nb: (bi, nb, r))]
    args = [pv, pv, pv, pv, pv, bias]
    if w is not None:
        axis = 0 if w.shape[0] > w.shape[1] else 1
        n_slabs = w.shape[axis] // 128
        assert n_slabs * 128 == w.shape[axis] and n_slabs <= b * dil * nbk
        slab = lambda bi, r, nb: jnp.minimum((bi * dil + r) * nbk + nb, n_slabs - 1)
        if axis == 0:
            w_spec = pl.BlockSpec((128, w.shape[1]), lambda bi, r, nb: (slab(bi, r, nb), 0))
        else:
            w_spec = pl.BlockSpec((w.shape[0], 128), lambda bi, r, nb: (0, slab(bi, r, nb)))
        in_specs.append(w_spec)
        out_shape.append(jax.ShapeDtypeStruct(w.shape, BF16))
        out_specs.append(w_spec)
        args.append(w)
    res = pl.pallas_call(
        _attn_prompt_kernel if w is None else _attn_prompt_cast_kernel,
        out_shape=tuple(out_shape),
        grid=(b, dil, nbk),
        in_specs=in_specs,
        out_specs=tuple(out_specs),
        compiler_params=pltpu.CompilerParams(
            dimension_semantics=("parallel", "parallel", "arbitrary") if w is None else ("arbitrary",) * 3,
            vmem_limit_bytes=VMEM_LIMIT_BYTES),
        name=f"attn_prompt_d{dil}",
    )(*args)
    o, lse = res[0].reshape(b * n_sub, dil * D_ATT), res[1].reshape(b * n_sub, dil * 128)
    return (o, lse) if w is None else (o, lse, res[2])


def _prompt_bias(rpb, window, dil):
    span = window // dil
    blk = ATT_BLOCK
    assert span == blk
    table = rpb[_rel_bucket(jnp.arange(span + 1) * dil)].T.astype(F32)
    fill = jnp.full((H_ATT, blk), -jnp.inf, F32)
    row = jnp.concatenate([fill, table[:, ::-1], fill], axis=1)
    wrapped = jnp.tile(row, (1, blk))[:, :blk * 3 * blk].reshape(H_ATT, blk, 3 * blk)
    return wrapped[:, :, blk:]


def _sample_log_bias(rpb, n_q, wb):
    reach = max(w for w, _ in DILATED_PATTERNS) + 2 * 8
    assert wb + n_q <= reach
    terms = []
    for window, dil in DILATED_PATTERNS:
        span = window // dil
        table = rpb[_rel_bucket(jnp.arange(span + 1) * dil)].T.astype(F32)
        gaps = jnp.full((H_ATT, span + 1, dil - 1), -jnp.inf, F32)
        by_dist = jnp.concatenate([table[:, :, None], gaps], axis=2).reshape(H_ATT, (span + 1) * dil)
        by_dist = by_dist[:, :reach]
        terms.append(jnp.pad(by_dist, ((0, 0), (0, reach - by_dist.shape[1])), constant_values=-jnp.inf))
    lb = jax.nn.logsumexp(jnp.stack(terms), axis=0)
    neg = jnp.full((H_ATT,), -jnp.inf, F32)
    rows_c, rows_n = [], []
    for r in range(8):
        n = r % n_q
        rows_c.append(lb[:, n + 1:n + 1 + wb][:, ::-1])
        rows_n.append(jnp.stack([lb[:, n - m] if (m < n_q and m <= n) else neg for m in range(8)], axis=1))
    return jnp.stack(rows_c, axis=1), jnp.stack(rows_n, axis=1)


TN_DIMS = (((0,), (0,)), ((), ()))
GROUP_W = D_SSM // SSM_GROUPS


def _split_bf16(v, pieces):
    out = []
    for _ in range(pieces):
        hi = v.astype(BF16)
        out.append(hi)
        v = v - hi.astype(F32)
    return out


def _dot_split(v, rhs_bf16, pieces, dims=None):
    acc = None
    for piece in _split_bf16(v, pieces):
        if dims is None:
            t = jnp.dot(piece, rhs_bf16, preferred_element_type=F32)
        else:
            t = lax.dot_general(piece, rhs_bf16, dims, preferred_element_type=F32)
        acc = t if acc is None else acc + t
    return acc


def _softplus(v):
    return jnp.maximum(v, 0.0) + jnp.log1p(jnp.exp(-jnp.abs(v)))


def _causal_conv(ext_ref, rows, cw_ref, cb_ref):
    conv = cb_ref[...]
    for k in range(CONV_W):
        conv = conv + cw_ref[CONV_W - 1 - k:CONV_W - k, :] * ext_ref[8 - k:8 - k + rows, :]
    return _silu(conv)


def _gated_group_norm(y, z, g):
    gated = y * _silu(z)
    parts = []
    for grp in range(SSM_GROUPS):
        seg = gated[:, grp * GROUP_W:(grp + 1) * GROUP_W]
        parts.append(seg * lax.rsqrt(jnp.mean(seg * seg, axis=-1, keepdims=True) + NORM_EPS))
    return jnp.concatenate(parts, axis=1) * g


def _ssd_prompt_kernel(z_ref, xd_ref, cw_ref, cb_ref, dtb_ref, alog_ref, dsk_ref, g_ref,
                       y_ref, hout_ref, ht_ref, ext_ref):
    c = pl.program_id(1)
    q = SSD_CHUNK

    @pl.when(c == 0)
    def _():
        ht_ref[...] = jnp.zeros_like(ht_ref)
        ext_ref[0:8, :] = jnp.zeros((8, CONV_DIM), F32)

    x_raw = xd_ref[:, 0:CONV_DIM]
    ext_ref[8:8 + q, :] = x_raw
    act = _causal_conv(ext_ref, q, cw_ref, cb_ref)
    ext_ref[0:8, :] = x_raw[q - 8:q, :]
    xs = act[:, 0:D_SSM]
    bm = act[:, D_SSM:D_SSM + SSM_GROUPS * D_STATE]
    cm = act[:, D_SSM + SSM_GROUPS * D_STATE:CONV_DIM]

    dt = _softplus(xd_ref[:, CONV_DIM:CONV_DIM + 128] + dtb_ref[...])
    da = dt * (-jnp.exp(alog_ref[...]))
    ti = lax.broadcasted_iota(jnp.int32, (q, q), 0)
    si = lax.broadcasted_iota(jnp.int32, (q, q), 1)
    causal = ti >= si
    a_cum = None
    for piece in _split_bf16(da, 3):
        t = jnp.dot(causal.astype(BF16), piece, preferred_element_type=F32)
        a_cum = t if a_cum is None else a_cum + t
    a_cum_t = a_cum.T
    expand = _head_lane_mask(128).astype(BF16)
    eac = jnp.exp(a_cum)
    dt_x = _dot_split(dt, expand, 2)
    eac_x = _dot_split(eac, expand, 2)
    dend_x = _dot_split(jnp.exp(a_cum[q - 1:q, :] - a_cum), expand, 2)
    xdt = xs * dt_x
    xdd = (xdt * dend_x).astype(BF16)
    xdt_b = xdt.astype(BF16)
    lane = lax.broadcasted_iota(jnp.int32, (q, 2 * SSM_HEAD_DIM), 1)
    lo = lane < SSM_HEAD_DIM
    heads_per_group = H_SSM // SSM_GROUPS
    for grp in range(SSM_GROUPS):
        gs = slice(grp * GROUP_W, (grp + 1) * GROUP_W)
        bm_g = bm[:, grp * D_STATE:(grp + 1) * D_STATE]
        cm_b = cm[:, grp * D_STATE:(grp + 1) * D_STATE].astype(BF16)
        cb_mat = lax.dot_general(cm_b, bm_g.astype(BF16), NT_DIMS, preferred_element_type=F32)
        h_in = ht_ref[:, gs]
        y_off = jnp.dot(cm_b, h_in.astype(BF16), preferred_element_type=F32) * eac_x[:, gs]
        states = jnp.dot(bm_g.T.astype(BF16), xdd[:, gs], preferred_element_type=F32)
        ht_ref[:, gs] = h_in * eac_x[q - 1:q, gs] + states
        for hp in range(heads_per_group // 2):
            h0 = grp * heads_per_group + 2 * hp
            ps = slice(h0 * SSM_HEAD_DIM, (h0 + 2) * SSM_HEAD_DIM)
            x_pair = xdt_b[:, ps]
            halves = []
            for half in range(2):
                h = h0 + half
                seg = a_cum[:, h:h + 1] - a_cum_t[h:h + 1, :]
                decay = jnp.exp(jnp.where(causal, seg, -jnp.inf))
                halves.append(jnp.dot((cb_mat * decay).astype(BF16), x_pair, preferred_element_type=F32))
            y_diag = jnp.where(lo, halves[0], halves[1])
            off = slice(2 * hp * SSM_HEAD_DIM, (2 * hp + 2) * SSM_HEAD_DIM)
            y_ref[:, ps] = y_diag + y_off[:, off] + dsk_ref[:, ps] * xs[:, ps]
    y_ref[...] = _gated_group_norm(y_ref[...], z_ref[...], g_ref[...])

    @pl.when(c == pl.num_programs(1) - 1)
    def _():
        hout_ref[...] = ht_ref[...].T


def _ssd_prompt_call(proj, sp):
    b, s_len, _ = proj.shape
    q = SSD_CHUNK
    const = lambda shape: pl.BlockSpec(shape, lambda bi, c: (0,) * len(shape))
    y, h_last = pl.pallas_call(
        _ssd_prompt_kernel,
        out_shape=(jax.ShapeDtypeStruct((b, s_len, D_SSM), F32),
                   jax.ShapeDtypeStruct((b, D_SSM, D_STATE), F32)),
        grid=(b, s_len // q),
        in_specs=[pl.BlockSpec((None, q, D_SSM), lambda bi, c: (bi, c, 3)),
                  pl.BlockSpec((None, q, 2048), lambda bi, c: (bi, c, 2)),
                  const((CONV_W, CONV_DIM)), const((1, CONV_DIM)), const((1, 128)), const((1, 128)),
                  const((1, D_SSM)), const((1, D_SSM))],
        out_specs=(pl.BlockSpec((None, q, D_SSM), lambda bi, c: (bi, c, 0)),
                   pl.BlockSpec((None, D_SSM, D_STATE), lambda bi, c: (bi, 0, 0))),
        scratch_shapes=[pltpu.VMEM((D_STATE, D_SSM), F32), pltpu.VMEM((8 + q, CONV_DIM), F32)],
        compiler_params=pltpu.CompilerParams(
            dimension_semantics=("parallel", "arbitrary"), vmem_limit_bytes=VMEM_LIMIT_BYTES),
        name="ssd_prompt",
    )(proj, proj, sp['conv_w'], sp['conv_b'], sp['dt_bias'], sp['a_log'], sp['d_skip_x'], sp['g_ssm_out'])
    return y.reshape(b * s_len, D_SSM), h_last.reshape(b, H_SSM, SSM_HEAD_DIM, D_STATE)


def _ssd_sample_kernel(z_ref, xd_ref, buf_ref, h0_ref, cw_ref, cb_ref, dtb_ref, alog_ref, dsk_ref, g_ref,
                       y_ref, hout_ref, ext_ref):
    n_seq, n_tok, _ = z_ref.shape
    expand = _head_lane_mask(128).astype(BF16)
    ti = lax.broadcasted_iota(jnp.int32, (8, 8), 0)
    si = lax.broadcasted_iota(jnp.int32, (8, 8), 1)
    causal = (ti >= si) & (si < n_tok)
    real = lax.broadcasted_iota(jnp.int32, (8, 1), 0) < n_tok
    ones = jnp.ones((8, D_STATE), BF16)
    ext_ref[...] = jnp.zeros_like(ext_ref)

    def one_sequence(j, ext_ref):
        ext_ref[8 - (CONV_W - 1):8, :] = buf_ref[j]
        ext_ref[8:8 + n_tok, :] = xd_ref[j][:, 0:CONV_DIM]
        act = _causal_conv(ext_ref, 8, cw_ref, cb_ref)
        xs = act[:, 0:D_SSM]
        bm = act[:, D_SSM:D_SSM + SSM_GROUPS * D_STATE]
        cm = act[:, D_SSM + SSM_GROUPS * D_STATE:CONV_DIM]
        pad = jnp.zeros((8 - n_tok, 128), F32)
        dt = _softplus(jnp.concatenate([xd_ref[j][:, CONV_DIM:CONV_DIM + 128], pad], axis=0) + dtb_ref[...])
        da = jnp.where(real, dt * (-jnp.exp(alog_ref[...])), 0.0)
        a_cum = None
        for piece in _split_bf16(da, 3):
            t = jnp.dot((ti >= si).astype(BF16), piece, preferred_element_type=F32)
            a_cum = t if a_cum is None else a_cum + t
        a_last = a_cum[n_tok - 1:n_tok, :]
        acum_x = _dot_split(a_cum, expand, 3)
        eac_x = jnp.exp(acum_x)
        dt_x = _dot_split(dt, expand, 2)
        xdt = jnp.where(real, xs * dt_x, 0.0)
        xdd = (xdt * jnp.exp(acum_x[n_tok - 1:n_tok, :] - acum_x)).astype(BF16)
        last_x = jnp.where(lax.broadcasted_iota(jnp.int32, (8, 1), 0) == n_tok - 1, eac_x, 0.0)
        keep = _dot_split(last_x, ones, 2, dims=TN_DIMS)
        y_parts, new_state = [], []
        for grp in range(SSM_GROUPS):
            gs = slice(grp * GROUP_W, (grp + 1) * GROUP_W)
            bm_b = bm[:, grp * D_STATE:(grp + 1) * D_STATE].astype(BF16)
            cm_b = cm[:, grp * D_STATE:(grp + 1) * D_STATE].astype(BF16)
            cb_mat = jnp.where(causal, lax.dot_general(cm_b, bm_b, NT_DIMS, preferred_element_type=F32), 0.0)
            h0_g = h0_ref[j, gs, :]
            y_g = lax.dot_general(cm_b, h0_g.astype(BF16), NT_DIMS, preferred_element_type=F32) * eac_x[:, gs]
            for s in range(n_tok):
                coef = cb_mat[:, s:s + 1] * jnp.exp(jnp.where(ti[:, 0:1] >= s, acum_x[:, gs] - acum_x[s:s + 1, gs], -jnp.inf))
                y_g = y_g + coef * xdt[s:s + 1, gs]
            y_parts.append(y_g)
            states = lax.dot_general(xdd[:, gs], bm_b, TN_DIMS, preferred_element_type=F32)
            hout_ref[j, gs, :] = h0_g * keep[gs, :] + states
        y = jnp.concatenate(y_parts, axis=1) + dsk_ref[...] * xs
        zj = jnp.concatenate([z_ref[j], jnp.zeros((8 - n_tok, D_SSM), F32)], axis=0)
        y_ref[j] = _gated_group_norm(y, zj, g_ref[...])[0:n_tok, :]

    def body(jj, carry):
        for slot in range(2):
            one_sequence(2 * jj + slot, ext_ref.at[slot])
        return carry

    lax.fori_loop(0, n_seq // 2, body, 0)


def _ssd_sample_call(proj, conv_buf, h0, sp):
    b, n_tok, _ = proj.shape
    nseq = 8
    const = lambda shape: pl.BlockSpec(shape, lambda i: (0,) * len(shape))
    y, h_last = pl.pallas_call(
        _ssd_sample_kernel,
        out_shape=(jax.ShapeDtypeStruct((b, n_tok, D_SSM), F32),
                   jax.ShapeDtypeStruct((b, D_SSM, D_STATE), F32)),
        grid=(b // nseq,),
        in_specs=[pl.BlockSpec((nseq, n_tok, D_SSM), lambda i: (i, 0, 3)),
                  pl.BlockSpec((nseq, n_tok, 2048), lambda i: (i, 0, 2)),
                  pl.BlockSpec((nseq, CONV_W - 1, CONV_DIM), lambda i: (i, 0, 0)),
                  pl.BlockSpec((nseq, D_SSM, D_STATE), lambda i: (i, 0, 0)),
                  const((CONV_W, CONV_DIM)), const((1, CONV_DIM)), const((1, 128)), const((1, 128)),
                  const((1, D_SSM)), const((1, D_SSM))],
        out_specs=(pl.BlockSpec((nseq, n_tok, D_SSM), lambda i: (i, 0, 0)),
                   pl.BlockSpec((nseq, D_SSM, D_STATE), lambda i: (i, 0, 0))),
        scratch_shapes=[pltpu.VMEM((2, 16, CONV_DIM), F32)],
        compiler_params=pltpu.CompilerParams(
            dimension_semantics=("parallel",), vmem_limit_bytes=VMEM_LIMIT_BYTES),
        name="ssd_sample",
    )(proj, proj, conv_buf, h0.reshape(b, D_SSM, D_STATE), sp['conv_w'], sp['conv_b'], sp['dt_bias'],
      sp['a_log'], sp['d_skip_x'], sp['g_ssm_out'])
    return y.reshape(b * n_tok, D_SSM), h_last.reshape(b, H_SSM, SSM_HEAD_DIM, D_STATE)


def _split_proj(proj, b, L, keep):
    proj = proj.reshape(b, L, D_IN_PAD)
    k = proj[:, L - keep:, D_ATT:2 * D_ATT].reshape(b, keep, H_ATT, HEAD_DIM)
    v = proj[:, L - keep:, 2 * D_ATT:3 * D_ATT].reshape(b, keep, H_ATT, HEAD_DIM)
    o = 3 * D_ATT + D_SSM
    return proj, k, v, proj[:, L - (CONV_W - 1):, o:o + CONV_DIM]


def kernel(x_prompt, x_sample, cache_k, cache_v, state_ssm, state_conv, c_prompt, c_sample,
           rpb_table, w_ada, b_ada, g_ffn1, w_gate1, w_up1, w_down1, g_mix, w_in, conv_w, conv_b,
           dt_bias, a_log, d_skip, g_ssm_out, g_attn_out, w_out, g_ffn2, w_gate2, w_up2, w_down2, g_final):
    bp, sp, _ = x_prompt.shape
    bs, ls, _ = x_sample.shape
    wbp = min(WINDOW_MAX, sp)
    layer = 0
    w_in_pad = jnp.pad(jnp.transpose(w_in[layer]).astype(BF16), ((0, D_IN_PAD - D_IN), (0, 0)))
    p = {'g_ffn1': g_ffn1[layer], 'g_mix': g_mix[layer], 'w_in_pad': w_in_pad,
         'conv_w': conv_w[layer], 'conv_b': conv_b[layer].reshape(1, CONV_DIM),
         'dt_bias': jnp.pad(dt_bias[layer], (0, 128 - H_SSM)).reshape(1, 128),
         'a_log': jnp.pad(a_log[layer], (0, 128 - H_SSM)).reshape(1, 128),
         'd_skip_x': jnp.repeat(d_skip[layer], SSM_HEAD_DIM).reshape(1, D_SSM),
         'g_ssm_out': g_ssm_out[layer].reshape(1, D_SSM), 'g_attn_out': g_attn_out[layer],
         'w_out_att': w_out[layer, :D_ATT].astype(BF16), 'w_out_ssm': w_out[layer, D_ATT:].astype(BF16),
         'g_ffn2': g_ffn2[layer]}

    n_c = bp + bs
    c_rows = -(-n_c // 16) * 16
    c_all = jnp.pad(jnp.concatenate([c_prompt, c_sample], axis=0), ((0, c_rows - n_c), (0, 0)))
    mod = _mod_call(c_all, w_ada[layer], b_ada[layer])
    mods_p = [mod[:bp, i * D_MODEL:(i + 1) * D_MODEL].reshape(bp, 1, D_MODEL) for i in range(N_MOD)]
    mods_s = [jnp.repeat(mod[bp:n_c, i * D_MODEL:(i + 1) * D_MODEL], ls, axis=0) for i in range(N_MOD)]

    lb_c, lb_n = _sample_log_bias(rpb_table, ls, cache_k.shape[2])
    cache_kt = jnp.transpose(cache_k[layer], (0, 2, 3, 1))
    cache_vt = jnp.transpose(cache_v[layer], (0, 2, 3, 1))
    dils = tuple(d for _, d in DILATED_PATTERNS)
    sh1, sc1, g1, sh2, sc2, g2, sh3, sc3, g3 = mods_p
    th1, tc1, t1, th2, tc2, t2, th3, tc3, t3 = mods_s
    kw_p = dict(tm=1024, rows_per_mod=sp)
    kw_s = dict(tm=bs * ls, rows_per_mod=ls)

    xs, *w1 = _ffn_call(x_sample.reshape(bs * ls, D_MODEL), p['g_ffn1'], th1, tc1, t1,
                        w_gate1[layer], w_up1[layer], w_down1[layer], g_final,
                        final_norm=False, emit_weights=True, **kw_s)
    ffn1 = (p['g_ffn1'], *w1)
    (proj_s,) = _inproj_call(xs, p['g_mix'], th2, tc2, p['w_in_pad'], dilations=(), **kw_s)
    proj_s, kn, vn, cs = _split_proj(proj_s, bs, ls, ls)
    rider = lambda first, n: (proj_s, cache_kt, cache_vt, lb_c, lb_n, first, n)

    xp, att_a = _ffn_call(x_prompt.reshape(bp * sp, D_MODEL), ffn1[0], sh1, sc1, g1, *ffn1[1:], g_final,
                          final_norm=False, rider=rider(0, bs // 2), **kw_p)
    proj_p, *qkv = _inproj_call(xp, p['g_mix'], sh2, sc2, p['w_in_pad'], dilations=dils, **kw_p)
    proj_p, kp, vp, cp = _split_proj(proj_p, bp, sp, wbp)
    outs, lses, w2 = [], [], []
    for (window, dil), qkv_d, w in zip(DILATED_PATTERNS, qkv, (w_gate2[layer], w_up2[layer], w_down2[layer])):
        o, lse, wb = _attn_prompt_call(qkv_d, _prompt_bias(rpb_table, window, dil), dil, bp, w)
        outs.append(o)
        lses.append(lse)
        w2.append(wb)
    ffn2 = (p['g_ffn2'], *w2)
    ssd_p, hp = _ssd_prompt_call(proj_p, p)
    xp = _outproj_call(xp, outs + lses, ssd_p, g2, p['w_out_att'], p['w_out_ssm'], p['g_attn_out'],
                       tm=256, rows_per_mod=sp, dilations=dils)
    yp, att_b = _ffn_call(xp, ffn2[0], sh3, sc3, g3, *ffn2[1:], g_final,
                          final_norm=True, rider=rider(bs // 2, bs - bs // 2), **kw_p)

    att_s = jnp.concatenate([att_a, att_b], axis=0).reshape(bs * ls, D_ATT)
    ssd_s, hs = _ssd_sample_call(proj_s, state_conv[layer], state_ssm[layer], p)
    xs = _outproj_call(xs, [att_s], ssd_s, t2, p['w_out_att'], p['w_out_ssm'], p['g_attn_out'],
                       tm=256, rows_per_mod=ls, dilations=())
    ys = _ffn_call(xs, ffn2[0], th3, tc3, t3, *ffn2[1:], g_final, final_norm=True, **kw_s)
    return (yp.reshape(bp, sp, D_MODEL), ys.reshape(bs, ls, D_MODEL), kp[None], vp[None],
            kn[None], vn[None], hp[None], hs[None], cp[None], cs[None])
```

```python
import functools
import math

import jax
import jax.numpy as jnp
from jax import lax
from jax.experimental import pallas as pl
from jax.experimental.pallas import tpu as pltpu

D_MODEL = 2048
HEAD_DIM = 64
D_ATT = 1024
D_SSM = 1024
H_ATT = 16
SSM_HEAD_DIM = 64
H_SSM = 16
SSM_GROUPS = 2
D_STATE = 128
CONV_W = 4
SSD_CHUNK = 128
CONV_DIM = D_SSM + 2 * SSM_GROUPS * D_STATE
D_FF = 5632
DILATED_PATTERNS = ((128, 1), (512, 4), (2048, 16))
WINDOW_MAX = 2048
ATT_BLOCK = 128
RPB_BUCKETS = 32
RPB_MAX_DIST = 2048
NORM_EPS = 1e-6
N_MOD = 9
PROJ_SIZES = (D_ATT, D_ATT, D_ATT, D_SSM, CONV_DIM, H_SSM)
D_IN = sum(PROJ_SIZES)
D_IN_PAD = 6144

VMEM_LIMIT_BYTES = 56 * 1024 * 1024

BF16 = jnp.bfloat16
F32 = jnp.float32


def _silu(v):
    return v * jax.nn.sigmoid(v)


def _norm_mod(x, g, shift, scale):
    y = x * lax.rsqrt(jnp.mean(x * x, axis=-1, keepdims=True) + NORM_EPS)
    return (y * g) * (1.0 + scale) + shift


def _mod_kernel(c_ref, w_ref, b_ref, o_ref, a_ref):
    @pl.when(pl.program_id(0) == 0)
    def _():
        a_ref[...] = _silu(c_ref[...]).astype(BF16)

    o_ref[...] = jnp.dot(a_ref[...], w_ref[...].astype(BF16),
                         preferred_element_type=F32) + b_ref[...]


def _mod_call(c_all, w_ada, b_ada):
    rows = c_all.shape[0]
    n = w_ada.shape[1]
    tn = 1024
    return pl.pallas_call(
        _mod_kernel,
        out_shape=jax.ShapeDtypeStruct((rows, n), F32),
        grid=(n // tn,),
        in_specs=[
            pl.BlockSpec((rows, D_MODEL), lambda j: (0, 0)),
            pl.BlockSpec((D_MODEL, tn), lambda j: (0, j)),
            pl.BlockSpec((1, tn), lambda j: (0, j)),
        ],
        out_specs=pl.BlockSpec((rows, tn), lambda j: (0, j)),
        scratch_shapes=[pltpu.VMEM((rows, D_MODEL), BF16)],
        compiler_params=pltpu.CompilerParams(
            dimension_semantics=("arbitrary",), vmem_limit_bytes=VMEM_LIMIT_BYTES),
        name="adaln_mod",
    )(c_all, w_ada, b_ada.reshape(1, n))


NT_DIMS = (((1,), (1,)), ((), ()))
RIDER_HEADS = 8
RIDER_GROUP = 8


def _sample_attn_heads(q_ref, kn_ref, vn_ref, kt_ref, vt_ref, lbc_ref, lbn_ref, o_ref):
    n_q, width = q_ref.shape
    pad = jnp.zeros((8 - n_q, width), F32)
    q8 = jnp.concatenate([q_ref[...] * (HEAD_DIM ** -0.5), pad], axis=0).astype(BF16)
    kn8 = jnp.concatenate([kn_ref[...], pad], axis=0).astype(BF16)
    vn8 = jnp.concatenate([vn_ref[...], pad], axis=0).astype(BF16)
    heads = width // HEAD_DIM
    wb = kt_ref.shape[2]
    hs = lambda h: slice(h * HEAD_DIM, (h + 1) * HEAD_DIM)
    group = RIDER_GROUP
    for g0 in range(0, heads, group):
        hh = range(g0, g0 + group)
        t_c = jnp.concatenate([jnp.dot(q8[:, hs(h)], kt_ref[h].astype(BF16), preferred_element_type=F32)
                               for h in hh], axis=0) + lbc_ref[g0:g0 + group].reshape(group * 8, wb)
        t_n = jnp.concatenate([lax.dot_general(q8[:, hs(h)], kn8[:, hs(h)], NT_DIMS,
                                               preferred_element_type=F32)
                               for h in hh], axis=0) + lbn_ref[g0:g0 + group].reshape(group * 8, 8)
        m = jnp.maximum(jnp.max(t_c, axis=-1, keepdims=True), jnp.max(t_n, axis=-1, keepdims=True))
        p_c = jnp.exp(t_c - m)
        p_n = jnp.exp(t_n - m)
        inv_l = 1.0 / (jnp.sum(p_c, axis=-1, keepdims=True) + jnp.sum(p_n, axis=-1, keepdims=True))
        for i, h in enumerate(hh):
            rows = slice(i * 8, (i + 1) * 8)
            acc = lax.dot_general(p_c[rows].astype(BF16), vt_ref[h].astype(BF16), NT_DIMS,
                                  preferred_element_type=F32)
            acc += jnp.dot(p_n[rows].astype(BF16), vn8[:, hs(h)], preferred_element_type=F32)
            o_ref[:, hs(h)] = (acc * inv_l[rows])[0:n_q, :]


def _ffn_kernel(*refs, final_norm, rider, emit_weights):
    x_ref, g_ref, sh_ref, sc_ref, gt_ref, wg_ref, wu_ref, wd_ref, gf_ref = refs[:9]
    if rider:
        rider_in, (o_ref, att_ref, h_ref) = refs[9:16], refs[16:]
    elif emit_weights:
        o_ref, wgb_ref, wub_ref, wdb_ref, h_ref = refs[9:]
    else:
        o_ref, h_ref = refs[9:]
    j = pl.program_id(1)

    @pl.when(j == 0)
    def _():
        h_ref[...] = _norm_mod(x_ref[...], g_ref[...], sh_ref[...], sc_ref[...]).astype(BF16)
        o_ref[...] = jnp.zeros_like(o_ref)

    wg, wu, wd = wg_ref[...], wu_ref[...], wd_ref[...]
    if emit_weights:
        wg, wu, wd = wg.astype(BF16), wu.astype(BF16), wd.astype(BF16)
        wgb_ref[...] = wg
        wub_ref[...] = wu
        wdb_ref[...] = wd
    h = h_ref[...]
    a = jnp.dot(h, wg, preferred_element_type=F32)
    b = jnp.dot(h, wu, preferred_element_type=F32)
    s = (_silu(a) * b).astype(BF16)
    o_ref[...] += jnp.dot(s, wd, preferred_element_type=F32)
    if rider:
        @pl.when(pl.program_id(0) * pl.num_programs(1) + j < rider)
        def _():
            _sample_attn_heads(*rider_in, att_ref)

    @pl.when(j == pl.num_programs(1) - 1)
    def _():
        y = x_ref[...] + 0.5 * gt_ref[...] * o_ref[...]
        if final_norm:
            y = y * lax.rsqrt(jnp.mean(y * y, axis=-1, keepdims=True) + NORM_EPS) * gf_ref[...]
        o_ref[...] = y


def _mod_spec(mod, tm, rows_per_mod):
    if mod.ndim == 3:
        tiles_per_row = rows_per_mod // tm
        return pl.BlockSpec((None, 1, D_MODEL), lambda i, j: (i // tiles_per_row, 0, 0))
    return pl.BlockSpec((tm, D_MODEL), lambda i, j: (i, 0))


def _ffn_call(x, g, sh, sc, gt, wg, wu, wd, g_final, *, tm, rows_per_mod, final_norm, rider=None,
              emit_weights=False):
    m = x.shape[0]
    assert not (rider and emit_weights) and (not emit_weights or m == tm)
    tf = 256 if (rider or emit_weights) else 512
    nj = D_FF // tf
    row = lambda i, j: (i, 0)
    const = lambda i, j: (0, 0)
    in_specs = [
        pl.BlockSpec((tm, D_MODEL), row, pipeline_mode=pl.Buffered(1)),
        pl.BlockSpec((1, D_MODEL), const),
        _mod_spec(sh, tm, rows_per_mod),
        _mod_spec(sc, tm, rows_per_mod),
        _mod_spec(gt, tm, rows_per_mod),
        pl.BlockSpec((D_MODEL, tf), lambda i, j: (0, j)),
        pl.BlockSpec((D_MODEL, tf), lambda i, j: (0, j)),
        pl.BlockSpec((tf, D_MODEL), lambda i, j: (j, 0)),
        pl.BlockSpec((1, D_MODEL), const),
    ]
    args = [x, g.reshape(1, D_MODEL), sh, sc, gt, wg, wu, wd, g_final.reshape(1, D_MODEL)]
    out_shape = jax.ShapeDtypeStruct((m, D_MODEL), F32)
    out_specs = pl.BlockSpec((tm, D_MODEL), row)
    if rider:
        proj_s, cache_kt, cache_vt, lb_c, lb_n, first_seq, n_seq = rider
        n_q = proj_s.shape[1]
        wb = cache_kt.shape[3]
        halves = H_ATT // RIDER_HEADS
        n_units = n_seq * halves
        assert n_units <= (m // tm) * nj
        rw = RIDER_HEADS * HEAD_DIM
        unit = lambda i, j: jnp.minimum(i * nj + j, n_units - 1)
        seq = lambda i, j: first_seq + unit(i, j) // halves
        half = lambda i, j: unit(i, j) % halves
        tiles = D_ATT // rw
        qspec = lambda t: pl.BlockSpec((None, n_q, rw), lambda i, j: (seq(i, j), 0, t * tiles + half(i, j)))
        cspec = pl.BlockSpec((None, RIDER_HEADS, HEAD_DIM, wb), lambda i, j: (seq(i, j), half(i, j), 0, 0))
        in_specs += [qspec(0), qspec(1), qspec(2), cspec, cspec,
                     pl.BlockSpec((RIDER_HEADS, 8, wb), lambda i, j: (half(i, j), 0, 0)),
                     pl.BlockSpec((RIDER_HEADS, 8, 8), lambda i, j: (half(i, j), 0, 0))]
        args += [proj_s, proj_s, proj_s, cache_kt, cache_vt, lb_c, lb_n]
        out_shape = (out_shape, jax.ShapeDtypeStruct((n_seq, n_q, D_ATT), F32))
        out_specs = (pl.BlockSpec((tm, D_MODEL), row, pipeline_mode=pl.Buffered(1)),
                     pl.BlockSpec((None, n_q, rw), lambda i, j: (seq(i, j) - first_seq, 0, half(i, j))))
    if emit_weights:
        out_shape = (out_shape, jax.ShapeDtypeStruct(wg.shape, BF16), jax.ShapeDtypeStruct(wu.shape, BF16),
                     jax.ShapeDtypeStruct(wd.shape, BF16))
        out_specs = (out_specs,) + tuple(in_specs[5:8])
    return pl.pallas_call(
        functools.partial(_ffn_kernel, final_norm=final_norm, rider=n_units if rider else 0,
                          emit_weights=emit_weights),
        out_shape=out_shape,
        grid=(m // tm, nj),
        in_specs=in_specs,
        out_specs=out_specs,
        scratch_shapes=[pltpu.VMEM((tm, D_MODEL), BF16)],
        compiler_params=pltpu.CompilerParams(
            dimension_semantics=("arbitrary", "arbitrary") if rider else ("parallel", "arbitrary"),
            vmem_limit_bytes=VMEM_LIMIT_BYTES),
        name=("ffn_final" if final_norm else "ffn") + ("_rider" if rider else ""),
    )(*args)


QKV_TILES = 3


def _inproj_kernel(x_ref, g_ref, sh_ref, sc_ref, w_ref, o_ref, *rest, dilations):
    qkv_refs, h_ref = rest[:-1], rest[-1]
    j = pl.program_id(1)
    tm = o_ref.shape[0]

    @pl.when(j == 0)
    def _():
        h_ref[...] = _norm_mod(x_ref[...], g_ref[...], sh_ref[...], sc_ref[...]).astype(BF16)

    res = lax.dot_general(h_ref[...], w_ref[...], NT_DIMS, preferred_element_type=F32)
    o_ref[...] = res

    @pl.when(j < QKV_TILES)
    def _():
        res_b = res.astype(BF16)
        part = min(tm, 256)
        out_row = lax.broadcasted_iota(jnp.int32, (part, part), 0)
        in_row = lax.broadcasted_iota(jnp.int32, (part, part), 1)
        for d, ref in zip(dilations, qkv_refs):
            if d == 1:
                ref[...] = res_b
                continue
            n = part // d
            perm = (in_row == (out_row % n) * d + out_row // n).astype(BF16)
            for s in range(tm // part):
                grouped = jnp.dot(perm, res_b[s * part:(s + 1) * part, :], preferred_element_type=F32)
                grouped = grouped.astype(BF16)
                for r in range(d):
                    ref[s * n:(s + 1) * n, r * D_ATT:(r + 1) * D_ATT] = grouped[r * n:(r + 1) * n, :]


def _inproj_call(x, g, sh, sc, w_pad_t, *, tm, rows_per_mod, dilations):
    m = x.shape[0]
    tn = D_ATT
    qkv_j = lambda j: jnp.minimum(j, QKV_TILES - 1)
    return pl.pallas_call(
        functools.partial(_inproj_kernel, dilations=dilations),
        out_shape=(jax.ShapeDtypeStruct((m, D_IN_PAD), F32),)
        + tuple(jax.ShapeDtypeStruct((m // d, QKV_TILES * d * tn), BF16) for d in dilations),
        grid=(m // tm, D_IN_PAD // tn),
        in_specs=[
            pl.BlockSpec((tm, D_MODEL), lambda i, j: (i, 0), pipeline_mode=pl.Buffered(1)),
            pl.BlockSpec((1, D_MODEL), lambda i, j: (0, 0)),
            _mod_spec(sh, tm, rows_per_mod),
            _mod_spec(sc, tm, rows_per_mod),
            pl.BlockSpec((tn, D_MODEL), lambda i, j: (j, 0)),
        ],
        out_specs=(pl.BlockSpec((tm, tn), lambda i, j: (i, j)),)
        + tuple(pl.BlockSpec((tm // d, d * tn), lambda i, j: (i, qkv_j(j))) for d in dilations),
        scratch_shapes=[pltpu.VMEM((tm, D_MODEL), BF16)],
        compiler_params=pltpu.CompilerParams(
            dimension_semantics=("parallel", "arbitrary"), vmem_limit_bytes=VMEM_LIMIT_BYTES),
        name="in_proj",
    )(x, g.reshape(1, D_MODEL), sh, sc, w_pad_t)


def _outproj_kernel(*refs, dilations):
    x_ref = refs[0]
    n_pat = len(dilations)
    if n_pat:
        o_refs = refs[1:1 + n_pat]
        l_refs = refs[1 + n_pat:1 + 2 * n_pat]
        ga_ref, ssd_ref, gt_ref, wa_ref, ws_ref, out_ref, o_scr, l_scr = refs[1 + 2 * n_pat:]
        tm = x_ref.shape[0]
        n_chunks = D_ATT // 128
        outs, lses = [], []
        for p, (d, o_ref, l_ref) in enumerate(zip(dilations, o_refs, l_refs)):
            if d == 1:
                outs.append(o_ref[...].astype(F32))
                lses.append(l_ref[...])
                continue
            for r in range(d):
                l_scr[p, pl.ds(r, tm // d, stride=d), :] = l_ref[:, r * 128:(r + 1) * 128]
                for c in range(n_chunks):
                    lanes = slice(r * D_ATT + c * 128, r * D_ATT + (c + 1) * 128)
                    o_scr[p, c, pl.ds(r, tm // d, stride=d), :] = o_ref[:, lanes].astype(F32)
            outs.append(jnp.concatenate([o_scr[p, c] for c in range(n_chunks)], axis=1))
            lses.append(l_scr[p])
        mx = functools.reduce(jnp.maximum, lses)
        ws = [jnp.exp(v - mx) for v in lses]
        wsum = functools.reduce(lambda a, b: a + b, ws)
        expand = _head_lane_mask(128).astype(BF16)
        att = None
        for w, o in zip(ws, outs):
            wn = w / wsum
            w_hi = wn.astype(BF16)
            w_lo = (wn - w_hi.astype(F32)).astype(BF16)
            wx = (jnp.dot(w_hi, expand, preferred_element_type=F32)
                  + jnp.dot(w_lo, expand, preferred_element_type=F32))
            att = wx * o if att is None else att + wx * o
    else:
        att_ref, ga_ref, ssd_ref, gt_ref, wa_ref, ws_ref, out_ref = refs[1:]
        att = att_ref[...]
    att = att * lax.rsqrt(jnp.mean(att * att, axis=-1, keepdims=True) + NORM_EPS) * ga_ref[...]
    mix = jnp.dot(att.astype(BF16), wa_ref[...], preferred_element_type=F32)
    mix += jnp.dot(ssd_ref[...].astype(BF16), ws_ref[...], preferred_element_type=F32)
    out_ref[...] = x_ref[...] + gt_ref[...] * mix


def _outproj_call(x, att_parts, ssd, gt, w_att, w_ssd, g_attn, *, tm, rows_per_mod, dilations):
    m = x.shape[0]
    n_pat = len(dilations)
    if gt.ndim == 3:
        tiles_per_row = rows_per_mod // tm
        gt_spec = pl.BlockSpec((None, 1, D_MODEL), lambda i: (i // tiles_per_row, 0, 0))
    else:
        gt_spec = pl.BlockSpec((tm, D_MODEL), lambda i: (i, 0))
    row = lambda width, d=1: pl.BlockSpec((tm // d, d * width), lambda i: (i, 0))
    scratch = []
    if n_pat:
        att_specs = ([row(D_ATT, d) for d in dilations] + [row(128, d) for d in dilations]
                     + [pl.BlockSpec((1, D_ATT), lambda i: (0, 0))])
        att_args = list(att_parts) + [g_attn.reshape(1, D_ATT)]
        scratch = [pltpu.VMEM((n_pat, D_ATT // 128, tm, 128), F32), pltpu.VMEM((n_pat, tm, 128), F32)]
    else:
        att_specs = [row(D_ATT), pl.BlockSpec((1, D_ATT), lambda i: (0, 0))]
        att_args = list(att_parts) + [g_attn.reshape(1, D_ATT)]
    weight = lambda k: pl.BlockSpec((k, D_MODEL), lambda i: (0, 0), pipeline_mode=pl.Buffered(1))
    return pl.pallas_call(
        functools.partial(_outproj_kernel, dilations=dilations),
        out_shape=jax.ShapeDtypeStruct((m, D_MODEL), F32),
        grid=(m // tm,),
        in_specs=[row(D_MODEL)] + att_specs + [row(D_SSM), gt_spec, weight(D_ATT), weight(D_SSM)],
        out_specs=row(D_MODEL),
        scratch_shapes=scratch,
        compiler_params=pltpu.CompilerParams(
            dimension_semantics=("parallel",), vmem_limit_bytes=VMEM_LIMIT_BYTES),
        name="out_proj_merge" if n_pat else "out_proj",
    )(x, *att_args, ssd, gt, w_att, w_ssd)


def _rel_bucket(dist):
    max_exact = RPB_BUCKETS // 2
    df = jnp.maximum(dist, 1).astype(jnp.float32)
    large = max_exact + (jnp.log(df / max_exact) / math.log(RPB_MAX_DIST / max_exact)
                         * (RPB_BUCKETS - max_exact)).astype(jnp.int32)
    large = jnp.minimum(large, RPB_BUCKETS - 1)
    return jnp.where(dist < max_exact, dist, large)


def _head_lane_mask(rows=H_ATT):
    r = lax.broadcasted_iota(jnp.int32, (rows, D_ATT), 0)
    c = lax.broadcasted_iota(jnp.int32, (rows, D_ATT), 1)
    return (c // HEAD_DIM == r).astype(F32)


def _attn_prompt_kernel(q_ref, kp_ref, kc_ref, vp_ref, vc_ref, bias_ref, o_ref, lse_ref):
    nb = pl.program_id(2)
    blk = ATT_BLOCK
    col = lax.broadcasted_iota(jnp.int32, (blk, 2 * blk), 1)
    has_prev = (col >= blk) | (nb > 0)
    lane = lax.broadcasted_iota(jnp.int32, (blk, 2 * HEAD_DIM), 1)
    lo = lane < HEAD_DIM
    lse_lane = lax.broadcasted_iota(jnp.int32, (blk, 128), 1)
    lse_tile = jnp.zeros((blk, 128), F32)
    scale = HEAD_DIM ** -0.5
    for hp in range(H_ATT // 2):
        sl = slice(hp * 2 * HEAD_DIM, (hp + 1) * 2 * HEAD_DIM)
        q2 = q_ref[:, sl] * scale
        k2 = jnp.concatenate([kp_ref[:, sl], kc_ref[:, sl]], axis=0)
        v2 = jnp.concatenate([vp_ref[:, sl], vc_ref[:, sl]], axis=0)
        outs = []
        for half in range(2):
            h = 2 * hp + half
            qh = jnp.where(lo if half == 0 else ~lo, q2, jnp.zeros_like(q2))
            s = lax.dot_general(qh, k2, NT_DIMS, preferred_element_type=F32) + bias_ref[h]
            s = jnp.where(has_prev, s, -jnp.inf)
            m = jnp.max(s, axis=-1, keepdims=True)
            p = jnp.exp(s - m)
            l = jnp.sum(p, axis=-1, keepdims=True)
            pv = jnp.dot(p.astype(BF16), v2, preferred_element_type=F32)
            outs.append(pv / l)
            lse_tile = jnp.where(lse_lane == h, m + jnp.log(l), lse_tile)
        o_ref[:, sl] = jnp.where(lo, outs[0], outs[1]).astype(o_ref.dtype)
    lse_ref[...] = lse_tile


def _attn_prompt_cast_kernel(*refs):
    w_ref, wb_ref = refs[6], refs[9]
    wb_ref[...] = w_ref[...].astype(BF16)
    _attn_prompt_kernel(*refs[:6], *refs[7:9])


def _attn_prompt_call(qkv, bias, dil, b, w=None):
    blk = ATT_BLOCK
    n_sub = qkv.shape[0] // b
    nbk = n_sub // blk
    pv = qkv.reshape(b, n_sub, qkv.shape[1])
    cur = lambda t: (lambda bi, r, nb: (bi, nb, t * dil + r))
    prev = lambda t: (lambda bi, r, nb: (bi, jnp.maximum(nb - 1, 0), t * dil + r))
    spec = lambda f: pl.BlockSpec((None, blk, D_ATT), f)
    in_specs = [spec(cur(0)), spec(prev(1)), spec(cur(1)), spec(prev(2)), spec(cur(2)),
                pl.BlockSpec((H_ATT, blk, 2 * blk), lambda bi, r, nb: (0, 0, 0))]
    out_shape = [jax.ShapeDtypeStruct((b, n_sub, dil * D_ATT), BF16),
                 jax.ShapeDtypeStruct((b, n_sub, dil * 128), F32)]
    out_specs = [pl.BlockSpec((None, blk, D_ATT), lambda bi, r, nb: (bi, nb, r)),
                 pl.BlockSpec((None, blk, 128), lambda bi, r, nb: (bi, nb, r))]
    args = [pv, pv, pv, pv, pv, bias]
    if w is not None:
        axis = 0 if w.shape[0] > w.shape[1] else 1
        n_slabs = w.shape[axis] // 128
        assert n_slabs * 128 == w.shape[axis] and n_slabs <= b * dil * nbk
        slab = lambda bi, r, nb: jnp.minimum((bi * dil + r) * nbk + nb, n_slabs - 1)
        if axis == 0:
            w_spec = pl.BlockSpec((128, w.shape[1]), lambda bi, r, nb: (slab(bi, r, nb), 0))
        else:
            w_spec = pl.BlockSpec((w.shape[0], 128), lambda bi, r, nb: (0, slab(bi, r, nb)))
        in_specs.append(w_spec)
        out_shape.append(jax.ShapeDtypeStruct(w.shape, BF16))
        out_specs.append(w_spec)
        args.append(w)
    res = pl.pallas_call(
        _attn_prompt_kernel if w is None else _attn_prompt_cast_kernel,
        out_shape=tuple(out_shape),
        grid=(b, dil, nbk),
        in_specs=in_specs,
        out_specs=tuple(out_specs),
        compiler_params=pltpu.CompilerParams(
            dimension_semantics=("parallel", "parallel", "arbitrary") if w is None else ("arbitrary",) * 3,
            vmem_limit_bytes=VMEM_LIMIT_BYTES),
        name=f"attn_prompt_d{dil}",
    )(*args)
    o, lse = res[0].reshape(b * n_sub, dil * D_ATT), res[1].reshape(b * n_sub, dil * 128)
    return (o, lse) if w is None else (o, lse, res[2])


def _prompt_bias(rpb, window, dil):
    span = window // dil
    blk = ATT_BLOCK
    assert span == blk
    table = rpb[_rel_bucket(jnp.arange(span + 1) * dil)].T.astype(F32)
    fill = jnp.full((H_ATT, blk), -jnp.inf, F32)
    row = jnp.concatenate([fill, table[:, ::-1], fill], axis=1)
    wrapped = jnp.tile(row, (1, blk))[:, :blk * 3 * blk].reshape(H_ATT, blk, 3 * blk)
    return wrapped[:, :, blk:]


def _sample_log_bias(rpb, n_q, wb):
    reach = max(w for w, _ in DILATED_PATTERNS) + 2 * 8
    assert wb + n_q <= reach
    terms = []
    for window, dil in DILATED_PATTERNS:
        span = window // dil
        table = rpb[_rel_bucket(jnp.arange(span + 1) * dil)].T.astype(F32)
        gaps = jnp.full((H_ATT, span + 1, dil - 1), -jnp.inf, F32)
        by_dist = jnp.concatenate([table[:, :, None], gaps], axis=2).reshape(H_ATT, (span + 1) * dil)
        by_dist = by_dist[:, :reach]
        terms.append(jnp.pad(by_dist, ((0, 0), (0, reach - by_dist.shape[1])), constant_values=-jnp.inf))
    lb = jax.nn.logsumexp(jnp.stack(terms), axis=0)
    neg = jnp.full((H_ATT,), -jnp.inf, F32)
    rows_c, rows_n = [], []
    for r in range(8):
        n = r % n_q
        rows_c.append(lb[:, n + 1:n + 1 + wb][:, ::-1])
        rows_n.append(jnp.stack([lb[:, n - m] if (m < n_q and m <= n) else neg for m in range(8)], axis=1))
    return jnp.stack(rows_c, axis=1), jnp.stack(rows_n, axis=1)


TN_DIMS = (((0,), (0,)), ((), ()))
GROUP_W = D_SSM // SSM_GROUPS


def _split_bf16(v, pieces):
    out = []
    for _ in range(pieces):
        hi = v.astype(BF16)
        out.append(hi)
        v = v - hi.astype(F32)
    return out


def _dot_split(v, rhs_bf16, pieces, dims=None):
    acc = None
    for piece in _split_bf16(v, pieces):
        if dims is None:
            t = jnp.dot(piece, rhs_bf16, preferred_element_type=F32)
        else:
            t = lax.dot_general(piece, rhs_bf16, dims, preferred_element_type=F32)
        acc = t if acc is None else acc + t
    return acc


def _softplus(v):
    return jnp.maximum(v, 0.0) + jnp.log1p(jnp.exp(-jnp.abs(v)))


def _causal_conv(ext_ref, rows, cw_ref, cb_ref):
    conv = cb_ref[...]
    for k in range(CONV_W):
        conv = conv + cw_ref[CONV_W - 1 - k:CONV_W - k, :] * ext_ref[8 - k:8 - k + rows, :]
    return _silu(conv)


def _gated_group_norm(y, z, g):
    gated = y * _silu(z)
    parts = []
    for grp in range(SSM_GROUPS):
        seg = gated[:, grp * GROUP_W:(grp + 1) * GROUP_W]
        parts.append(seg * lax.rsqrt(jnp.mean(seg * seg, axis=-1, keepdims=True) + NORM_EPS))
    return jnp.concatenate(parts, axis=1) * g


def _ssd_prompt_kernel(z_ref, xd_ref, cw_ref, cb_ref, dtb_ref, alog_ref, dsk_ref, g_ref,
                       y_ref, hout_ref, ht_ref, ext_ref):
    c = pl.program_id(1)
    q = SSD_CHUNK

    @pl.when(c == 0)
    def _():
        ht_ref[...] = jnp.zeros_like(ht_ref)
        ext_ref[0:8, :] = jnp.zeros((8, CONV_DIM), F32)

    x_raw = xd_ref[:, 0:CONV_DIM]
    ext_ref[8:8 + q, :] = x_raw
    act = _causal_conv(ext_ref, q, cw_ref, cb_ref)
    ext_ref[0:8, :] = x_raw[q - 8:q, :]
    xs = act[:, 0:D_SSM]
    bm = act[:, D_SSM:D_SSM + SSM_GROUPS * D_STATE]
    cm = act[:, D_SSM + SSM_GROUPS * D_STATE:CONV_DIM]

    dt = _softplus(xd_ref[:, CONV_DIM:CONV_DIM + 128] + dtb_ref[...])
    da = dt * (-jnp.exp(alog_ref[...]))
    ti = lax.broadcasted_iota(jnp.int32, (q, q), 0)
    si = lax.broadcasted_iota(jnp.int32, (q, q), 1)
    causal = ti >= si
    a_cum = None
    for piece in _split_bf16(da, 3):
        t = jnp.dot(causal.astype(BF16), piece, preferred_element_type=F32)
        a_cum = t if a_cum is None else a_cum + t
    a_cum_t = a_cum.T
    expand = _head_lane_mask(128).astype(BF16)
    eac = jnp.exp(a_cum)
    dt_x = _dot_split(dt, expand, 2)
    eac_x = _dot_split(eac, expand, 2)
    dend_x = _dot_split(jnp.exp(a_cum[q - 1:q, :] - a_cum), expand, 2)
    xdt = xs * dt_x
    xdd = (xdt * dend_x).astype(BF16)
    xdt_b = xdt.astype(BF16)
    lane = lax.broadcasted_iota(jnp.int32, (q, 2 * SSM_HEAD_DIM), 1)
    lo = lane < SSM_HEAD_DIM
    heads_per_group = H_SSM // SSM_GROUPS
    for grp in range(SSM_GROUPS):
        gs = slice(grp * GROUP_W, (grp + 1) * GROUP_W)
        bm_g = bm[:, grp * D_STATE:(grp + 1) * D_STATE]
        cm_b = cm[:, grp * D_STATE:(grp + 1) * D_STATE].astype(BF16)
        cb_mat = lax.dot_general(cm_b, bm_g.astype(BF16), NT_DIMS, preferred_element_type=F32)
        h_in = ht_ref[:, gs]
        y_off = jnp.dot(cm_b, h_in.astype(BF16), preferred_element_type=F32) * eac_x[:, gs]
        states = jnp.dot(bm_g.T.astype(BF16), xdd[:, gs], preferred_element_type=F32)
        ht_ref[:, gs] = h_in * eac_x[q - 1:q, gs] + states
        for hp in range(heads_per_group // 2):
            h0 = grp * heads_per_group + 2 * hp
            ps = slice(h0 * SSM_HEAD_DIM, (h0 + 2) * SSM_HEAD_DIM)
            x_pair = xdt_b[:, ps]
            halves = []
            for half in range(2):
                h = h0 + half
                seg = a_cum[:, h:h + 1] - a_cum_t[h:h + 1, :]
                decay = jnp.exp(jnp.where(causal, seg, -jnp.inf))
                halves.append(jnp.dot((cb_mat * decay).astype(BF16), x_pair, preferred_element_type=F32))
            y_diag = jnp.where(lo, halves[0], halves[1])
            off = slice(2 * hp * SSM_HEAD_DIM, (2 * hp + 2) * SSM_HEAD_DIM)
            y_ref[:, ps] = y_diag + y_off[:, off] + dsk_ref[:, ps] * xs[:, ps]
    y_ref[...] = _gated_group_norm(y_ref[...], z_ref[...], g_ref[...])

    @pl.when(c == pl.num_programs(1) - 1)
    def _():
        hout_ref[...] = ht_ref[...].T


def _ssd_prompt_call(proj, sp):
    b, s_len, _ = proj.shape
    q = SSD_CHUNK
    const = lambda shape: pl.BlockSpec(shape, lambda bi, c: (0,) * len(shape))
    y, h_last = pl.pallas_call(
        _ssd_prompt_kernel,
        out_shape=(jax.ShapeDtypeStruct((b, s_len, D_SSM), F32),
                   jax.ShapeDtypeStruct((b, D_SSM, D_STATE), F32)),
        grid=(b, s_len // q),
        in_specs=[pl.BlockSpec((None, q, D_SSM), lambda bi, c: (bi, c, 3)),
                  pl.BlockSpec((None, q, 2048), lambda bi, c: (bi, c, 2)),
                  const((CONV_W, CONV_DIM)), const((1, CONV_DIM)), const((1, 128)), const((1, 128)),
                  const((1, D_SSM)), const((1, D_SSM))],
        out_specs=(pl.BlockSpec((None, q, D_SSM), lambda bi, c: (bi, c, 0)),
                   pl.BlockSpec((None, D_SSM, D_STATE), lambda bi, c: (bi, 0, 0))),
        scratch_shapes=[pltpu.VMEM((D_STATE, D_SSM), F32), pltpu.VMEM((8 + q, CONV_DIM), F32)],
        compiler_params=pltpu.CompilerParams(
            dimension_semantics=("parallel", "arbitrary"), vmem_limit_bytes=VMEM_LIMIT_BYTES),
        name="ssd_prompt",
    )(proj, proj, sp['conv_w'], sp['conv_b'], sp['dt_bias'], sp['a_log'], sp['d_skip_x'], sp['g_ssm_out'])
    return y.reshape(b * s_len, D_SSM), h_last.reshape(b, H_SSM, SSM_HEAD_DIM, D_STATE)


def _ssd_sample_kernel(z_ref, xd_ref, buf_ref, h0_ref, cw_ref, cb_ref, dtb_ref, alog_ref, dsk_ref, g_ref,
                       y_ref, hout_ref, ext_ref):
    n_seq, n_tok, _ = z_ref.shape
    expand = _head_lane_mask(128).astype(BF16)
    ti = lax.broadcasted_iota(jnp.int32, (8, 8), 0)
    si = lax.broadcasted_iota(jnp.int32, (8, 8), 1)
    causal = (ti >= si) & (si < n_tok)
    real = lax.broadcasted_iota(jnp.int32, (8, 1), 0) < n_tok
    ones = jnp.ones((8, D_STATE), BF16)
    ext_ref[...] = jnp.zeros_like(ext_ref)

    def one_sequence(j, ext_ref):
        ext_ref[8 - (CONV_W - 1):8, :] = buf_ref[j]
        ext_ref[8:8 + n_tok, :] = xd_ref[j][:, 0:CONV_DIM]
        act = _causal_conv(ext_ref, 8, cw_ref, cb_ref)
        xs = act[:, 0:D_SSM]
        bm = act[:, D_SSM:D_SSM + SSM_GROUPS * D_STATE]
        cm = act[:, D_SSM + SSM_GROUPS * D_STATE:CONV_DIM]
        pad = jnp.zeros((8 - n_tok, 128), F32)
        dt = _softplus(jnp.concatenate([xd_ref[j][:, CONV_DIM:CONV_DIM + 128], pad], axis=0) + dtb_ref[...])
        da = jnp.where(real, dt * (-jnp.exp(alog_ref[...])), 0.0)
        a_cum = None
        for piece in _split_bf16(da, 3):
            t = jnp.dot((ti >= si).astype(BF16), piece, preferred_element_type=F32)
            a_cum = t if a_cum is None else a_cum + t
        a_last = a_cum[n_tok - 1:n_tok, :]
        acum_x = _dot_split(a_cum, expand, 3)
        eac_x = jnp.exp(acum_x)
        dt_x = _dot_split(dt, expand, 2)
        xdt = jnp.where(real, xs * dt_x, 0.0)
        xdd = (xdt * jnp.exp(acum_x[n_tok - 1:n_tok, :] - acum_x)).astype(BF16)
        last_x = jnp.where(lax.broadcasted_iota(jnp.int32, (8, 1), 0) == n_tok - 1, eac_x, 0.0)
        keep = _dot_split(last_x, ones, 2, dims=TN_DIMS)
        y_parts, new_state = [], []
        for grp in range(SSM_GROUPS):
            gs = slice(grp * GROUP_W, (grp + 1) * GROUP_W)
            bm_b = bm[:, grp * D_STATE:(grp + 1) * D_STATE].astype(BF16)
            cm_b = cm[:, grp * D_STATE:(grp + 1) * D_STATE].astype(BF16)
            cb_mat = jnp.where(causal, lax.dot_general(cm_b, bm_b, NT_DIMS, preferred_element_type=F32), 0.0)
            h0_g = h0_ref[j, gs, :]
            y_g = lax.dot_general(cm_b, h0_g.astype(BF16), NT_DIMS, preferred_element_type=F32) * eac_x[:, gs]
            for s in range(n_tok):
                coef = cb_mat[:, s:s + 1] * jnp.exp(jnp.where(ti[:, 0:1] >= s, acum_x[:, gs] - acum_x[s:s + 1, gs], -jnp.inf))
                y_g = y_g + coef * xdt[s:s + 1, gs]
            y_parts.append(y_g)
            states = lax.dot_general(xdd[:, gs], bm_b, TN_DIMS, preferred_element_type=F32)
            hout_ref[j, gs, :] = h0_g * keep[gs, :] + states
        y = jnp.concatenate(y_parts, axis=1) + dsk_ref[...] * xs
        zj = jnp.concatenate([z_ref[j], jnp.zeros((8 - n_tok, D_SSM), F32)], axis=0)
        y_ref[j] = _gated_group_norm(y, zj, g_ref[...])[0:n_tok, :]

    def body(jj, carry):
        for slot in range(2):
            one_sequence(2 * jj + slot, ext_ref.at[slot])
        return carry

    lax.fori_loop(0, n_seq // 2, body, 0)


def _ssd_sample_call(proj, conv_buf, h0, sp):
    b, n_tok, _ = proj.shape
    nseq = 8
    const = lambda shape: pl.BlockSpec(shape, lambda i: (0,) * len(shape))
    y, h_last = pl.pallas_call(
        _ssd_sample_kernel,
        out_shape=(jax.ShapeDtypeStruct((b, n_tok, D_SSM), F32),
                   jax.ShapeDtypeStruct((b, D_SSM, D_STATE), F32)),
        grid=(b // nseq,),
        in_specs=[pl.BlockSpec((nseq, n_tok, D_SSM), lambda i: (i, 0, 3)),
                  pl.BlockSpec((nseq, n_tok, 2048), lambda i: (i, 0, 2)),
                  pl.BlockSpec((nseq, CONV_W - 1, CONV_DIM), lambda i: (i, 0, 0)),
                  pl.BlockSpec((nseq, D_SSM, D_STATE), lambda i: (i, 0, 0)),
                  const((CONV_W, CONV_DIM)), const((1, CONV_DIM)), const((1, 128)), const((1, 128)),
                  const((1, D_SSM)), const((1, D_SSM))],
        out_specs=(pl.BlockSpec((nseq, n_tok, D_SSM), lambda i: (i, 0, 0)),
                   pl.BlockSpec((nseq, D_SSM, D_STATE), lambda i: (i, 0, 0))),
        scratch_shapes=[pltpu.VMEM((2, 16, CONV_DIM), F32)],
        compiler_params=pltpu.CompilerParams(
            dimension_semantics=("parallel",), vmem_limit_bytes=VMEM_LIMIT_BYTES),
        name="ssd_sample",
    )(proj, proj, conv_buf, h0.reshape(b, D_SSM, D_STATE), sp['conv_w'], sp['conv_b'], sp['dt_bias'],
      sp['a_log'], sp['d_skip_x'], sp['g_ssm_out'])
    return y.reshape(b * n_tok, D_SSM), h_last.reshape(b, H_SSM, SSM_HEAD_DIM, D_STATE)


def _split_proj(proj, b, L, keep):
    proj = proj.reshape(b, L, D_IN_PAD)
    k = proj[:, L - keep:, D_ATT:2 * D_ATT].reshape(b, keep, H_ATT, HEAD_DIM)
    v = proj[:, L - keep:, 2 * D_ATT:3 * D_ATT].reshape(b, keep, H_ATT, HEAD_DIM)
    o = 3 * D_ATT + D_SSM
    return proj, k, v, proj[:, L - (CONV_W - 1):, o:o + CONV_DIM]


def kernel(x_prompt, x_sample, cache_k, cache_v, state_ssm, state_conv, c_prompt, c_sample,
           rpb_table, w_ada, b_ada, g_ffn1, w_gate1, w_up1, w_down1, g_mix, w_in, conv_w, conv_b,
           dt_bias, a_log, d_skip, g_ssm_out, g_attn_out, w_out, g_ffn2, w_gate2, w_up2, w_down2, g_final):
    bp, sp, _ = x_prompt.shape
    bs, ls, _ = x_sample.shape
    wbp = min(WINDOW_MAX, sp)
    layer = 0
    w_in_pad = jnp.pad(jnp.transpose(w_in[layer]).astype(BF16), ((0, D_IN_PAD - D_IN), (0, 0)))
    p = {'g_ffn1': g_ffn1[layer], 'g_mix': g_mix[layer], 'w_in_pad': w_in_pad,
         'conv_w': conv_w[layer], 'conv_b': conv_b[layer].reshape(1, CONV_DIM),
         'dt_bias': jnp.pad(dt_bias[layer], (0, 128 - H_SSM)).reshape(1, 128),
         'a_log': jnp.pad(a_log[layer], (0, 128 - H_SSM)).reshape(1, 128),
         'd_skip_x': jnp.repeat(d_skip[layer], SSM_HEAD_DIM).reshape(1, D_SSM),
         'g_ssm_out': g_ssm_out[layer].reshape(1, D_SSM), 'g_attn_out': g_attn_out[layer],
         'w_out_att': w_out[layer, :D_ATT].astype(BF16), 'w_out_ssm': w_out[layer, D_ATT:].astype(BF16),
         'g_ffn2': g_ffn2[layer]}

    n_c = bp + bs
    c_rows = -(-n_c // 16) * 16
    c_all = jnp.pad(jnp.concatenate([c_prompt, c_sample], axis=0), ((0, c_rows - n_c), (0, 0)))
    mod = _mod_call(c_all, w_ada[layer], b_ada[layer])
    mods_p = [mod[:bp, i * D_MODEL:(i + 1) * D_MODEL].reshape(bp, 1, D_MODEL) for i in range(N_MOD)]
    mods_s = [jnp.repeat(mod[bp:n_c, i * D_MODEL:(i + 1) * D_MODEL], ls, axis=0) for i in range(N_MOD)]

    lb_c, lb_n = _sample_log_bias(rpb_table, ls, cache_k.shape[2])
    cache_kt = jnp.transpose(cache_k[layer], (0, 2, 3, 1))
    cache_vt = jnp.transpose(cache_v[layer], (0, 2, 3, 1))
    dils = tuple(d for _, d in DILATED_PATTERNS)
    sh1, sc1, g1, sh2, sc2, g2, sh3, sc3, g3 = mods_p
    th1, tc1, t1, th2, tc2, t2, th3, tc3, t3 = mods_s
    kw_p = dict(tm=1024, rows_per_mod=sp)
    kw_s = dict(tm=bs * ls, rows_per_mod=ls)

    xs, *w1 = _ffn_call(x_sample.reshape(bs * ls, D_MODEL), p['g_ffn1'], th1, tc1, t1,
                        w_gate1[layer], w_up1[layer], w_down1[layer], g_final,
                        final_norm=False, emit_weights=True, **kw_s)
    ffn1 = (p['g_ffn1'], *w1)
    (proj_s,) = _inproj_call(xs, p['g_mix'], th2, tc2, p['w_in_pad'], dilations=(), **kw_s)
    proj_s, kn, vn, cs = _split_proj(proj_s, bs, ls, ls)
    rider = lambda first, n: (proj_s, cache_kt, cache_vt, lb_c, lb_n, first, n)

    xp, att_a = _ffn_call(x_prompt.reshape(bp * sp, D_MODEL), ffn1[0], sh1, sc1, g1, *ffn1[1:], g_final,
                          final_norm=False, rider=rider(0, bs // 2), **kw_p)
    proj_p, *qkv = _inproj_call(xp, p['g_mix'], sh2, sc2, p['w_in_pad'], dilations=dils, **kw_p)
    proj_p, kp, vp, cp = _split_proj(proj_p, bp, sp, wbp)
    outs, lses, w2 = [], [], []
    for (window, dil), qkv_d, w in zip(DILATED_PATTERNS, qkv, (w_gate2[layer], w_up2[layer], w_down2[layer])):
        o, lse, wb = _attn_prompt_call(qkv_d, _prompt_bias(rpb_table, window, dil), dil, bp, w)
        outs.append(o)
        lses.append(lse)
        w2.append(wb)
    ffn2 = (p['g_ffn2'], *w2)
    ssd_p, hp = _ssd_prompt_call(proj_p, p)
    xp = _outproj_call(xp, outs + lses, ssd_p, g2, p['w_out_att'], p['w_out_ssm'], p['g_attn_out'],
                       tm=256, rows_per_mod=sp, dilations=dils)
    yp, att_b = _ffn_call(xp, ffn2[0], sh3, sc3, g3, *ffn2[1:], g_final,
                          final_norm=True, rider=rider(bs // 2, bs - bs // 2), **kw_p)

    att_s = jnp.concatenate([att_a, att_b], axis=0).reshape(bs * ls, D_ATT)
    ssd_s, hs = _ssd_sample_call(proj_s, state_conv[layer], state_ssm[layer], p)
    xs = _outproj_call(xs, [att_s], ssd_s, t2, p['w_out_att'], p['w_out_ssm'], p['g_attn_out'],
                       tm=256, rows_per_mod=ls, dilations=())
    ys = _ffn_call(xs, ffn2[0], th3, tc3, t3, *ffn2[1:], g_final, final_norm=True, **kw_s)
    return (yp.reshape(bp, sp, D_MODEL), ys.reshape(bs, ls, D_MODEL), kp[None], vp[None],
            kn[None], vn[None], hp[None], hs[None], cp[None], cs[None])
```

```python
import functools
import math

import jax
import jax.numpy as jnp
from jax import lax
from jax.experimental import pallas as pl
from jax.experimental.pallas import tpu as pltpu

D_MODEL = 2048
HEAD_DIM = 64
D_ATT = 1024
D_SSM = 1024
H_ATT = 16
SSM_HEAD_DIM = 64
H_SSM = 16
SSM_GROUPS = 2
D_STATE = 128
CONV_W = 4
SSD_CHUNK = 128
CONV_DIM = D_SSM + 2 * SSM_GROUPS * D_STATE
D_FF = 5632
DILATED_PATTERNS = ((128, 1), (512, 4), (2048, 16))
WINDOW_MAX = 2048
ATT_BLOCK = 128
RPB_BUCKETS = 32
RPB_MAX_DIST = 2048
NORM_EPS = 1e-6
N_MOD = 9
PROJ_SIZES = (D_ATT, D_ATT, D_ATT, D_SSM, CONV_DIM, H_SSM)
D_IN = sum(PROJ_SIZES)
D_IN_PAD = 6144

VMEM_LIMIT_BYTES = 56 * 1024 * 1024

BF16 = jnp.bfloat16
F32 = jnp.float32


def _silu(v):
    return v * jax.nn.sigmoid(v)


def _norm_mod(x, g, shift, scale):
    y = x * lax.rsqrt(jnp.mean(x * x, axis=-1, keepdims=True) + NORM_EPS)
    return (y * g) * (1.0 + scale) + shift


def _mod_kernel(c_ref, w_ref, b_ref, o_ref, a_ref):
    @pl.when(pl.program_id(0) == 0)
    def _():
        a_ref[...] = _silu(c_ref[...]).astype(BF16)

    o_ref[...] = jnp.dot(a_ref[...], w_ref[...].astype(BF16),
                         preferred_element_type=F32) + b_ref[...]


def _mod_call(c_all, w_ada, b_ada):
    rows = c_all.shape[0]
    n = w_ada.shape[1]
    tn = 1024
    return pl.pallas_call(
        _mod_kernel,
        out_shape=jax.ShapeDtypeStruct((rows, n), F32),
        grid=(n // tn,),
        in_specs=[
            pl.BlockSpec((rows, D_MODEL), lambda j: (0, 0)),
            pl.BlockSpec((D_MODEL, tn), lambda j: (0, j)),
            pl.BlockSpec((1, tn), lambda j: (0, j)),
        ],
        out_specs=pl.BlockSpec((rows, tn), lambda j: (0, j)),
        scratch_shapes=[pltpu.VMEM((rows, D_MODEL), BF16)],
        compiler_params=pltpu.CompilerParams(
            dimension_semantics=("arbitrary",), vmem_limit_bytes=VMEM_LIMIT_BYTES),
        name="adaln_mod",
    )(c_all, w_ada, b_ada.reshape(1, n))


NT_DIMS = (((1,), (1,)), ((), ()))
RIDER_HEADS = 8
RIDER_GROUP = 8


def _sample_attn_heads(q_ref, kn_ref, vn_ref, kt_ref, vt_ref, lbc_ref, lbn_ref, o_ref):
    n_q, width = q_ref.shape
    pad = jnp.zeros((8 - n_q, width), F32)
    q8 = jnp.concatenate([q_ref[...] * (HEAD_DIM ** -0.5), pad], axis=0).astype(BF16)
    kn8 = jnp.concatenate([kn_ref[...], pad], axis=0).astype(BF16)
    vn8 = jnp.concatenate([vn_ref[...], pad], axis=0).astype(BF16)
    heads = width // HEAD_DIM
    wb = kt_ref.shape[2]
    hs = lambda h: slice(h * HEAD_DIM, (h + 1) * HEAD_DIM)
    group = RIDER_GROUP
    for g0 in range(0, heads, group):
        hh = range(g0, g0 + group)
        t_c = jnp.concatenate([jnp.dot(q8[:, hs(h)], kt_ref[h].astype(BF16), preferred_element_type=F32)
                               for h in hh], axis=0) + lbc_ref[g0:g0 + group].reshape(group * 8, wb)
        t_n = jnp.concatenate([lax.dot_general(q8[:, hs(h)], kn8[:, hs(h)], NT_DIMS,
                                               preferred_element_type=F32)
                               for h in hh], axis=0) + lbn_ref[g0:g0 + group].reshape(group * 8, 8)
        m = jnp.maximum(jnp.max(t_c, axis=-1, keepdims=True), jnp.max(t_n, axis=-1, keepdims=True))
        p_c = jnp.exp(t_c - m)
        p_n = jnp.exp(t_n - m)
        inv_l = 1.0 / (jnp.sum(p_c, axis=-1, keepdims=True) + jnp.sum(p_n, axis=-1, keepdims=True))
        for i, h in enumerate(hh):
            rows = slice(i * 8, (i + 1) * 8)
            acc = lax.dot_general(p_c[rows].astype(BF16), vt_ref[h].astype(BF16), NT_DIMS,
                                  preferred_element_type=F32)
            acc += jnp.dot(p_n[rows].astype(BF16), vn8[:, hs(h)], preferred_element_type=F32)
            o_ref[:, hs(h)] = (acc * inv_l[rows])[0:n_q, :]


def _ffn_kernel(*refs, final_norm, rider, emit_weights):
    x_ref, g_ref, sh_ref, sc_ref, gt_ref, wg_ref, wu_ref, wd_ref, gf_ref = refs[:9]
    if rider:
        rider_in, (o_ref, att_ref, h_ref) = refs[9:16], refs[16:]
    elif emit_weights:
        o_ref, wgb_ref, wub_ref, wdb_ref, h_ref = refs[9:]
    else:
        o_ref, h_ref = refs[9:]
    j = pl.program_id(1)

    @pl.when(j == 0)
    def _():
        h_ref[...] = _norm_mod(x_ref[...], g_ref[...], sh_ref[...], sc_ref[...]).astype(BF16)
        o_ref[...] = jnp.zeros_like(o_ref)

    wg, wu, wd = wg_ref[...], wu_ref[...], wd_ref[...]
    if emit_weights:
        wg, wu, wd = wg.astype(BF16), wu.astype(BF16), wd.astype(BF16)
        wgb_ref[...] = wg
        wub_ref[...] = wu
        wdb_ref[...] = wd
    h = h_ref[...]
    a = jnp.dot(h, wg, preferred_element_type=F32)
    b = jnp.dot(h, wu, preferred_element_type=F32)
    s = (_silu(a) * b).astype(BF16)
    o_ref[...] += jnp.dot(s, wd, preferred_element_type=F32)
    if rider:
        @pl.when(pl.program_id(0) * pl.num_programs(1) + j < rider)
        def _():
            _sample_attn_heads(*rider_in, att_ref)

    @pl.when(j == pl.num_programs(1) - 1)
    def _():
        y = x_ref[...] + 0.5 * gt_ref[...] * o_ref[...]
        if final_norm:
            y = y * lax.rsqrt(jnp.mean(y * y, axis=-1, keepdims=True) + NORM_EPS) * gf_ref[...]
        o_ref[...] = y


def _mod_spec(mod, tm, rows_per_mod):
    if mod.ndim == 3:
        tiles_per_row = rows_per_mod // tm
        return pl.BlockSpec((None, 1, D_MODEL), lambda i, j: (i // tiles_per_row, 0, 0))
    return pl.BlockSpec((tm, D_MODEL), lambda i, j: (i, 0))


def _ffn_call(x, g, sh, sc, gt, wg, wu, wd, g_final, *, tm, rows_per_mod, final_norm, rider=None,
              emit_weights=False):
    m = x.shape[0]
    assert not (rider and emit_weights) and (not emit_weights or m == tm)
    tf = 256 if (rider or emit_weights) else 512
    nj = D_FF // tf
    row = lambda i, j: (i, 0)
    const = lambda i, j: (0, 0)
    in_specs = [
        pl.BlockSpec((tm, D_MODEL), row, pipeline_mode=pl.Buffered(1)),
        pl.BlockSpec((1, D_MODEL), const),
        _mod_spec(sh, tm, rows_per_mod),
        _mod_spec(sc, tm, rows_per_mod),
        _mod_spec(gt, tm, rows_per_mod),
        pl.BlockSpec((D_MODEL, tf), lambda i, j: (0, j)),
        pl.BlockSpec((D_MODEL, tf), lambda i, j: (0, j)),
        pl.BlockSpec((tf, D_MODEL), lambda i, j: (j, 0)),
        pl.BlockSpec((1, D_MODEL), const),
    ]
    args = [x, g.reshape(1, D_MODEL), sh, sc, gt, wg, wu, wd, g_final.reshape(1, D_MODEL)]
    out_shape = jax.ShapeDtypeStruct((m, D_MODEL), F32)
    out_specs = pl.BlockSpec((tm, D_MODEL), row)
    if rider:
        proj_s, cache_kt, cache_vt, lb_c, lb_n, first_seq, n_seq = rider
        n_q = proj_s.shape[1]
        wb = cache_kt.shape[3]
        halves = H_ATT // RIDER_HEADS
        n_units = n_seq * halves
        assert n_units <= (m // tm) * nj
        rw = RIDER_HEADS * HEAD_DIM
        unit = lambda i, j: jnp.minimum(i * nj + j, n_units - 1)
        seq = lambda i, j: first_seq + unit(i, j) // halves
        half = lambda i, j: unit(i, j) % halves
        tiles = D_ATT // rw
        qspec = lambda t: pl.BlockSpec((None, n_q, rw), lambda i, j: (seq(i, j), 0, t * tiles + half(i, j)))
        cspec = pl.BlockSpec((None, RIDER_HEADS, HEAD_DIM, wb), lambda i, j: (seq(i, j), half(i, j), 0, 0))
        in_specs += [qspec(0), qspec(1), qspec(2), cspec, cspec,
                     pl.BlockSpec((RIDER_HEADS, 8, wb), lambda i, j: (half(i, j), 0, 0)),
                     pl.BlockSpec((RIDER_HEADS, 8, 8), lambda i, j: (half(i, j), 0, 0))]
        args += [proj_s, proj_s, proj_s, cache_kt, cache_vt, lb_c, lb_n]
        out_shape = (out_shape, jax.ShapeDtypeStruct((n_seq, n_q, D_ATT), F32))
        out_specs = (pl.BlockSpec((tm, D_MODEL), row, pipeline_mode=pl.Buffered(1)),
                     pl.BlockSpec((None, n_q, rw), lambda i, j: (seq(i, j) - first_seq, 0, half(i, j))))
    if emit_weights:
        out_shape = (out_shape, jax.ShapeDtypeStruct(wg.shape, BF16), jax.ShapeDtypeStruct(wu.shape, BF16),
                     jax.ShapeDtypeStruct(wd.shape, BF16))
        out_specs = (out_specs,) + tuple(in_specs[5:8])
    return pl.pallas_call(
        functools.partial(_ffn_kernel, final_norm=final_norm, rider=n_units if rider else 0,
                          emit_weights=emit_weights),
        out_shape=out_shape,
        grid=(m // tm, nj),
        in_specs=in_specs,
        out_specs=out_specs,
        scratch_shapes=[pltpu.VMEM((tm, D_MODEL), BF16)],
        compiler_params=pltpu.CompilerParams(
            dimension_semantics=("arbitrary", "arbitrary") if rider else ("parallel", "arbitrary"),
            vmem_limit_bytes=VMEM_LIMIT_BYTES),
        name=("ffn_final" if final_norm else "ffn") + ("_rider" if rider else ""),
    )(*args)


QKV_TILES = 3


def _inproj_kernel(x_ref, g_ref, sh_ref, sc_ref, w_ref, o_ref, *rest, dilations):
    qkv_refs, h_ref = rest[:-1], rest[-1]
    j = pl.program_id(1)
    tm = o_ref.shape[0]

    @pl.when(j == 0)
    def _():
        h_ref[...] = _norm_mod(x_ref[...], g_ref[...], sh_ref[...], sc_ref[...]).astype(BF16)

    res = lax.dot_general(h_ref[...], w_ref[...], NT_DIMS, preferred_element_type=F32)
    o_ref[...] = res

    @pl.when(j < QKV_TILES)
    def _():
        res_b = res.astype(BF16)
        part = min(tm, 256)
        out_row = lax.broadcasted_iota(jnp.int32, (part, part), 0)
        in_row = lax.broadcasted_iota(jnp.int32, (part, part), 1)
        for d, ref in zip(dilations, qkv_refs):
            if d == 1:
                ref[...] = res_b
                continue
            n = part // d
            perm = (in_row == (out_row % n) * d + out_row // n).astype(BF16)
            for s in range(tm // part):
                grouped = jnp.dot(perm, res_b[s * part:(s + 1) * part, :], preferred_element_type=F32)
                grouped = grouped.astype(BF16)
                for r in range(d):
                    ref[s * n:(s + 1) * n, r * D_ATT:(r + 1) * D_ATT] = grouped[r * n:(r + 1) * n, :]


def _inproj_call(x, g, sh, sc, w_pad_t, *, tm, rows_per_mod, dilations):
    m = x.shape[0]
    tn = D_ATT
    qkv_j = lambda j: jnp.minimum(j, QKV_TILES - 1)
    return pl.pallas_call(
        functools.partial(_inproj_kernel, dilations=dilations),
        out_shape=(jax.ShapeDtypeStruct((m, D_IN_PAD), F32),)
        + tuple(jax.ShapeDtypeStruct((m // d, QKV_TILES * d * tn), BF16) for d in dilations),
        grid=(m // tm, D_IN_PAD // tn),
        in_specs=[
            pl.BlockSpec((tm, D_MODEL), lambda i, j: (i, 0), pipeline_mode=pl.Buffered(1)),
            pl.BlockSpec((1, D_MODEL), lambda i, j: (0, 0)),
            _mod_spec(sh, tm, rows_per_mod),
            _mod_spec(sc, tm, rows_per_mod),
            pl.BlockSpec((tn, D_MODEL), lambda i, j: (j, 0)),
        ],
        out_specs=(pl.BlockSpec((tm, tn), lambda i, j: (i, j)),)
        + tuple(pl.BlockSpec((tm // d, d * tn), lambda i, j: (i, qkv_j(j))) for d in dilations),
        scratch_shapes=[pltpu.VMEM((tm, D_MODEL), BF16)],
        compiler_params=pltpu.CompilerParams(
            dimension_semantics=("parallel", "arbitrary"), vmem_limit_bytes=VMEM_LIMIT_BYTES),
        name="in_proj",
    )(x, g.reshape(1, D_MODEL), sh, sc, w_pad_t)


def _outproj_kernel(*refs, dilations):
    x_ref = refs[0]
    n_pat = len(dilations)
    if n_pat:
        o_refs = refs[1:1 + n_pat]
        l_refs = refs[1 + n_pat:1 + 2 * n_pat]
        ga_ref, ssd_ref, gt_ref, wa_ref, ws_ref, out_ref, o_scr, l_scr = refs[1 + 2 * n_pat:]
        tm = x_ref.shape[0]
        n_chunks = D_ATT // 128
        outs, lses = [], []
        for p, (d, o_ref, l_ref) in enumerate(zip(dilations, o_refs, l_refs)):
            if d == 1:
                outs.append(o_ref[...].astype(F32))
                lses.append(l_ref[...])
                continue
            for r in range(d):
                l_scr[p, pl.ds(r, tm // d, stride=d), :] = l_ref[:, r * 128:(r + 1) * 128]
                for c in range(n_chunks):
                    lanes = slice(r * D_ATT + c * 128, r * D_ATT + (c + 1) * 128)
                    o_scr[p, c, pl.ds(r, tm // d, stride=d), :] = o_ref[:, lanes].astype(F32)
            outs.append(jnp.concatenate([o_scr[p, c] for c in range(n_chunks)], axis=1))
            lses.append(l_scr[p])
        mx = functools.reduce(jnp.maximum, lses)
        ws = [jnp.exp(v - mx) for v in lses]
        wsum = functools.reduce(lambda a, b: a + b, ws)
        expand = _head_lane_mask(128).astype(BF16)
        att = None
        for w, o in zip(ws, outs):
            wn = w / wsum
            w_hi = wn.astype(BF16)
            w_lo = (wn - w_hi.astype(F32)).astype(BF16)
            wx = (jnp.dot(w_hi, expand, preferred_element_type=F32)
                  + jnp.dot(w_lo, expand, preferred_element_type=F32))
            att = wx * o if att is None else att + wx * o
    else:
        att_ref, ga_ref, ssd_ref, gt_ref, wa_ref, ws_ref, out_ref = refs[1:]
        att = att_ref[...]
    att = att * lax.rsqrt(jnp.mean(att * att, axis=-1, keepdims=True) + NORM_EPS) * ga_ref[...]
    mix = jnp.dot(att.astype(BF16), wa_ref[...], preferred_element_type=F32)
    mix += jnp.dot(ssd_ref[...].astype(BF16), ws_ref[...], preferred_element_type=F32)
    out_ref[...] = x_ref[...] + gt_ref[...] * mix


def _outproj_call(x, att_parts, ssd, gt, w_att, w_ssd, g_attn, *, tm, rows_per_mod, dilations):
    m = x.shape[0]
    n_pat = len(dilations)
    if gt.ndim == 3:
        tiles_per_row = rows_per_mod // tm
        gt_spec = pl.BlockSpec((None, 1, D_MODEL), lambda i: (i // tiles_per_row, 0, 0))
    else:
        gt_spec = pl.BlockSpec((tm, D_MODEL), lambda i: (i, 0))
    row = lambda width, d=1: pl.BlockSpec((tm // d, d * width), lambda i: (i, 0))
    scratch = []
    if n_pat:
        att_specs = ([row(D_ATT, d) for d in dilations] + [row(128, d) for d in dilations]
                     + [pl.BlockSpec((1, D_ATT), lambda i: (0, 0))])
        att_args = list(att_parts) + [g_attn.reshape(1, D_ATT)]
        scratch = [pltpu.VMEM((n_pat, D_ATT // 128, tm, 128), F32), pltpu.VMEM((n_pat, tm, 128), F32)]
    else:
        att_specs = [row(D_ATT), pl.BlockSpec((1, D_ATT), lambda i: (0, 0))]
        att_args = list(att_parts) + [g_attn.reshape(1, D_ATT)]
    weight = lambda k: pl.BlockSpec((k, D_MODEL), lambda i: (0, 0), pipeline_mode=pl.Buffered(1))
    return pl.pallas_call(
        functools.partial(_outproj_kernel, dilations=dilations),
        out_shape=jax.ShapeDtypeStruct((m, D_MODEL), F32),
        grid=(m // tm,),
        in_specs=[row(D_MODEL)] + att_specs + [row(D_SSM), gt_spec, weight(D_ATT), weight(D_SSM)],
        out_specs=row(D_MODEL),
        scratch_shapes=scratch,
        compiler_params=pltpu.CompilerParams(
            dimension_semantics=("parallel",), vmem_limit_bytes=VMEM_LIMIT_BYTES),
        name="out_proj_merge" if n_pat else "out_proj",
    )(x, *att_args, ssd, gt, w_att, w_ssd)


def _rel_bucket(dist):
    max_exact = RPB_BUCKETS // 2
    df = jnp.maximum(dist, 1).astype(jnp.float32)
    large = max_exact + (jnp.log(df / max_exact) / math.log(RPB_MAX_DIST / max_exact)
                         * (RPB_BUCKETS - max_exact)).astype(jnp.int32)
    large = jnp.minimum(large, RPB_BUCKETS - 1)
    return jnp.where(dist < max_exact, dist, large)


def _head_lane_mask(rows=H_ATT):
    r = lax.broadcasted_iota(jnp.int32, (rows, D_ATT), 0)
    c = lax.broadcasted_iota(jnp.int32, (rows, D_ATT), 1)
    return (c // HEAD_DIM == r).astype(F32)


def _attn_prompt_kernel(q_ref, kp_ref, kc_ref, vp_ref, vc_ref, bias_ref, o_ref, lse_ref):
    nb = pl.program_id(2)
    blk = ATT_BLOCK
    col = lax.broadcasted_iota(jnp.int32, (blk, 2 * blk), 1)
    has_prev = (col >= blk) | (nb > 0)
    lane = lax.broadcasted_iota(jnp.int32, (blk, 2 * HEAD_DIM), 1)
    lo = lane < HEAD_DIM
    lse_lane = lax.broadcasted_iota(jnp.int32, (blk, 128), 1)
    lse_tile = jnp.zeros((blk, 128), F32)
    scale = HEAD_DIM ** -0.5
    for hp in range(H_ATT // 2):
        sl = slice(hp * 2 * HEAD_DIM, (hp + 1) * 2 * HEAD_DIM)
        q2 = q_ref[:, sl] * scale
        k2 = jnp.concatenate([kp_ref[:, sl], kc_ref[:, sl]], axis=0)
        v2 = jnp.concatenate([vp_ref[:, sl], vc_ref[:, sl]], axis=0)
        outs = []
        for half in range(2):
            h = 2 * hp + half
            qh = jnp.where(lo if half == 0 else ~lo, q2, jnp.zeros_like(q2))
            s = lax.dot_general(qh, k2, NT_DIMS, preferred_element_type=F32) + bias_ref[h]
            s = jnp.where(has_prev, s, -jnp.inf)
            m = jnp.max(s, axis=-1, keepdims=True)
            p = jnp.exp(s - m)
            l = jnp.sum(p, axis=-1, keepdims=True)
            pv = jnp.dot(p.astype(BF16), v2, preferred_element_type=F32)
            outs.append(pv * (1.0 / l))
            lse_ref[:, h:h + 1] = m + jnp.log(l)
        o_ref[:, sl] = jnp.where(lo, outs[0], outs[1]).astype(o_ref.dtype)
    lse_ref[:, H_ATT:] = jnp.zeros((blk, 128 - H_ATT), F32)


def _attn_prompt_cast_kernel(*refs):
    w_ref, wb_ref = refs[6], refs[9]
    wb_ref[...] = w_ref[...].astype(BF16)
    _attn_prompt_kernel(*refs[:6], *refs[7:9])


def _attn_prompt_call(qkv, bias, dil, b, w=None):
    blk = ATT_BLOCK
    n_sub = qkv.shape[0] // b
    nbk = n_sub // blk
    pv = qkv.reshape(b, n_sub, qkv.shape[1])
    cur = lambda t: (lambda bi, r, nb: (bi, nb, t * dil + r))
    prev = lambda t: (lambda bi, r, nb: (bi, jnp.maximum(nb - 1, 0), t * dil + r))
    spec = lambda f: pl.BlockSpec((None, blk, D_ATT), f)
    in_specs = [spec(cur(0)), spec(prev(1)), spec(cur(1)), spec(prev(2)), spec(cur(2)),
                pl.BlockSpec((H_ATT, blk, 2 * blk), lambda bi, r, nb: (0, 0, 0))]
    out_shape = [jax.ShapeDtypeStruct((b, n_sub, dil * D_ATT), BF16),
                 jax.ShapeDtypeStruct((b, n_sub, dil * 128), F32)]
    out_specs = [pl.BlockSpec((None, blk, D_ATT), lambda bi, r, nb: (bi, nb, r)),
                 pl.BlockSpec((None, blk, 128), lambda bi, r, nb: (bi, nb, r))]
    args = [pv, pv, pv, pv, pv, bias]
    if w is not None:
        axis = 0 if w.shape[0] > w.shape[1] else 1
        n_slabs = w.shape[axis] // 128
        assert n_slabs * 128 == w.shape[axis] and n_slabs <= b * dil * nbk
        slab = lambda bi, r, nb: jnp.minimum((bi * dil + r) * nbk + nb, n_slabs - 1)
        if axis == 0:
            w_spec = pl.BlockSpec((128, w.shape[1]), lambda bi, r, nb: (slab(bi, r, nb), 0))
        else:
            w_spec = pl.BlockSpec((w.shape[0], 128), lambda bi, r, nb: (0, slab(bi, r, nb)))
        in_specs.append(w_spec)
        out_shape.append(jax.ShapeDtypeStruct(w.shape, BF16))
        out_specs.append(w_spec)
        args.append(w)
    res = pl.pallas_call(
        _attn_prompt_kernel if w is None else _attn_prompt_cast_kernel,
        out_shape=tuple(out_shape),
        grid=(b, dil, nbk),
        in_specs=in_specs,
        out_specs=tuple(out_specs),
        compiler_params=pltpu.CompilerParams(
            dimension_semantics=("parallel", "parallel", "arbitrary") if w is None else ("arbitrary",) * 3,
            vmem_limit_bytes=VMEM_LIMIT_BYTES),
        name=f"attn_prompt_d{dil}",
    )(*args)
    o, lse = res[0].reshape(b * n_sub, dil * D_ATT), res[1].reshape(b * n_sub, dil * 128)
    return (o, lse) if w is None else (o, lse, res[2])


def _prompt_bias(rpb, window, dil):
    span = window // dil
    blk = ATT_BLOCK
    assert span == blk
    table = rpb[_rel_bucket(jnp.arange(span + 1) * dil)].T.astype(F32)
    fill = jnp.full((H_ATT, blk), -jnp.inf, F32)
    row = jnp.concatenate([fill, table[:, ::-1], fill], axis=1)
    wrapped = jnp.tile(row, (1, blk))[:, :blk * 3 * blk].reshape(H_ATT, blk, 3 * blk)
    return wrapped[:, :, blk:]


def _sample_log_bias(rpb, n_q, wb):
    reach = max(w for w, _ in DILATED_PATTERNS) + 2 * 8
    assert wb + n_q <= reach
    terms = []
    for window, dil in DILATED_PATTERNS:
        span = window // dil
        table = rpb[_rel_bucket(jnp.arange(span + 1) * dil)].T.astype(F32)
        gaps = jnp.full((H_ATT, span + 1, dil - 1), -jnp.inf, F32)
        by_dist = jnp.concatenate([table[:, :, None], gaps], axis=2).reshape(H_ATT, (span + 1) * dil)
        by_dist = by_dist[:, :reach]
        terms.append(jnp.pad(by_dist, ((0, 0), (0, reach - by_dist.shape[1])), constant_values=-jnp.inf))
    lb = jax.nn.logsumexp(jnp.stack(terms), axis=0)
    neg = jnp.full((H_ATT,), -jnp.inf, F32)
    rows_c, rows_n = [], []
    for r in range(8):
        n = r % n_q
        rows_c.append(lb[:, n + 1:n + 1 + wb][:, ::-1])
        rows_n.append(jnp.stack([lb[:, n - m] if (m < n_q and m <= n) else neg for m in range(8)], axis=1))
    return jnp.stack(rows_c, axis=1), jnp.stack(rows_n, axis=1)


TN_DIMS = (((0,), (0,)), ((), ()))
GROUP_W = D_SSM // SSM_GROUPS


def _split_bf16(v, pieces):
    out = []
    for _ in range(pieces):
        hi = v.astype(BF16)
        out.append(hi)
        v = v - hi.astype(F32)
    return out


def _dot_split(v, rhs_bf16, pieces, dims=None):
    acc = None
    for piece in _split_bf16(v, pieces):
        if dims is None:
            t = jnp.dot(piece, rhs_bf16, preferred_element_type=F32)
        else:
            t = lax.dot_general(piece, rhs_bf16, dims, preferred_element_type=F32)
        acc = t if acc is None else acc + t
    return acc


def _softplus(v):
    return jnp.maximum(v, 0.0) + jnp.log1p(jnp.exp(-jnp.abs(v)))


def _causal_conv(ext_ref, rows, cw_ref, cb_ref):
    conv = cb_ref[...]
    for k in range(CONV_W):
        conv = conv + cw_ref[CONV_W - 1 - k:CONV_W - k, :] * ext_ref[8 - k:8 - k + rows, :]
    return _silu(conv)


def _gated_group_norm(y, z, g):
    gated = y * _silu(z)
    parts = []
    for grp in range(SSM_GROUPS):
        seg = gated[:, grp * GROUP_W:(grp + 1) * GROUP_W]
        parts.append(seg * lax.rsqrt(jnp.mean(seg * seg, axis=-1, keepdims=True) + NORM_EPS))
    return jnp.concatenate(parts, axis=1) * g


def _ssd_prompt_kernel(z_ref, xd_ref, cw_ref, cb_ref, dtb_ref, alog_ref, dsk_ref, g_ref,
                       y_ref, hout_ref, ht_ref, ext_ref):
    c = pl.program_id(1)
    q = SSD_CHUNK

    @pl.when(c == 0)
    def _():
        ht_ref[...] = jnp.zeros_like(ht_ref)
        ext_ref[0:8, :] = jnp.zeros((8, CONV_DIM), F32)

    x_raw = xd_ref[:, 0:CONV_DIM]
    ext_ref[8:8 + q, :] = x_raw
    act = _causal_conv(ext_ref, q, cw_ref, cb_ref)
    ext_ref[0:8, :] = x_raw[q - 8:q, :]
    xs = act[:, 0:D_SSM]
    bm = act[:, D_SSM:D_SSM + SSM_GROUPS * D_STATE]
    cm = act[:, D_SSM + SSM_GROUPS * D_STATE:CONV_DIM]

    dt = _softplus(xd_ref[:, CONV_DIM:CONV_DIM + 128] + dtb_ref[...])
    da = dt * (-jnp.exp(alog_ref[...]))
    ti = lax.broadcasted_iota(jnp.int32, (q, q), 0)
    si = lax.broadcasted_iota(jnp.int32, (q, q), 1)
    causal = ti >= si
    a_cum = None
    for piece in _split_bf16(da, 3):
        t = jnp.dot(causal.astype(BF16), piece, preferred_element_type=F32)
        a_cum = t if a_cum is None else a_cum + t
    a_cum_t = a_cum.T
    expand = _head_lane_mask(128).astype(BF16)
    eac = jnp.exp(a_cum)
    dt_x = _dot_split(dt, expand, 2)
    eac_x = _dot_split(eac, expand, 2)
    dend_x = _dot_split(jnp.exp(a_cum[q - 1:q, :] - a_cum), expand, 2)
    xdt = xs * dt_x
    xdd = (xdt * dend_x).astype(BF16)
    xdt_b = xdt.astype(BF16)
    lane = lax.broadcasted_iota(jnp.int32, (q, 2 * SSM_HEAD_DIM), 1)
    lo = lane < SSM_HEAD_DIM
    heads_per_group = H_SSM // SSM_GROUPS
    for grp in range(SSM_GROUPS):
        gs = slice(grp * GROUP_W, (grp + 1) * GROUP_W)
        bm_g = bm[:, grp * D_STATE:(grp + 1) * D_STATE]
        cm_b = cm[:, grp * D_STATE:(grp + 1) * D_STATE].astype(BF16)
        cb_mat = lax.dot_general(cm_b, bm_g.astype(BF16), NT_DIMS, preferred_element_type=F32)
        h_in = ht_ref[:, gs]
        y_off = jnp.dot(cm_b, h_in.astype(BF16), preferred_element_type=F32) * eac_x[:, gs]
        states = jnp.dot(bm_g.T.astype(BF16), xdd[:, gs], preferred_element_type=F32)
        ht_ref[:, gs] = h_in * eac_x[q - 1:q, gs] + states
        for hp in range(heads_per_group // 2):
            h0 = grp * heads_per_group + 2 * hp
            ps = slice(h0 * SSM_HEAD_DIM, (h0 + 2) * SSM_HEAD_DIM)
            x_pair = xdt_b[:, ps]
            halves = []
            for half in range(2):
                h = h0 + half
                seg = a_cum[:, h:h + 1] - a_cum_t[h:h + 1, :]
                decay = jnp.exp(jnp.where(causal, seg, -jnp.inf))
                halves.append(jnp.dot((cb_mat * decay).astype(BF16), x_pair, preferred_element_type=F32))
            y_diag = jnp.where(lo, halves[0], halves[1])
            off = slice(2 * hp * SSM_HEAD_DIM, (2 * hp + 2) * SSM_HEAD_DIM)
            y_ref[:, ps] = y_diag + y_off[:, off] + dsk_ref[:, ps] * xs[:, ps]
    y_ref[...] = _gated_group_norm(y_ref[...], z_ref[...], g_ref[...])

    @pl.when(c == pl.num_programs(1) - 1)
    def _():
        hout_ref[...] = ht_ref[...].T


def _ssd_prompt_call(proj, sp):
    b, s_len, _ = proj.shape
    q = SSD_CHUNK
    const = lambda shape: pl.BlockSpec(shape, lambda bi, c: (0,) * len(shape))
    y, h_last = pl.pallas_call(
        _ssd_prompt_kernel,
        out_shape=(jax.ShapeDtypeStruct((b, s_len, D_SSM), F32),
                   jax.ShapeDtypeStruct((b, D_SSM, D_STATE), F32)),
        grid=(b, s_len // q),
        in_specs=[pl.BlockSpec((None, q, D_SSM), lambda bi, c: (bi, c, 3)),
                  pl.BlockSpec((None, q, 2048), lambda bi, c: (bi, c, 2)),
                  const((CONV_W, CONV_DIM)), const((1, CONV_DIM)), const((1, 128)), const((1, 128)),
                  const((1, D_SSM)), const((1, D_SSM))],
        out_specs=(pl.BlockSpec((None, q, D_SSM), lambda bi, c: (bi, c, 0)),
                   pl.BlockSpec((None, D_SSM, D_STATE), lambda bi, c: (bi, 0, 0))),
        scratch_shapes=[pltpu.VMEM((D_STATE, D_SSM), F32), pltpu.VMEM((8 + q, CONV_DIM), F32)],
        compiler_params=pltpu.CompilerParams(
            dimension_semantics=("parallel", "arbitrary"), vmem_limit_bytes=VMEM_LIMIT_BYTES),
        name="ssd_prompt",
    )(proj, proj, sp['conv_w'], sp['conv_b'], sp['dt_bias'], sp['a_log'], sp['d_skip_x'], sp['g_ssm_out'])
    return y.reshape(b * s_len, D_SSM), h_last.reshape(b, H_SSM, SSM_HEAD_DIM, D_STATE)


def _ssd_sample_kernel(z_ref, xd_ref, buf_ref, h0_ref, cw_ref, cb_ref, dtb_ref, alog_ref, dsk_ref, g_ref,
                       y_ref, hout_ref, ext_ref):
    n_seq, n_tok, _ = z_ref.shape
    expand = _head_lane_mask(128).astype(BF16)
    ti = lax.broadcasted_iota(jnp.int32, (8, 8), 0)
    si = lax.broadcasted_iota(jnp.int32, (8, 8), 1)
    causal = (ti >= si) & (si < n_tok)
    real = lax.broadcasted_iota(jnp.int32, (8, 1), 0) < n_tok
    ones = jnp.ones((8, D_STATE), BF16)
    ext_ref[...] = jnp.zeros_like(ext_ref)

    def one_sequence(j, ext_ref):
        ext_ref[8 - (CONV_W - 1):8, :] = buf_ref[j]
        ext_ref[8:8 + n_tok, :] = xd_ref[j][:, 0:CONV_DIM]
        act = _causal_conv(ext_ref, 8, cw_ref, cb_ref)
        xs = act[:, 0:D_SSM]
        bm = act[:, D_SSM:D_SSM + SSM_GROUPS * D_STATE]
        cm = act[:, D_SSM + SSM_GROUPS * D_STATE:CONV_DIM]
        pad = jnp.zeros((8 - n_tok, 128), F32)
        dt = _softplus(jnp.concatenate([xd_ref[j][:, CONV_DIM:CONV_DIM + 128], pad], axis=0) + dtb_ref[...])
        da = jnp.where(real, dt * (-jnp.exp(alog_ref[...])), 0.0)
        a_cum = None
        for piece in _split_bf16(da, 3):
            t = jnp.dot((ti >= si).astype(BF16), piece, preferred_element_type=F32)
            a_cum = t if a_cum is None else a_cum + t
        a_last = a_cum[n_tok - 1:n_tok, :]
        acum_x = _dot_split(a_cum, expand, 3)
        eac_x = jnp.exp(acum_x)
        dt_x = _dot_split(dt, expand, 2)
        xdt = jnp.where(real, xs * dt_x, 0.0)
        xdd = (xdt * jnp.exp(acum_x[n_tok - 1:n_tok, :] - acum_x)).astype(BF16)
        last_x = jnp.where(lax.broadcasted_iota(jnp.int32, (8, 1), 0) == n_tok - 1, eac_x, 0.0)
        keep = _dot_split(last_x, ones, 2, dims=TN_DIMS)
        y_parts, new_state = [], []
        for grp in range(SSM_GROUPS):
            gs = slice(grp * GROUP_W, (grp + 1) * GROUP_W)
            bm_b = bm[:, grp * D_STATE:(grp + 1) * D_STATE].astype(BF16)
            cm_b = cm[:, grp * D_STATE:(grp + 1) * D_STATE].astype(BF16)
            cb_mat = jnp.where(causal, lax.dot_general(cm_b, bm_b, NT_DIMS, preferred_element_type=F32), 0.0)
            h0_g = h0_ref[j, gs, :]
            y_g = lax.dot_general(cm_b, h0_g.astype(BF16), NT_DIMS, preferred_element_type=F32) * eac_x[:, gs]
            for s in range(n_tok):
                coef = cb_mat[:, s:s + 1] * jnp.exp(jnp.where(ti[:, 0:1] >= s, acum_x[:, gs] - acum_x[s:s + 1, gs], -jnp.inf))
                y_g = y_g + coef * xdt[s:s + 1, gs]
            y_parts.append(y_g)
            states = lax.dot_general(xdd[:, gs], bm_b, TN_DIMS, preferred_element_type=F32)
            hout_ref[j, gs, :] = h0_g * keep[gs, :] + states
        y = jnp.concatenate(y_parts, axis=1) + dsk_ref[...] * xs
        zj = jnp.concatenate([z_ref[j], jnp.zeros((8 - n_tok, D_SSM), F32)], axis=0)
        y_ref[j] = _gated_group_norm(y, zj, g_ref[...])[0:n_tok, :]

    def body(jj, carry):
        for slot in range(2):
            one_sequence(2 * jj + slot, ext_ref.at[slot])
        return carry

    lax.fori_loop(0, n_seq // 2, body, 0)


def _ssd_sample_call(proj, conv_buf, h0, sp):
    b, n_tok, _ = proj.shape
    nseq = 8
    const = lambda shape: pl.BlockSpec(shape, lambda i: (0,) * len(shape))
    y, h_last = pl.pallas_call(
        _ssd_sample_kernel,
        out_shape=(jax.ShapeDtypeStruct((b, n_tok, D_SSM), F32),
                   jax.ShapeDtypeStruct((b, D_SSM, D_STATE), F32)),
        grid=(b // nseq,),
        in_specs=[pl.BlockSpec((nseq, n_tok, D_SSM), lambda i: (i, 0, 3)),
                  pl.BlockSpec((nseq, n_tok, 2048), lambda i: (i, 0, 2)),
                  pl.BlockSpec((nseq, CONV_W - 1, CONV_DIM), lambda i: (i, 0, 0)),
                  pl.BlockSpec((nseq, D_SSM, D_STATE), lambda i: (i, 0, 0)),
                  const((CONV_W, CONV_DIM)), const((1, CONV_DIM)), const((1, 128)), const((1, 128)),
                  const((1, D_SSM)), const((1, D_SSM))],
        out_specs=(pl.BlockSpec((nseq, n_tok, D_SSM), lambda i: (i, 0, 0)),
                   pl.BlockSpec((nseq, D_SSM, D_STATE), lambda i: (i, 0, 0))),
        scratch_shapes=[pltpu.VMEM((2, 16, CONV_DIM), F32)],
        compiler_params=pltpu.CompilerParams(
            dimension_semantics=("parallel",), vmem_limit_bytes=VMEM_LIMIT_BYTES),
        name="ssd_sample",
    )(proj, proj, conv_buf, h0.reshape(b, D_SSM, D_STATE), sp['conv_w'], sp['conv_b'], sp['dt_bias'],
      sp['a_log'], sp['d_skip_x'], sp['g_ssm_out'])
    return y.reshape(b * n_tok, D_SSM), h_last.reshape(b, H_SSM, SSM_HEAD_DIM, D_STATE)


def _split_proj(proj, b, L, keep):
    proj = proj.reshape(b, L, D_IN_PAD)
    k = proj[:, L - keep:, D_ATT:2 * D_ATT].reshape(b, keep, H_ATT, HEAD_DIM)
    v = proj[:, L - keep:, 2 * D_ATT:3 * D_ATT].reshape(b, keep, H_ATT, HEAD_DIM)
    o = 3 * D_ATT + D_SSM
    return proj, k, v, proj[:, L - (CONV_W - 1):, o:o + CONV_DIM]


def kernel(x_prompt, x_sample, cache_k, cache_v, state_ssm, state_conv, c_prompt, c_sample,
           rpb_table, w_ada, b_ada, g_ffn1, w_gate1, w_up1, w_down1, g_mix, w_in, conv_w, conv_b,
           dt_bias, a_log, d_skip, g_ssm_out, g_attn_out, w_out, g_ffn2, w_gate2, w_up2, w_down2, g_final):
    bp, sp, _ = x_prompt.shape
    bs, ls, _ = x_sample.shape
    wbp = min(WINDOW_MAX, sp)
    layer = 0
    w_in_pad = jnp.pad(jnp.transpose(w_in[layer]).astype(BF16), ((0, D_IN_PAD - D_IN), (0, 0)))
    p = {'g_ffn1': g_ffn1[layer], 'g_mix': g_mix[layer], 'w_in_pad': w_in_pad,
         'conv_w': conv_w[layer], 'conv_b': conv_b[layer].reshape(1, CONV_DIM),
         'dt_bias': jnp.pad(dt_bias[layer], (0, 128 - H_SSM)).reshape(1, 128),
         'a_log': jnp.pad(a_log[layer], (0, 128 - H_SSM)).reshape(1, 128),
         'd_skip_x': jnp.repeat(d_skip[layer], SSM_HEAD_DIM).reshape(1, D_SSM),
         'g_ssm_out': g_ssm_out[layer].reshape(1, D_SSM), 'g_attn_out': g_attn_out[layer],
         'w_out_att': w_out[layer, :D_ATT].astype(BF16), 'w_out_ssm': w_out[layer, D_ATT:].astype(BF16),
         'g_ffn2': g_ffn2[layer]}

    n_c = bp + bs
    c_rows = -(-n_c // 16) * 16
    c_all = jnp.pad(jnp.concatenate([c_prompt, c_sample], axis=0), ((0, c_rows - n_c), (0, 0)))
    mod = _mod_call(c_all, w_ada[layer], b_ada[layer])
    mods_p = [mod[:bp, i * D_MODEL:(i + 1) * D_MODEL].reshape(bp, 1, D_MODEL) for i in range(N_MOD)]
    mods_s = [jnp.repeat(mod[bp:n_c, i * D_MODEL:(i + 1) * D_MODEL], ls, axis=0) for i in range(N_MOD)]

    lb_c, lb_n = _sample_log_bias(rpb_table, ls, cache_k.shape[2])
    cache_kt = jnp.transpose(cache_k[layer], (0, 2, 3, 1))
    cache_vt = jnp.transpose(cache_v[layer], (0, 2, 3, 1))
    dils = tuple(d for _, d in DILATED_PATTERNS)
    sh1, sc1, g1, sh2, sc2, g2, sh3, sc3, g3 = mods_p
    th1, tc1, t1, th2, tc2, t2, th3, tc3, t3 = mods_s
    kw_p = dict(tm=1024, rows_per_mod=sp)
    kw_s = dict(tm=bs * ls, rows_per_mod=ls)

    xs, *w1 = _ffn_call(x_sample.reshape(bs * ls, D_MODEL), p['g_ffn1'], th1, tc1, t1,
                        w_gate1[layer], w_up1[layer], w_down1[layer], g_final,
                        final_norm=False, emit_weights=True, **kw_s)
    ffn1 = (p['g_ffn1'], *w1)
    (proj_s,) = _inproj_call(xs, p['g_mix'], th2, tc2, p['w_in_pad'], dilations=(), **kw_s)
    proj_s, kn, vn, cs = _split_proj(proj_s, bs, ls, ls)
    rider = lambda first, n: (proj_s, cache_kt, cache_vt, lb_c, lb_n, first, n)

    xp, att_a = _ffn_call(x_prompt.reshape(bp * sp, D_MODEL), ffn1[0], sh1, sc1, g1, *ffn1[1:], g_final,
                          final_norm=False, rider=rider(0, bs // 2), **kw_p)
    proj_p, *qkv = _inproj_call(xp, p['g_mix'], sh2, sc2, p['w_in_pad'], dilations=dils, **kw_p)
    proj_p, kp, vp, cp = _split_proj(proj_p, bp, sp, wbp)
    outs, lses, w2 = [], [], []
    for (window, dil), qkv_d, w in zip(DILATED_PATTERNS, qkv, (w_gate2[layer], w_up2[layer], w_down2[layer])):
        o, lse, wb = _attn_prompt_call(qkv_d, _prompt_bias(rpb_table, window, dil), dil, bp, w)
        outs.append(o)
        lses.append(lse)
        w2.append(wb)
    ffn2 = (p['g_ffn2'], *w2)
    ssd_p, hp = _ssd_prompt_call(proj_p, p)
    xp = _outproj_call(xp, outs + lses, ssd_p, g2, p['w_out_att'], p['w_out_ssm'], p['g_attn_out'],
                       tm=256, rows_per_mod=sp, dilations=dils)
    yp, att_b = _ffn_call(xp, ffn2[0], sh3, sc3, g3, *ffn2[1:], g_final,
                          final_norm=True, rider=rider(bs // 2, bs - bs // 2), **kw_p)

    att_s = jnp.concatenate([att_a, att_b], axis=0).reshape(bs * ls, D_ATT)
    ssd_s, hs = _ssd_sample_call(proj_s, state_conv[layer], state_ssm[layer], p)
    xs = _outproj_call(xs, [att_s], ssd_s, t2, p['w_out_att'], p['w_out_ssm'], p['g_attn_out'],
                       tm=256, rows_per_mod=ls, dilations=())
    ys = _ffn_call(xs, ffn2[0], th3, tc3, t3, *ffn2[1:], g_final, final_norm=True, **kw_s)
    return (yp.reshape(bp, sp, D_MODEL), ys.reshape(bs, ls, D_MODEL), kp[None], vp[None],
            kn[None], vn[None], hp[None], hs[None], cp[None], cs[None])
```
